```python
import math
import jax, jax.numpy as jnp
from jax import lax
import numpy as np

D_MODEL = 4096
BATCH = 2
SEQ = 8192
DEPTH = 4

MIX_WIDTH = D_MODEL
S5_WIDTH = D_MODEL // 4
S5_GROUP = 16
S5_GROUPS = S5_WIDTH // S5_GROUP
S5_STATE = 64
HG_WIDTH = (3 * D_MODEL) // 8
HG_HEAD = 128
HG_HEADS = HG_WIDTH // HG_HEAD
GLA_WIDTH = MIX_WIDTH - S5_WIDTH - HG_WIDTH
GLA_HEADS = 4
GLA_V_HEAD = GLA_WIDTH // GLA_HEADS
GLA_K_HEAD = GLA_V_HEAD // 2
GLA_K_WIDTH = GLA_HEADS * GLA_K_HEAD
GLA_GATE_RANK = 16
GLA_GATE_TEMP = 16.0
CHUNK = 64
D_FF = ((8 * D_MODEL + 3 * 256 - 1) // (3 * 256)) * 256
N_MOD = 6
LN_EPS = 1e-5
DEEPNORM_ALPHA = (2.0 * DEPTH) ** 0.25
DEEPNORM_BETA = (8.0 * DEPTH) ** -0.25
IN_SIZES = (S5_WIDTH,
            HG_WIDTH, HG_WIDTH, HG_WIDTH, HG_WIDTH,
            GLA_K_WIDTH, GLA_K_WIDTH, GLA_WIDTH,
            GLA_WIDTH, GLA_GATE_RANK)
IN_WIDTH = sum(IN_SIZES)

kernel_name = 'hybrid_s5_hgrn2_gla_deepnorm_adaln'


def _normalize(x, eps=LN_EPS):
    mu = jnp.mean(x, axis=-1, keepdims=True)
    xc = x - mu
    var = jnp.mean(xc * xc, axis=-1, keepdims=True)
    return xc * lax.rsqrt(var + eps)


def layer_norm(x, gain, bias):
    y = _normalize(x.astype(jnp.float32)) * gain.astype(jnp.float32) + bias.astype(jnp.float32)
    return y.astype(x.dtype)


def chunked_gated_linear_attention(q, k, v, log_g):
    bsz, seq, heads, dk = q.shape
    dv = v.shape[-1]
    n_chunks = seq // CHUNK

    def to_chunks(t):
        return t.reshape(bsz, n_chunks, CHUNK, heads, t.shape[-1]).transpose(1, 0, 3, 2, 4)

    causal = jnp.tril(jnp.ones((CHUNK, CHUNK), dtype=bool))[:, :, None]

    def step(state, inp):
        qc, kc, vc, gc = inp
        b = jnp.cumsum(gc, axis=2)
        diff = b[:, :, :, None, :] - b[:, :, None, :, :]
        decay = jnp.exp(jnp.where(causal, diff, -jnp.inf))
        scores = jnp.einsum('bhijd,bhjd->bhij', qc[:, :, :, None, :] * decay, kc)
        o = jnp.einsum('bhij,bhje->bhie', scores, vc) \
            + jnp.einsum('bhid,bhde->bhie', qc * jnp.exp(b), state)
        b_last = b[:, :, -1:, :]
        state = jnp.exp(b_last)[:, :, 0, :, None] * state \
            + jnp.einsum('bhjd,bhje->bhde', kc * jnp.exp(b_last - b), vc)
        return state, o

    state0 = jnp.zeros((bsz, heads, dk, dv), jnp.float32)
    _, o = lax.scan(step, state0, (to_chunks(q), to_chunks(k), to_chunks(v), to_chunks(log_g)))
    return o.transpose(1, 0, 3, 2, 4).reshape(bsz, seq, heads, dv)


def s5_mixer(u, a_re, a_im, log_dt, b_re, b_im, c_re, c_im, d_skip, glu_w, glu_b):
    f32 = jnp.float32
    bsz, seq, _ = u.shape
    uf = u.astype(f32).reshape(bsz, seq, S5_GROUPS, S5_GROUP)
    ar = a_re.astype(f32)
    ai = a_im.astype(f32)
    dt = jnp.exp(log_dt.astype(f32))[:, None]
    mag = jnp.exp(ar * dt)
    lam_re = mag * jnp.cos(ai * dt)
    lam_im = mag * jnp.sin(ai * dt)
    den = ar * ar + ai * ai
    nr = lam_re - 1.0
    ni = lam_im
    coef_re = ((nr * ar + ni * ai) / den)[..., None]
    coef_im = ((ni * ar - nr * ai) / den)[..., None]
    br = b_re.astype(f32)
    bi = b_im.astype(f32)
    bbar_re = coef_re * br - coef_im * bi
    bbar_im = coef_re * bi + coef_im * br
    bu_re = jnp.einsum('blgh,gph->blgp', uf, bbar_re)
    bu_im = jnp.einsum('blgh,gph->blgp', uf, bbar_im)
    lam_re_seq = jnp.broadcast_to(lam_re[None, None], (1, seq, S5_GROUPS, S5_STATE))
    lam_im_seq = jnp.broadcast_to(lam_im[None, None], (1, seq, S5_GROUPS, S5_STATE))

    def combine(left, right):
        lr1, li1, xr1, xi1 = left
        lr2, li2, xr2, xi2 = right
        return (lr2 * lr1 - li2 * li1,
                lr2 * li1 + li2 * lr1,
                lr2 * xr1 - li2 * xi1 + xr2,
                lr2 * xi1 + li2 * xr1 + xi2)

    _, _, xs_re, xs_im = lax.associative_scan(
        combine, (lam_re_seq, lam_im_seq, bu_re, bu_im), axis=1)
    y = jnp.einsum('blgp,ghp->blgh', xs_re, c_re.astype(f32)) \
        - jnp.einsum('blgp,ghp->blgh', xs_im, c_im.astype(f32)) \
        + uf * d_skip.astype(f32).reshape(S5_GROUPS, S5_GROUP)
    z = jax.nn.gelu(y.reshape(bsz, seq, S5_WIDTH))
    return z * jax.nn.sigmoid(z @ glu_w.astype(f32) + glu_b.astype(f32))


def hgrn2_mixer(q_in, f_in, i_in, g_in, lower_bound, gain, bias):
    f32 = jnp.float32
    bsz, seq, _ = q_in.shape
    shp = (bsz, seq, HG_HEADS, HG_HEAD)
    q = jax.nn.silu(q_in.astype(f32)).reshape(shp)
    f = lower_bound + (1.0 - lower_bound) * jax.nn.sigmoid(f_in.astype(f32))
    k = (1.0 - f).reshape(shp)
    log_f = jnp.log(f).reshape(shp)
    o = chunked_gated_linear_attention(q, k, i_in.astype(f32).reshape(shp), log_f)
    o = _normalize(jax.nn.sigmoid(g_in.astype(f32)).reshape(shp) * o)
    return o.reshape(bsz, seq, HG_WIDTH) * gain.astype(f32) + bias.astype(f32)


def gla_mixer(q_in, k_in, v_in, g_in, gate_lr, w_gate, b_gate, gain):
    f32 = jnp.float32
    bsz, seq, _ = q_in.shape
    kshp = (bsz, seq, GLA_HEADS, GLA_K_HEAD)
    vshp = (bsz, seq, GLA_HEADS, GLA_V_HEAD)
    q = q_in.astype(f32).reshape(kshp) * (GLA_K_HEAD ** -0.5)
    k = k_in.astype(f32).reshape(kshp)
    v = v_in.astype(f32).reshape(vshp)
    log_a = jax.nn.log_sigmoid(gate_lr.astype(f32) @ w_gate.astype(f32) + b_gate.astype(f32)) / GLA_GATE_TEMP
    o = chunked_gated_linear_attention(q, k, v, log_a.reshape(kshp))
    o = o * lax.rsqrt(jnp.mean(o * o, axis=-1, keepdims=True) + LN_EPS)
    return o.reshape(bsz, seq, GLA_WIDTH) * gain.astype(f32) * jax.nn.silu(g_in.astype(f32))


def setup_inputs(seed: int = 0) -> dict:
    key = jax.random.key(seed)
    ks = jax.random.split(key, 40)
    f32 = jnp.float32

    def nrm(i, shape, scale):
        return jax.random.normal(ks[i], shape, f32) * scale

    L = DEPTH
    return {
        'x': nrm(0, (BATCH, SEQ, D_MODEL), 1.0),
        'c': nrm(1, (BATCH, D_MODEL), 1.0),
        'w_ada': nrm(2, (D_MODEL, N_MOD * D_MODEL), 0.1 * D_MODEL ** -0.5),
        'b_ada': nrm(3, (N_MOD * D_MODEL,), 0.01),
        'ada_table': nrm(4, (L, N_MOD, D_MODEL), 0.02),
        'w_in': nrm(5, (L, D_MODEL, IN_WIDTH), D_MODEL ** -0.5),
        'w_out': nrm(6, (L, MIX_WIDTH, D_MODEL), DEEPNORM_BETA * MIX_WIDTH ** -0.5),
        's5_a_re': -0.5 * jnp.exp(nrm(7, (L, S5_GROUPS, S5_STATE), 0.05)),
        's5_a_im': jnp.pi * jnp.arange(S5_STATE, dtype=f32)[None, None, :]
                   + nrm(8, (L, S5_GROUPS, S5_STATE), 0.01),
        's5_log_dt': jax.random.uniform(ks[9], (L, S5_GROUPS), f32,
                                        minval=math.log(1e-3), maxval=math.log(1e-1)),
        's5_b_re': nrm(10, (L, S5_GROUPS, S5_STATE, S5_GROUP), (2.0 * S5_GROUP) ** -0.5),
        's5_b_im': nrm(11, (L, S5_GROUPS, S5_STATE, S5_GROUP), (2.0 * S5_GROUP) ** -0.5),
        's5_c_re': nrm(12, (L, S5_GROUPS, S5_GROUP, S5_STATE), (2.0 * S5_STATE) ** -0.5),
        's5_c_im': nrm(13, (L, S5_GROUPS, S5_GROUP, S5_STATE), (2.0 * S5_STATE) ** -0.5),
        's5_d': nrm(14, (L, S5_WIDTH), 1.0),
        's5_glu_w': nrm(15, (L, S5_WIDTH, S5_WIDTH), S5_WIDTH ** -0.5),
        's5_glu_b': nrm(16, (L, S5_WIDTH), 0.01),
        'hg_lb_raw': nrm(17, (L, HG_WIDTH), 1.0),
        'hg_norm_gain': 1.0 + nrm(18, (L, HG_WIDTH), 0.01),
        'hg_norm_bias': nrm(19, (L, HG_WIDTH), 0.01),
        'gla_w_gate': nrm(20, (L, GLA_GATE_RANK, GLA_K_WIDTH), GLA_GATE_RANK ** -0.5),
        'gla_b_gate': nrm(21, (L, GLA_K_WIDTH), 0.01),
        'gla_norm_gain': 1.0 + nrm(22, (L, GLA_WIDTH), 0.01),
        'w_ffn_gate': nrm(23, (L, D_MODEL, D_FF), D_MODEL ** -0.5),
        'w_ffn_up': nrm(24, (L, D_MODEL, D_FF), D_MODEL ** -0.5),
        'w_ffn_down': nrm(25, (L, D_FF, D_MODEL), DEEPNORM_BETA * D_FF ** -0.5),
        'ln1_gain': 1.0 + nrm(26, (L, D_MODEL), 0.01),
        'ln1_bias': nrm(27, (L, D_MODEL), 0.01),
        'ln2_gain': 1.0 + nrm(28, (L, D_MODEL), 0.01),
        'ln2_bias': nrm(29, (L, D_MODEL), 0.01),
    }


def reference(x, c, w_ada, b_ada, ada_table, w_in, w_out,
              s5_a_re, s5_a_im, s5_log_dt, s5_b_re, s5_b_im, s5_c_re, s5_c_im, s5_d,
              s5_glu_w, s5_glu_b, hg_lb_raw, hg_norm_gain, hg_norm_bias,
              gla_w_gate, gla_b_gate, gla_norm_gain,
              w_ffn_gate, w_ffn_up, w_ffn_down, ln1_gain, ln1_bias, ln2_gain, ln2_bias):
    bsz = x.shape[0]
    cond = (jax.nn.silu(c) @ w_ada + b_ada).reshape(bsz, N_MOD, D_MODEL)
    lb_cum = jnp.cumsum(jax.nn.softmax(hg_lb_raw.astype(jnp.float32), axis=0), axis=0)
    lower_bounds = lb_cum - lb_cum[0:1]
    split_points = np.cumsum(IN_SIZES)[:-1].tolist()

    for l in range(DEPTH):
        mod = (cond + ada_table[l][None]).astype(x.dtype)
        shift1, scale1, gate1 = mod[:, 0, None, :], mod[:, 1, None, :], mod[:, 2, None, :]
        shift2, scale2, gate2 = mod[:, 3, None, :], mod[:, 4, None, :], mod[:, 5, None, :]

        h = x * (1.0 + scale1) + shift1
        proj = h @ w_in[l]
        (u_a, q_b, f_b, i_b, g_b, q_c, k_c, v_c, g_c, lr_c) = jnp.split(proj, split_points, axis=-1)
        y_a = s5_mixer(u_a, s5_a_re[l], s5_a_im[l], s5_log_dt[l], s5_b_re[l], s5_b_im[l],
                       s5_c_re[l], s5_c_im[l], s5_d[l], s5_glu_w[l], s5_glu_b[l])
        y_b = hgrn2_mixer(q_b, f_b, i_b, g_b, lower_bounds[l], hg_norm_gain[l], hg_norm_bias[l])
        y_c = gla_mixer(q_c, k_c, v_c, g_c, lr_c, gla_w_gate[l], gla_b_gate[l], gla_norm_gain[l])
        mixed = jnp.concatenate([y_a, y_b, y_c], axis=-1).astype(x.dtype) @ w_out[l]
        x = layer_norm(DEEPNORM_ALPHA * x + (1.0 + gate1) * mixed, ln1_gain[l], ln1_bias[l])

        h = x * (1.0 + scale2) + shift2
        ffn = (jax.nn.silu(h @ w_ffn_gate[l]) * (h @ w_ffn_up[l])) @ w_ffn_down[l]
        x = layer_norm(DEEPNORM_ALPHA * x + (1.0 + gate2) * ffn, ln2_gain[l], ln2_bias[l])
    return x
```

```python
import functools
import math

import numpy as np
import jax
import jax.numpy as jnp
from jax import lax
from jax.experimental import pallas as pl
from jax.experimental.pallas import tpu as pltpu

F32 = jnp.float32
BF16 = jnp.bfloat16

LANES = 128
V7X_VMEM_BYTES = 64 * 1024 * 1024
VMEM_LIMIT = V7X_VMEM_BYTES - 8 * 1024 * 1024

S5_GROUP = 16
HG_HEAD = 128
GLA_HEADS = 4
GLA_GATE_TEMP = 16.0
N_MOD = 6
LN_EPS = 1e-5

S5_T = 16
GLA_CHUNK = 64
GLA_DK_PAD = 256


def _cparams(n_axes):
    return pltpu.CompilerParams(
        dimension_semantics=("arbitrary",) * n_axes, vmem_limit_bytes=VMEM_LIMIT)


def _sigmoid(x):
    return 1.0 / (1.0 + jnp.exp(-x))


def _dot(a, b):
    return jnp.dot(a, b, preferred_element_type=F32)


def _dot_nt(a, b, precision=None):
    return lax.dot_general(a, b, (((1,), (1,)), ((), ())),
                           preferred_element_type=F32, precision=precision)


def _dot_tn(a, b):
    return lax.dot_general(a, b, (((0,), (0,)), ((), ())), preferred_element_type=F32)


def _pick_tile(n, cap):
    best = None
    for t in range(LANES, min(n, cap) + 1, LANES):
        if n % t == 0:
            best = t
    assert best is not None, (n, cap)
    return best


def _mm_kernel(a_ref, w_ref, o_ref):
    o_ref[...] = _dot(a_ref[...], w_ref[...]).astype(o_ref.dtype)


def matmul(a, w, out_dtype, tm=512, tn_cap=1024):
    m, k = a.shape
    n = w.shape[1]
    tn = _pick_tile(n, tn_cap)
    tm = min(tm, m)
    return pl.pallas_call(
        _mm_kernel,
        grid=(n // tn, m // tm),
        in_specs=[pl.BlockSpec((tm, k), lambda j, i: (i, 0)),
                  pl.BlockSpec((k, tn), lambda j, i: (0, j))],
        out_specs=pl.BlockSpec((tm, tn), lambda j, i: (i, j)),
        out_shape=jax.ShapeDtypeStruct((m, n), out_dtype),
        compiler_params=_cparams(2),
        name="matmul",
    )(a, w)


def _ffn_up_kernel(a_ref, wg_ref, wu_ref, o_ref):
    a = a_ref[...]
    g = _dot(a, wg_ref[...])
    u = _dot(a, wu_ref[...])
    o_ref[...] = (g * _sigmoid(g) * u).astype(o_ref.dtype)


def ffn_up(h, wg, wu, tm=512, tn_cap=1024):
    m, k = h.shape
    n = wg.shape[1]
    tn = _pick_tile(n, tn_cap)
    tm = min(tm, m)
    return pl.pallas_call(
        _ffn_up_kernel,
        grid=(n // tn, m // tm),
        in_specs=[pl.BlockSpec((tm, k), lambda j, i: (i, 0)),
                  pl.BlockSpec((k, tn), lambda j, i: (0, j)),
                  pl.BlockSpec((k, tn), lambda j, i: (0, j))],
        out_specs=pl.BlockSpec((tm, tn), lambda j, i: (i, j)),
        out_shape=jax.ShapeDtypeStruct((m, n), BF16),
        compiler_params=_cparams(2),
        name="ffn_up",
    )(h, wg, wu)


def _cond_kernel(c_ref, w_ref, b_ref, tab_ref, o_ref):
    c = c_ref[...]
    act = (c * _sigmoid(c)).astype(BF16)
    cond = _dot(act, w_ref[...].astype(BF16)) + b_ref[...]
    for l in range(tab_ref.shape[0]):
        o_ref[l] = cond + tab_ref[l:l + 1, :]


def cond_table(c_pad, w_ada, b_ada, ada_table2, tn=1024):
    rows, d = c_pad.shape
    n = w_ada.shape[1]
    depth = ada_table2.shape[0]
    tn = _pick_tile(n, tn)
    return pl.pallas_call(
        _cond_kernel,
        grid=(n // tn,),
        in_specs=[pl.BlockSpec((rows, d), lambda j: (0, 0)),
                  pl.BlockSpec((d, tn), lambda j: (0, j)),
                  pl.BlockSpec((1, tn), lambda j: (0, j)),
                  pl.BlockSpec((depth, tn), lambda j: (0, j))],
        out_specs=pl.BlockSpec((depth, rows, tn), lambda j: (0, 0, j)),
        out_shape=jax.ShapeDtypeStruct((depth, rows, n), F32),
        compiler_params=_cparams(1),
        name="cond_table",
    )(c_pad, w_ada, b_ada, ada_table2)


def _modulate_kernel(x_ref, mod_ref, h_ref):
    m = mod_ref[0]
    h_ref[...] = (x_ref[...] * (1.0 + m[1:2, :]) + m[0:1, :]).astype(h_ref.dtype)


def modulate(x2, mod_l, seq, tm=512):
    n, d = x2.shape
    tm = min(tm, seq)
    per_b = seq // tm
    return pl.pallas_call(
        _modulate_kernel,
        grid=(n // tm,),
        in_specs=[pl.BlockSpec((tm, d), lambda i: (i, 0)),
                  pl.BlockSpec((1, N_MOD, d), lambda i: (i // per_b, 0, 0))],
        out_specs=pl.BlockSpec((tm, d), lambda i: (i, 0)),
        out_shape=jax.ShapeDtypeStruct((n, d), BF16),
        compiler_params=_cparams(1),
        name="modulate",
    )(x2, mod_l)


def _ln_mod_kernel(x_ref, mm_ref, mod_ref, nmod_ref, gain_ref, bias_ref, xo_ref, *h_refs,
                   alpha, gate_row, next_row):
    m = mod_ref[0]
    z = alpha * x_ref[...] + (1.0 + m[gate_row:gate_row + 1, :]) * mm_ref[...]
    mu = jnp.mean(z, axis=-1, keepdims=True)
    zc = z - mu
    var = jnp.mean(zc * zc, axis=-1, keepdims=True)
    y = zc * lax.rsqrt(var + LN_EPS) * gain_ref[...] + bias_ref[...]
    xo_ref[...] = y
    if h_refs:
        nm = nmod_ref[0]
        h_refs[0][...] = (y * (1.0 + nm[next_row + 1:next_row + 2, :])
                          + nm[next_row:next_row + 1, :]).astype(BF16)


def ln_mod(x2, mm, mod_l, mod_next, gain, bias, seq, *, alpha, gate_row, next_row, with_h, tm=256):
    n, d = x2.shape
    tm = min(tm, seq)
    per_b = seq // tm
    row = pl.BlockSpec((tm, d), lambda i: (i, 0))
    modspec = pl.BlockSpec((1, N_MOD, d), lambda i: (i // per_b, 0, 0))
    vec = pl.BlockSpec((1, d), lambda i: (0, 0))
    out_shape = [jax.ShapeDtypeStruct((n, d), F32)]
    out_specs = [row]
    if with_h:
        out_shape.append(jax.ShapeDtypeStruct((n, d), BF16))
        out_specs.append(row)
    res = pl.pallas_call(
        functools.partial(_ln_mod_kernel, alpha=alpha, gate_row=gate_row, next_row=next_row),
        grid=(n // tm,),
        in_specs=[row, row, modspec, modspec, vec, vec],
        out_specs=out_specs,
        out_shape=out_shape,
        compiler_params=_cparams(1),
        name="ln_mod",
    )(x2, mm, mod_l, mod_next, gain, bias)
    return (res[0], res[1]) if with_h else (res[0], None)


def _gla_levels(chunk):
    lv = []
    h = chunk // 2
    while h >= 1:
        lv.append(h)
        h //= 2
    return lv


def _gla_sum_matrix(chunk):
    blocks = []
    idx = np.arange(chunk)
    for h in _gla_levels(chunk):
        m = np.zeros((chunk, chunk), np.float32)
        for i in range(chunk):
            r = (i // (2 * h)) * 2 * h + h - 1
            if i % (2 * h) >= h:
                m[i, r + 1:i + 1] = 1.0
            else:
                m[i, i + 1:r + 1] = 1.0
        blocks.append(m)
    blocks.append((idx[None, :] <= idx[:, None]).astype(np.float32))
    blocks.append((idx[None, :] > idx[:, None]).astype(np.float32))
    return np.concatenate(blocks, axis=0)


def _gla_chunk_heads(q, k, v, g, s_ref, msum, chunk, dk, dv, heads):
    levels = _gla_levels(chunk)
    g_hi = g.astype(BF16)
    r1 = g - g_hi.astype(F32)
    g_mid = r1.astype(BF16)
    g_lo = (r1 - g_mid.astype(F32)).astype(BF16)
    expo = _dot(msum, g_hi) + _dot(msum, g_mid) + _dot(msum, g_lo)
    e_all = jnp.exp(expo)

    row = lax.broadcasted_iota(jnp.int32, (chunk, chunk), 0)
    col = lax.broadcasted_iota(jnp.int32, (chunk, chunk), 1)
    outs = []
    for hd in range(heads):
        ks = slice(hd * dk, (hd + 1) * dk)
        qh, kh = q[:, ks], k[:, ks]
        vh = v[:, hd * dv:(hd + 1) * dv].astype(BF16)
        a = jnp.where(row == col, _dot_nt(qh.astype(BF16), kh.astype(BF16)), 0.0)
        for li, h in enumerate(levels):
            e = e_all[li * chunk:(li + 1) * chunk, ks]
            p = _dot_nt((qh * e).astype(BF16), (kh * e).astype(BF16))
            blk = 2 * h
            if blk == chunk:
                mask = (row >= h) & (col < h)
            else:
                mask = ((row // blk) == (col // blk)) & ((row % blk) >= h) & ((col % blk) < h)
            a = jnp.where(mask, p, a)
        nl = len(levels)
        e_cum = e_all[nl * chunk:(nl + 1) * chunk, ks]
        e_rev = e_all[(nl + 1) * chunk:(nl + 2) * chunk, ks]
        s_t = s_ref[hd]
        o = _dot(a.astype(BF16), vh) + _dot_nt((qh * e_cum).astype(BF16), s_t.astype(BF16))
        kb = (kh * e_rev).astype(BF16)
        s_ref[hd] = s_t * e_cum[chunk - 1:chunk, :] + _dot_tn(vh, kb)
        outs.append(o)
    return outs


def _hgrn_kernel(q_ref, f_ref, i_ref, g_ref, lbraw_ref, gain_ref, bias_ref, msum_ref,
                 o_ref, s_ref, *, layer, heads, chunk):
    dk = dv = HG_HEAD
    tb = q_ref.shape[0]

    @pl.when(pl.program_id(2) == 0)
    def _():
        s_ref[...] = jnp.zeros_like(s_ref)

    raw = lbraw_ref[...]
    ex = jnp.exp(raw - jnp.max(raw, axis=0, keepdims=True))
    sm = ex / jnp.sum(ex, axis=0, keepdims=True)
    lb = jnp.zeros_like(sm[0:1, :])
    for m in range(1, layer + 1):
        lb = lb + sm[m:m + 1, :]
    gain = gain_ref[...]
    bias = bias_ref[...]
    msum = msum_ref[...]

    def body(c, carry):
        r0 = pl.multiple_of(c * chunk, chunk)
        rows = pl.ds(r0, chunk)
        q_in = q_ref[rows, :]
        f = lb + (1.0 - lb) * _sigmoid(f_ref[rows, :])
        q = q_in * _sigmoid(q_in)
        outs = _gla_chunk_heads(q, 1.0 - f, i_ref[rows, :], jnp.log(f), s_ref, msum,
                                chunk, dk, dv, heads)
        gate = _sigmoid(g_ref[rows, :])
        for hd, o in enumerate(outs):
            cs = slice(hd * dv, (hd + 1) * dv)
            y = gate[:, cs] * o
            mu = jnp.mean(y, axis=-1, keepdims=True)
            yc = y - mu
            var = jnp.mean(yc * yc, axis=-1, keepdims=True)
            yn = yc * lax.rsqrt(var + LN_EPS)
            o_ref[rows, cs] = (yn * gain[:, cs] + bias[:, cs]).astype(o_ref.dtype)
        return carry

    lax.fori_loop(0, tb // chunk, body, 0)


def hgrn2(proj, lb_raw, gain, bias, *, layer, batch, seq, heads_per_block=4, tb=256):
    n = proj.shape[0]
    w = proj.shape[1] // 4
    nheads = w // HG_HEAD
    hpb = heads_per_block
    while nheads % hpb:
        hpb -= 1
    bw = hpb * HG_HEAD
    ngrp = nheads // hpb
    tb = min(tb, seq)
    chunk = min(GLA_CHUNK, tb)
    nt = seq // tb
    msum = jnp.asarray(_gla_sum_matrix(chunk), BF16)

    def sec(s):
        return pl.BlockSpec((tb, bw), lambda b, hg, t: (b * nt + t, s * ngrp + hg))

    vec = pl.BlockSpec((1, bw), lambda b, hg, t: (0, hg))
    return pl.pallas_call(
        functools.partial(_hgrn_kernel, layer=layer, heads=hpb, chunk=chunk),
        grid=(batch, ngrp, nt),
        in_specs=[sec(0), sec(1), sec(2), sec(3),
                  pl.BlockSpec((lb_raw.shape[0], bw), lambda b, hg, t: (0, hg)),
                  vec, vec,
                  pl.BlockSpec(msum.shape, lambda b, hg, t: (0, 0))],
        out_specs=pl.BlockSpec((tb, bw), lambda b, hg, t: (b * nt + t, hg)),
        out_shape=jax.ShapeDtypeStruct((n, w), BF16),
        scratch_shapes=[pltpu.VMEM((hpb, HG_HEAD, HG_HEAD), F32)],
        compiler_params=_cparams(3),
        name="hgrn2",
    )(proj, proj, proj, proj, lb_raw, gain, bias, msum)


def _gla_kernel(q_ref, k_ref, v_ref, g_ref, lr_ref, wg_ref, bg_ref, gain_ref, msum_ref,
                o_ref, s_ref, *, heads, chunk, dk, dv, q_scale):
    tb = q_ref.shape[0]

    @pl.when(pl.program_id(2) == 0)
    def _():
        s_ref[...] = jnp.zeros_like(s_ref)

    wg = wg_ref[...]
    bg = bg_ref[...]
    gain = gain_ref[...]
    msum = msum_ref[...]

    def body(c, carry):
        r0 = pl.multiple_of(c * chunk, chunk)
        rows = pl.ds(r0, chunk)
        pre = _dot(lr_ref[rows, :].astype(BF16), wg) + bg
        log_a = (jnp.minimum(pre, 0.0) - jnp.log(1.0 + jnp.exp(-jnp.abs(pre)))) / GLA_GATE_TEMP
        outs = _gla_chunk_heads(q_ref[rows, :] * q_scale, k_ref[rows, :], v_ref[rows, :], log_a,
                                s_ref, msum, chunk, dk, dv, heads)
        g_in = g_ref[rows, :]
        swish = g_in * _sigmoid(g_in)
        for hd, o in enumerate(outs):
            cs = slice(hd * dv, (hd + 1) * dv)
            y = o * lax.rsqrt(jnp.mean(o * o, axis=-1, keepdims=True) + LN_EPS)
            o_ref[rows, cs] = (y * gain[:, cs] * swish[:, cs]).astype(o_ref.dtype)
        return carry

    lax.fori_loop(0, tb // chunk, body, 0)


def gla(qk, vg, lr, w_gate, b_gate, gain, *, batch, seq, dk, dv, q_scale, heads_per_block=1, tb=256):
    n = qk.shape[0]
    nheads = qk.shape[1] // (2 * dk)
    hpb = heads_per_block
    ngrp = nheads // hpb
    tb = min(tb, seq)
    chunk = min(GLA_CHUNK, tb)
    nt = seq // tb
    msum = jnp.asarray(_gla_sum_matrix(chunk), BF16)
    kw, vw = hpb * dk, hpb * dv

    def rows(width, s):
        return pl.BlockSpec((tb, width), lambda b, hg, t: (b * nt + t, s * ngrp + hg))

    return pl.pallas_call(
        functools.partial(_gla_kernel, heads=hpb, chunk=chunk, dk=dk, dv=dv, q_scale=q_scale),
        grid=(batch, ngrp, nt),
        in_specs=[rows(kw, 0), rows(kw, 1), rows(vw, 0), rows(vw, 1),
                  pl.BlockSpec((tb, lr.shape[1]), lambda b, hg, t: (b * nt + t, 0)),
                  pl.BlockSpec((w_gate.shape[0], kw), lambda b, hg, t: (0, hg)),
                  pl.BlockSpec((1, kw), lambda b, hg, t: (0, hg)),
                  pl.BlockSpec((1, vw), lambda b, hg, t: (0, hg)),
                  pl.BlockSpec(msum.shape, lambda b, hg, t: (0, 0))],
        out_specs=pl.BlockSpec((tb, vw), lambda b, hg, t: (b * nt + t, hg)),
        out_shape=jax.ShapeDtypeStruct((n, nheads * dv), BF16),
        scratch_shapes=[pltpu.VMEM((hpb, dv, dk), F32)],
        compiler_params=_cparams(3),
        name="gla",
    )(qk, qk, vg, vg, lr, w_gate, b_gate, gain, msum)


def _s5_scan_steps(nchunks):
    return max(1, int(math.ceil(math.log2(nchunks)))) if nchunks > 1 else 0


def _s5_prep_kernel(ar_ref, ai_ref, ldt_ref, b2_ref, c2_ref, d_ref,
                    kbig_ref, win_ref, wout_ref, lscan_ref, *, nsteps):
    p2 = ar_ref.shape[-1]
    half = p2 // 2
    t_sub = S5_T
    hgrp = b2_ref.shape[1]
    ar = ar_ref[0]
    ai = ai_ref[0]
    dt = jnp.exp(ldt_ref[0])
    lane = lax.broadcasted_iota(jnp.int32, (1, p2), 1)
    sgn_im = jnp.where(lane < half, -1.0, 1.0)
    sgn_re = -sgn_im

    nrow = ((t_sub + 1 + 7) // 8) * 8
    kf = lax.broadcasted_iota(jnp.int32, (nrow, p2), 0).astype(F32)
    mag = jnp.exp(kf * (ar * dt))
    th = kf * (ai * dt)
    l_re = mag * jnp.cos(th)
    l_im = mag * jnp.sin(th)
    l_sw = l_im * sgn_im

    def cmul(x, kpow):
        return (x * l_re[kpow:kpow + 1, :]
                + pltpu.roll(x, half, axis=1) * l_sw[kpow:kpow + 1, :])

    lam_re = l_re[1:2, :]
    lam_im = l_im[1:2, :]
    den = ar * ar + ai * ai
    nr = lam_re - 1.0
    ni = lam_im
    coef_re = (nr * ar + ni * ai) / den
    coef_im = (ni * ar - nr * ai) / den
    b2 = b2_ref[0]
    bbar = b2 * coef_re + pltpu.roll(b2, half, axis=1) * (coef_im * sgn_im)
    c2 = c2_ref[0]

    win_ref[0] = jnp.concatenate([cmul(bbar, t_sub - 1 - s) for s in range(t_sub)],
                                 axis=0).astype(win_ref.dtype)
    wout_ref[0] = jnp.concatenate([cmul(c2, i + 1) * sgn_re for i in range(t_sub)],
                                  axis=0).astype(wout_ref.dtype)

    n = t_sub * hgrp
    rt = lax.broadcasted_iota(jnp.int32, (n, n), 0) // hgrp
    ct = lax.broadcasted_iota(jnp.int32, (n, n), 1) // hgrp
    zero = jnp.zeros_like(c2)
    bbar_s = bbar * sgn_re
    cl = jnp.concatenate([c2] * t_sub, axis=0)
    bl = jnp.concatenate([bbar_s] * t_sub, axis=0)
    hp = lax.Precision.HIGHEST
    kbig = jnp.where(rt == ct, _dot_nt(cl, bl, hp), 0.0)
    h = t_sub // 2
    while h >= 1:
        blk = 2 * h
        cl = jnp.concatenate([cmul(c2, (t % blk) - h + 1) if (t % blk) >= h else zero
                              for t in range(t_sub)], axis=0)
        bl = jnp.concatenate([cmul(bbar, h - 1 - (s % blk)) * sgn_re if (s % blk) < h else zero
                              for s in range(t_sub)], axis=0)
        p = _dot_nt(cl, bl, hp)
        if blk == t_sub:
            kbig = kbig + p
        else:
            kbig = kbig + jnp.where((rt // blk) == (ct // blk), p, 0.0)
        h //= 2
    r = lax.broadcasted_iota(jnp.int32, (n, n), 0)
    c = lax.broadcasted_iota(jnp.int32, (n, n), 1)
    kbig = kbig + jnp.where(r == c, d_ref[0], 0.0)
    kbig_ref[0] = kbig.astype(kbig_ref.dtype)

    cur_re = l_re[t_sub:t_sub + 1, :]
    cur_im = l_im[t_sub:t_sub + 1, :]
    rows_re, rows_sw = [], []
    for _ in range(nsteps):
        rows_re.append(cur_re)
        rows_sw.append(cur_im * sgn_im)
        cur_re, cur_im = cur_re * cur_re - cur_im * cur_im, 2.0 * cur_re * cur_im
    pad = lscan_ref.shape[1] - 2 * nsteps
    parts = rows_re + rows_sw + ([jnp.zeros((pad, p2), F32)] if pad else [])
    lscan_ref[0] = jnp.concatenate(parts, axis=0)


def s5_prep(ar2, ai2, ldt2, b2, c2, drow, nsteps):
    g, _, p2 = ar2.shape
    hgrp = b2.shape[1]
    n = S5_T * hgrp
    lrows = ((2 * nsteps + 7) // 8) * 8
    vec = pl.BlockSpec((1, 1, p2), lambda i: (i, 0, 0))
    mat = pl.BlockSpec((1, hgrp, p2), lambda i: (i, 0, 0))
    return pl.pallas_call(
        functools.partial(_s5_prep_kernel, nsteps=nsteps),
        grid=(g,),
        in_specs=[vec, vec, vec, mat, mat, pl.BlockSpec((1, 1, n), lambda i: (i, 0, 0))],
        out_specs=[pl.BlockSpec((1, n, n), lambda i: (i, 0, 0)),
                   pl.BlockSpec((1, n, p2), lambda i: (i, 0, 0)),
                   pl.BlockSpec((1, n, p2), lambda i: (i, 0, 0)),
                   pl.BlockSpec((1, lrows, p2), lambda i: (i, 0, 0))],
        out_shape=[jax.ShapeDtypeStruct((g, n, n), BF16),
                   jax.ShapeDtypeStruct((g, n, p2), BF16),
                   jax.ShapeDtypeStruct((g, n, p2), BF16),
                   jax.ShapeDtypeStruct((g, lrows, p2), F32)],
        compiler_params=_cparams(1),
        name="s5_prep",
    )(ar2, ai2, ldt2, b2, c2, drow)


def _s5_main_kernel(u_ref, kbig_ref, win_ref, wout_ref, lscan_ref, y_ref, *, nsteps, per_batch):
    u = u_ref[0]
    nrows = u.shape[0]
    p2 = win_ref.shape[-1]
    half = p2 // 2
    x = _dot(u, win_ref[0])
    pos = lax.broadcasted_iota(jnp.int32, (nrows, p2), 0) % per_batch
    lscan = lscan_ref[0]
    for j in range(nsteps):
        d = 1 << j
        sh = jnp.where(pos >= d, pltpu.roll(x, d, axis=0), 0.0)
        x = x + sh * lscan[j:j + 1, :] + pltpu.roll(sh, half, axis=1) * lscan[nsteps + j:nsteps + j + 1, :]
    xprev = jnp.where(pos >= 1, pltpu.roll(x, 1, axis=0), 0.0)
    y_ref[0] = _dot_nt(u, kbig_ref[0]) + _dot_nt(xprev.astype(BF16), wout_ref[0])


def s5_main(uf, kbig, win, wout, lscan, *, nsteps, per_batch):
    g, nrows, n = uf.shape
    p2 = win.shape[-1]

    def spec(a):
        return pl.BlockSpec((1,) + a.shape[1:], lambda i: (i, 0, 0))

    return pl.pallas_call(
        functools.partial(_s5_main_kernel, nsteps=nsteps, per_batch=per_batch),
        grid=(g,),
        in_specs=[spec(uf), spec(kbig), spec(win), spec(wout), spec(lscan)],
        out_specs=pl.BlockSpec((1, nrows, n), lambda i: (i, 0, 0)),
        out_shape=jax.ShapeDtypeStruct((g, nrows, n), F32),
        compiler_params=_cparams(1),
        name="s5_main",
    )(uf, kbig, win, wout, lscan)


def _s5_glu_kernel(y_ref, w_ref, b_ref, o_ref):
    y = y_ref[...]
    z = 0.5 * y * (1.0 + jnp.tanh(math.sqrt(2.0 / math.pi) * (y + 0.044715 * (y * y * y))))
    gate = _dot(z.astype(BF16), w_ref[...]) + b_ref[...]
    o_ref[...] = (z * _sigmoid(gate)).astype(o_ref.dtype)


def s5_glu(y, w, b, tm=512):
    n, wd = y.shape
    tm = min(tm, n)
    return pl.pallas_call(
        _s5_glu_kernel,
        grid=(n // tm,),
        in_specs=[pl.BlockSpec((tm, wd), lambda i: (i, 0)),
                  pl.BlockSpec((wd, wd), lambda i: (0, 0)),
                  pl.BlockSpec((1, wd), lambda i: (0, 0))],
        out_specs=pl.BlockSpec((tm, wd), lambda i: (i, 0)),
        out_shape=jax.ShapeDtypeStruct((n, wd), BF16),
        compiler_params=_cparams(1),
        name="s5_glu",
    )(y, w, b)


def s5_mixer(u, a_re, a_im, log_dt, b_re, b_im, c_re, c_im, d_skip, glu_w, glu_b, *, batch, seq):
    n, wd = u.shape
    g, p = a_re.shape
    hgrp = wd // g
    assert hgrp == S5_GROUP and seq % S5_T == 0
    per_batch = seq // S5_T
    nsteps = _s5_scan_steps(per_batch)
    dup = lambda a: jnp.concatenate([a, a], axis=-1)[:, None, :]
    ar2, ai2 = dup(a_re), dup(a_im)
    ldt2 = jnp.broadcast_to(log_dt[:, None, None], (g, 1, 2 * p))
    b2 = jnp.concatenate([b_re.transpose(0, 2, 1), b_im.transpose(0, 2, 1)], axis=-1)
    c2 = jnp.concatenate([c_re, c_im], axis=-1)
    drow = jnp.tile(d_skip.reshape(g, 1, hgrp), (1, 1, S5_T))
    kbig, win, wout, lscan = s5_prep(ar2, ai2, ldt2, b2, c2, drow, nsteps)
    nr = n // S5_T
    uf = u.reshape(nr, S5_T, g, hgrp).transpose(2, 0, 1, 3).reshape(g, nr, S5_T * hgrp).astype(BF16)
    yf = s5_main(uf, kbig, win, wout, lscan, nsteps=nsteps, per_batch=per_batch)
    y = yf.reshape(g, nr, S5_T, hgrp).transpose(1, 2, 0, 3).reshape(n, wd)
    return s5_glu(y, glu_w.astype(BF16), glu_b[None, :])


def _pad_cols(w, new):
    return jnp.pad(w, ((0, 0), (0, new - w.shape[1])))


def _pad_heads(w, heads, width, new):
    r = w.shape[0]
    return jnp.pad(w.reshape(r, heads, width), ((0, 0), (0, 0), (0, new - width))).reshape(r, heads * new)


def kernel(x, c, w_ada, b_ada, ada_table, w_in, w_out, s5_a_re, s5_a_im, s5_log_dt, s5_b_re, s5_b_im, s5_c_re, s5_c_im, s5_d, s5_glu_w, s5_glu_b, hg_lb_raw, hg_norm_gain, hg_norm_bias, gla_w_gate, gla_b_gate, gla_norm_gain, w_ffn_gate, w_ffn_up, w_ffn_down, ln1_gain, ln1_bias, ln2_gain, ln2_bias):
    bsz, seq, d = x.shape
    depth = w_in.shape[0]
    n = bsz * seq
    s5_w = s5_d.shape[1]
    hg_w = hg_lb_raw.shape[1]
    gla_kw = gla_b_gate.shape[1]
    gla_vw = gla_norm_gain.shape[1]
    rank = gla_w_gate.shape[1]
    gla_dk = gla_kw // GLA_HEADS
    gla_dv = gla_vw // GLA_HEADS
    dk_pad = ((gla_dk + GLA_DK_PAD - 1) // GLA_DK_PAD) * GLA_DK_PAD
    d_ff = w_ffn_gate.shape[2]
    ff_pad = ((d_ff + 1023) // 1024) * 1024
    alpha = (2.0 * depth) ** 0.25

    rows = ((bsz + 7) // 8) * 8
    c_pad = jnp.pad(c, ((0, rows - bsz), (0, 0)))
    mod = cond_table(c_pad, w_ada, b_ada[None, :], ada_table.reshape(depth, N_MOD * d))
    mod = mod[:, :bsz].reshape(depth, bsz, N_MOD, d)

    o_u = 0
    o_hg = o_u + s5_w
    o_q = o_hg + 4 * hg_w
    o_k = o_q + gla_kw
    o_v = o_k + gla_kw
    o_lr = o_v + 2 * gla_vw

    x2 = x.reshape(n, d)
    h = modulate(x2, mod[0], seq)
    for l in range(depth):
        wl = w_in[l]
        u_a = matmul(h, wl[:, o_u:o_hg].astype(BF16), F32)
        p_b = matmul(h, wl[:, o_hg:o_q].astype(BF16), F32)
        w_qk = jnp.concatenate([_pad_heads(wl[:, o_q:o_k], GLA_HEADS, gla_dk, dk_pad),
                                _pad_heads(wl[:, o_k:o_v], GLA_HEADS, gla_dk, dk_pad)], axis=1)
        qk_c = matmul(h, w_qk.astype(BF16), F32)
        vg_c = matmul(h, wl[:, o_v:o_lr].astype(BF16), F32)
        lr_c = matmul(h, _pad_cols(wl[:, o_lr:], LANES).astype(BF16), F32)

        y_a = s5_mixer(u_a, s5_a_re[l], s5_a_im[l], s5_log_dt[l], s5_b_re[l], s5_b_im[l],
                       s5_c_re[l], s5_c_im[l], s5_d[l], s5_glu_w[l], s5_glu_b[l],
                       batch=bsz, seq=seq)
        y_b = hgrn2(p_b, hg_lb_raw, hg_norm_gain[l][None, :], hg_norm_bias[l][None, :],
                    layer=l, batch=bsz, seq=seq)
        w_gate = jnp.pad(_pad_heads(gla_w_gate[l], GLA_HEADS, gla_dk, dk_pad),
                         ((0, LANES - rank), (0, 0))).astype(BF16)
        b_gate = _pad_heads(gla_b_gate[l][None, :], GLA_HEADS, gla_dk, dk_pad)
        y_c = gla(qk_c, vg_c, lr_c, w_gate, b_gate, gla_norm_gain[l][None, :],
                  batch=bsz, seq=seq, dk=dk_pad, dv=gla_dv, q_scale=float(gla_dk) ** -0.5)
        mixed_in = jnp.concatenate([y_a, y_b, y_c], axis=1)
        mixed = matmul(mixed_in, w_out[l].astype(BF16), F32)
        x2, h = ln_mod(x2, mixed, mod[l], mod[l], ln1_gain[l][None, :], ln1_bias[l][None, :], seq,
                       alpha=alpha, gate_row=2, next_row=3, with_h=True)

        act = ffn_up(h, _pad_cols(w_ffn_gate[l], ff_pad).astype(BF16),
                     _pad_cols(w_ffn_up[l], ff_pad).astype(BF16))
        w_dn = jnp.pad(w_ffn_down[l], ((0, ff_pad - d_ff), (0, 0))).astype(BF16)
        ffn = matmul(act, w_dn, F32, tn_cap=512)
        last = l == depth - 1
        x2, h = ln_mod(x2, ffn, mod[l], mod[l if last else l + 1],
                       ln2_gain[l][None, :], ln2_bias[l][None, :], seq,
                       alpha=alpha, gate_row=5, next_row=0, with_h=not last)
    return x2.reshape(bsz, seq, d)
```

```python
import functools
import math

import numpy as np
import jax
import jax.numpy as jnp
from jax import lax
from jax.experimental import pallas as pl
from jax.experimental.pallas import tpu as pltpu

F32 = jnp.float32
BF16 = jnp.bfloat16

LANES = 128
V7X_VMEM_BYTES = 64 * 1024 * 1024
VMEM_LIMIT = V7X_VMEM_BYTES - 8 * 1024 * 1024

S5_GROUP = 16
HG_HEAD = 128
GLA_HEADS = 4
GLA_GATE_TEMP = 16.0
N_MOD = 6
LN_EPS = 1e-5

S5_T = 16
GLA_CHUNK = 64
GLA_DK_PAD = 256


def _cparams(n_axes):
    return pltpu.CompilerParams(
        dimension_semantics=("arbitrary",) * n_axes, vmem_limit_bytes=VMEM_LIMIT)


def _sigmoid(x):
    return 1.0 / (1.0 + jnp.exp(-x))


def _dot(a, b):
    return jnp.dot(a, b, preferred_element_type=F32)


def _dot_nt(a, b, precision=None):
    return lax.dot_general(a, b, (((1,), (1,)), ((), ())),
                           preferred_element_type=F32, precision=precision)


def _dot_tn(a, b):
    return lax.dot_general(a, b, (((0,), (0,)), ((), ())), preferred_element_type=F32)


def _pick_tile(n, cap):
    best = None
    for t in range(LANES, min(n, cap) + 1, LANES):
        if n % t == 0:
            best = t
    assert best is not None, (n, cap)
    return best


def _mm_kernel(a_ref, w_ref, o_ref):
    o_ref[...] = _dot(a_ref[...], w_ref[...]).astype(o_ref.dtype)


def matmul(a, w, out_dtype, tm=512, tn_cap=1024):
    m, k = a.shape
    n = w.shape[1]
    tn = _pick_tile(n, tn_cap)
    tm = min(tm, m)
    return pl.pallas_call(
        _mm_kernel,
        grid=(n // tn, m // tm),
        in_specs=[pl.BlockSpec((tm, k), lambda j, i: (i, 0)),
                  pl.BlockSpec((k, tn), lambda j, i: (0, j))],
        out_specs=pl.BlockSpec((tm, tn), lambda j, i: (i, j)),
        out_shape=jax.ShapeDtypeStruct((m, n), out_dtype),
        compiler_params=_cparams(2),
        name="matmul",
    )(a, w)


def _mm_w32_kernel(*refs, n_a):
    a_refs, w_ref, o_ref, wbf_ref = refs[:n_a], refs[n_a], refs[n_a + 1], refs[n_a + 2]

    @pl.when(pl.program_id(1) == 0)
    def _():
        wbf_ref[...] = w_ref[...].astype(BF16)

    acc = None
    r0 = 0
    for a_ref in a_refs:
        k = a_ref.shape[1]
        part = _dot(a_ref[...], wbf_ref[r0:r0 + k, :])
        acc = part if acc is None else acc + part
        r0 += k
    o_ref[...] = acc.astype(o_ref.dtype)


def matmul_w32(a_list, w_stack, layer, col0, ncols, out_dtype, tm=512, tn=512):
    m = a_list[0].shape[0]
    k = w_stack.shape[1]
    assert sum(a.shape[1] for a in a_list) == k
    tn = _pick_tile(math.gcd(ncols, col0) if col0 else ncols, tn)
    tm = min(tm, m)
    cb = col0 // tn
    return pl.pallas_call(
        functools.partial(_mm_w32_kernel, n_a=len(a_list)),
        grid=(ncols // tn, m // tm),
        in_specs=[pl.BlockSpec((tm, a.shape[1]), lambda j, i: (i, 0)) for a in a_list]
        + [pl.BlockSpec((None, k, tn), lambda j, i: (layer, 0, cb + j))],
        out_specs=pl.BlockSpec((tm, tn), lambda j, i: (i, j)),
        out_shape=jax.ShapeDtypeStruct((m, ncols), out_dtype),
        scratch_shapes=[pltpu.VMEM((k, tn), BF16)],
        compiler_params=_cparams(2),
        name="matmul_w32",
    )(*a_list, w_stack)


def _ffn_up_kernel(a_ref, wg_ref, wu_ref, o_ref, wgb_ref, wub_ref):
    @pl.when(pl.program_id(1) == 0)
    def _():
        wgb_ref[...] = wg_ref[...].astype(BF16)
        wub_ref[...] = wu_ref[...].astype(BF16)

    a = a_ref[...]
    g = _dot(a, wgb_ref[...])
    u = _dot(a, wub_ref[...])
    o_ref[...] = (g * _sigmoid(g) * u).astype(o_ref.dtype)


def ffn_up(h, wg_stack, wu_stack, layer, tm=1024, tn=256):
    m, k = h.shape
    n = wg_stack.shape[2]
    tn = _pick_tile(n, tn)
    tm = min(tm, m)
    wspec = pl.BlockSpec((None, k, tn), lambda j, i: (layer, 0, j))
    return pl.pallas_call(
        _ffn_up_kernel,
        grid=(n // tn, m // tm),
        in_specs=[pl.BlockSpec((tm, k), lambda j, i: (i, 0)), wspec, wspec],
        out_specs=pl.BlockSpec((tm, tn), lambda j, i: (i, j)),
        out_shape=jax.ShapeDtypeStruct((m, n), BF16),
        scratch_shapes=[pltpu.VMEM((k, tn), BF16), pltpu.VMEM((k, tn), BF16)],
        compiler_params=_cparams(2),
        name="ffn_up",
    )(h, wg_stack, wu_stack)


def _cond_kernel(c_ref, w_ref, b_ref, tab_ref, o_ref):
    c = c_ref[...]
    act = (c * _sigmoid(c)).astype(BF16)
    cond = _dot(act, w_ref[...].astype(BF16)) + b_ref[...]
    for l in range(tab_ref.shape[0]):
        o_ref[l] = cond + tab_ref[l:l + 1, :]


def cond_table(c_pad, w_ada, b_ada, ada_table2, tn=1024):
    rows, d = c_pad.shape
    n = w_ada.shape[1]
    depth = ada_table2.shape[0]
    tn = _pick_tile(n, tn)
    return pl.pallas_call(
        _cond_kernel,
        grid=(n // tn,),
        in_specs=[pl.BlockSpec((rows, d), lambda j: (0, 0)),
                  pl.BlockSpec((d, tn), lambda j: (0, j)),
                  pl.BlockSpec((1, tn), lambda j: (0, j)),
                  pl.BlockSpec((depth, tn), lambda j: (0, j))],
        out_specs=pl.BlockSpec((depth, rows, tn), lambda j: (0, 0, j)),
        out_shape=jax.ShapeDtypeStruct((depth, rows, n), F32),
        compiler_params=_cparams(1),
        name="cond_table",
    )(c_pad, w_ada, b_ada, ada_table2)


def _modulate_kernel(x_ref, mod_ref, h_ref):
    m = mod_ref[0]
    h_ref[...] = (x_ref[...] * (1.0 + m[1:2, :]) + m[0:1, :]).astype(h_ref.dtype)


def modulate(x2, mod_l, seq, tm=512):
    n, d = x2.shape
    tm = min(tm, seq)
    per_b = seq // tm
    return pl.pallas_call(
        _modulate_kernel,
        grid=(n // tm,),
        in_specs=[pl.BlockSpec((tm, d), lambda i: (i, 0)),
                  pl.BlockSpec((1, N_MOD, d), lambda i: (i // per_b, 0, 0))],
        out_specs=pl.BlockSpec((tm, d), lambda i: (i, 0)),
        out_shape=jax.ShapeDtypeStruct((n, d), BF16),
        compiler_params=_cparams(1),
        name="modulate",
    )(x2, mod_l)


def _ln_mod_kernel(x_ref, mm_ref, mod_ref, nmod_ref, gain_ref, bias_ref, xo_ref, *h_refs,
                   alpha, gate_row, next_row):
    m = mod_ref[0]
    z = alpha * x_ref[...] + (1.0 + m[gate_row:gate_row + 1, :]) * mm_ref[...]
    mu = jnp.mean(z, axis=-1, keepdims=True)
    zc = z - mu
    var = jnp.mean(zc * zc, axis=-1, keepdims=True)
    y = zc * lax.rsqrt(var + LN_EPS) * gain_ref[...] + bias_ref[...]
    xo_ref[...] = y
    if h_refs:
        nm = nmod_ref[0]
        h_refs[0][...] = (y * (1.0 + nm[next_row + 1:next_row + 2, :])
                          + nm[next_row:next_row + 1, :]).astype(BF16)


def ln_mod(x2, mm, mod_l, mod_next, gain, bias, seq, *, alpha, gate_row, next_row, with_h, tm=256):
    n, d = x2.shape
    tm = min(tm, seq)
    per_b = seq // tm
    row = pl.BlockSpec((tm, d), lambda i: (i, 0))
    modspec = pl.BlockSpec((1, N_MOD, d), lambda i: (i // per_b, 0, 0))
    vec = pl.BlockSpec((1, d), lambda i: (0, 0))
    out_shape = [jax.ShapeDtypeStruct((n, d), F32)]
    out_specs = [row]
    if with_h:
        out_shape.append(jax.ShapeDtypeStruct((n, d), BF16))
        out_specs.append(row)
    res = pl.pallas_call(
        functools.partial(_ln_mod_kernel, alpha=alpha, gate_row=gate_row, next_row=next_row),
        grid=(n // tm,),
        in_specs=[row, row, modspec, modspec, vec, vec],
        out_specs=out_specs,
        out_shape=out_shape,
        compiler_params=_cparams(1),
        name="ln_mod",
    )(x2, mm, mod_l, mod_next, gain, bias)
    return (res[0], res[1]) if with_h else (res[0], None)


def _gla_levels(chunk):
    lv = []
    h = chunk // 2
    while h >= 1:
        lv.append(h)
        h //= 2
    return lv


def _gla_sum_matrix(chunk):
    blocks = []
    idx = np.arange(chunk)
    for h in _gla_levels(chunk):
        m = np.zeros((chunk, chunk), np.float32)
        for i in range(chunk):
            r = (i // (2 * h)) * 2 * h + h - 1
            if i % (2 * h) >= h:
                m[i, r + 1:i + 1] = 1.0
            else:
                m[i, i + 1:r + 1] = 1.0
        blocks.append(m)
    blocks.append((idx[None, :] <= idx[:, None]).astype(np.float32))
    blocks.append((idx[None, :] > idx[:, None]).astype(np.float32))
    m = np.concatenate(blocks, axis=0)
    return np.concatenate([m, m], axis=1)


def _gla_chunk_heads(q, k, v, g, s_ref, msum, chunk, dk, dv, heads):
    levels = _gla_levels(chunk)
    g_hi = g.astype(BF16)
    g_lo = (g - g_hi.astype(F32)).astype(BF16)
    expo = _dot(msum, jnp.concatenate([g_hi, g_lo], axis=0))
    e_all = jnp.exp(expo)

    row = lax.broadcasted_iota(jnp.int32, (chunk, chunk), 0)
    col = lax.broadcasted_iota(jnp.int32, (chunk, chunk), 1)
    outs = []
    for hd in range(heads):
        ks = slice(hd * dk, (hd + 1) * dk)
        qh, kh = q[:, ks], k[:, ks]
        vh = v[:, hd * dv:(hd + 1) * dv].astype(BF16)
        a = jnp.where(row == col, _dot_nt(qh.astype(BF16), kh.astype(BF16)), 0.0)
        for li, h in enumerate(levels):
            e = e_all[li * chunk:(li + 1) * chunk, ks]
            p = _dot_nt((qh * e).astype(BF16), (kh * e).astype(BF16))
            blk = 2 * h
            if blk == chunk:
                mask = (row >= h) & (col < h)
            else:
                mask = ((row // blk) == (col // blk)) & ((row % blk) >= h) & ((col % blk) < h)
            a = jnp.where(mask, p, a)
        nl = len(levels)
        e_cum = e_all[nl * chunk:(nl + 1) * chunk, ks]
        e_rev = e_all[(nl + 1) * chunk:(nl + 2) * chunk, ks]
        s_t = s_ref[hd]
        o = _dot(a.astype(BF16), vh) + _dot_nt((qh * e_cum).astype(BF16), s_t.astype(BF16))
        kb = (kh * e_rev).astype(BF16)
        s_ref[hd] = s_t * e_cum[chunk - 1:chunk, :] + _dot_tn(vh, kb)
        outs.append(o)
    return outs


def _hgrn_kernel(q_ref, f_ref, i_ref, g_ref, lbraw_ref, gain_ref, bias_ref, msum_ref,
                 o_ref, s_ref, *, layer, heads, chunk):
    dk = dv = HG_HEAD
    tb = q_ref.shape[0]

    @pl.when(pl.program_id(2) == 0)
    def _():
        s_ref[...] = jnp.zeros_like(s_ref)

    raw = lbraw_ref[...]
    ex = jnp.exp(raw - jnp.max(raw, axis=0, keepdims=True))
    sm = ex / jnp.sum(ex, axis=0, keepdims=True)
    lb = jnp.zeros_like(sm[0:1, :])
    for m in range(1, layer + 1):
        lb = lb + sm[m:m + 1, :]
    gain = gain_ref[...]
    bias = bias_ref[...]
    msum = msum_ref[...]

    def body(c, carry):
        rows = pl.ds(c * chunk, chunk)
        q_in = q_ref[rows, :]
        f = lb + (1.0 - lb) * _sigmoid(f_ref[rows, :])
        q = q_in * _sigmoid(q_in)
        outs = _gla_chunk_heads(q, 1.0 - f, i_ref[rows, :], jnp.log(f), s_ref, msum,
                                chunk, dk, dv, heads)
        gate = _sigmoid(g_ref[rows, :])
        for hd, o in enumerate(outs):
            cs = slice(hd * dv, (hd + 1) * dv)
            y = gate[:, cs] * o
            mu = jnp.mean(y, axis=-1, keepdims=True)
            yc = y - mu
            var = jnp.mean(yc * yc, axis=-1, keepdims=True)
            yn = yc * lax.rsqrt(var + LN_EPS)
            o_ref[rows, cs] = (yn * gain[:, cs] + bias[:, cs]).astype(o_ref.dtype)
        return carry

    for c in range(tb // chunk):
        body(c, 0)


def hgrn2(proj, lb_raw, gain, bias, *, layer, batch, seq, heads_per_block=4, tb=256):
    n = proj.shape[0]
    w = proj.shape[1] // 4
    nheads = w // HG_HEAD
    hpb = heads_per_block
    while nheads % hpb:
        hpb -= 1
    bw = hpb * HG_HEAD
    ngrp = nheads // hpb
    tb = min(tb, seq)
    chunk = min(GLA_CHUNK, tb)
    nt = seq // tb
    msum = jnp.asarray(_gla_sum_matrix(chunk), BF16)

    def sec(s):
        return pl.BlockSpec((tb, bw), lambda b, hg, t: (b * nt + t, s * ngrp + hg))

    vec = pl.BlockSpec((1, bw), lambda b, hg, t: (0, hg))
    return pl.pallas_call(
        functools.partial(_hgrn_kernel, layer=layer, heads=hpb, chunk=chunk),
        grid=(batch, ngrp, nt),
        in_specs=[sec(0), sec(1), sec(2), sec(3),
                  pl.BlockSpec((lb_raw.shape[0], bw), lambda b, hg, t: (0, hg)),
                  vec, vec,
                  pl.BlockSpec(msum.shape, lambda b, hg, t: (0, 0))],
        out_specs=pl.BlockSpec((tb, bw), lambda b, hg, t: (b * nt + t, hg)),
        out_shape=jax.ShapeDtypeStruct((n, w), BF16),
        scratch_shapes=[pltpu.VMEM((hpb, HG_HEAD, HG_HEAD), F32)],
        compiler_params=_cparams(3),
        name="hgrn2",
    )(proj, proj, proj, proj, lb_raw, gain, bias, msum)


def _gla_kernel(q_ref, k_ref, v_ref, g_ref, wg_ref, bg_ref, gain_ref, msum_ref,
                o_ref, s_ref, *, heads, chunk, dk, dv, q_scale):
    tb = q_ref.shape[0]

    @pl.when(pl.program_id(2) == 0)
    def _():
        s_ref[...] = jnp.zeros_like(s_ref)

    wg = wg_ref[...]
    bg = bg_ref[...]
    gain = gain_ref[...]
    msum = msum_ref[...]

    def body(c, carry):
        rows = pl.ds(c * chunk, chunk)
        q_raw = q_ref[rows, :]
        q_bf = q_raw.astype(BF16)
        pre = jnp.concatenate([_dot(q_bf[:, hd * dk:(hd + 1) * dk], wg[:, hd * dk:(hd + 1) * dk])
                               for hd in range(heads)], axis=1) + bg
        log_a = (jnp.minimum(pre, 0.0) - jnp.log(1.0 + jnp.exp(-jnp.abs(pre)))) / GLA_GATE_TEMP
        outs = _gla_chunk_heads(q_raw * q_scale, k_ref[rows, :], v_ref[rows, :], log_a,
                                s_ref, msum, chunk, dk, dv, heads)
        g_in = g_ref[rows, :]
        swish = g_in * _sigmoid(g_in)
        for hd, o in enumerate(outs):
            cs = slice(hd * dv, (hd + 1) * dv)
            y = o * lax.rsqrt(jnp.mean(o * o, axis=-1, keepdims=True) + LN_EPS)
            o_ref[rows, cs] = (y * gain[:, cs] * swish[:, cs]).astype(o_ref.dtype)
        return carry

    for c in range(tb // chunk):
        body(c, 0)


def gla(qk, vg, w_gate, b_gate, gain, *, batch, seq, dk, dv, q_scale, heads_per_block=1, tb=256):
    n = qk.shape[0]
    nheads = qk.shape[1] // (2 * dk)
    hpb = heads_per_block
    ngrp = nheads // hpb
    tb = min(tb, seq)
    chunk = min(GLA_CHUNK, tb)
    nt = seq // tb
    msum = jnp.asarray(_gla_sum_matrix(chunk), BF16)
    kw, vw = hpb * dk, hpb * dv

    def rows(width, s):
        return pl.BlockSpec((tb, width), lambda b, hg, t: (b * nt + t, s * ngrp + hg))

    return pl.pallas_call(
        functools.partial(_gla_kernel, heads=hpb, chunk=chunk, dk=dk, dv=dv, q_scale=q_scale),
        grid=(batch, ngrp, nt),
        in_specs=[rows(kw, 0), rows(kw, 1), rows(vw, 0), rows(vw, 1),
                  pl.BlockSpec((w_gate.shape[0], kw), lambda b, hg, t: (0, hg)),
                  pl.BlockSpec((1, kw), lambda b, hg, t: (0, hg)),
                  pl.BlockSpec((1, vw), lambda b, hg, t: (0, hg)),
                  pl.BlockSpec(msum.shape, lambda b, hg, t: (0, 0))],
        out_specs=pl.BlockSpec((tb, vw), lambda b, hg, t: (b * nt + t, hg)),
        out_shape=jax.ShapeDtypeStruct((n, nheads * dv), BF16),
        scratch_shapes=[pltpu.VMEM((hpb, dv, dk), F32)],
        compiler_params=_cparams(3),
        name="gla",
    )(qk, qk, vg, vg, w_gate, b_gate, gain, msum)


def _s5_scan_steps(nchunks):
    return max(1, int(math.ceil(math.log2(nchunks)))) if nchunks > 1 else 0


def _s5_prep_kernel(ar_ref, ai_ref, ldt_ref, b2_ref, c2_ref, d_ref,
                    kbig_ref, win_ref, wout_ref, lscan_ref, *, nsteps):
    p2 = ar_ref.shape[-1]
    half = p2 // 2
    t_sub = S5_T
    hgrp = b2_ref.shape[1]
    ar = ar_ref[0]
    ai = ai_ref[0]
    dt = jnp.exp(ldt_ref[0])
    lane = lax.broadcasted_iota(jnp.int32, (1, p2), 1)
    sgn_im = jnp.where(lane < half, -1.0, 1.0)
    sgn_re = -sgn_im

    nrow = ((t_sub + 1 + 7) // 8) * 8
    kf = lax.broadcasted_iota(jnp.int32, (nrow, p2), 0).astype(F32)
    mag = jnp.exp(kf * (ar * dt))
    th = kf * (ai * dt)
    l_re = mag * jnp.cos(th)
    l_im = mag * jnp.sin(th)
    l_sw = l_im * sgn_im

    def cmul(x, kpow):
        return (x * l_re[kpow:kpow + 1, :]
                + pltpu.roll(x, half, axis=1) * l_sw[kpow:kpow + 1, :])

    lam_re = l_re[1:2, :]
    lam_im = l_im[1:2, :]
    den = ar * ar + ai * ai
    nr = lam_re - 1.0
    ni = lam_im
    coef_re = (nr * ar + ni * ai) / den
    coef_im = (ni * ar - nr * ai) / den
    b2 = b2_ref[0]
    bbar = b2 * coef_re + pltpu.roll(b2, half, axis=1) * (coef_im * sgn_im)
    c2 = c2_ref[0]

    win_ref[0] = jnp.concatenate([cmul(bbar, t_sub - 1 - s) for s in range(t_sub)],
                                 axis=0).astype(win_ref.dtype)
    wout_ref[0] = jnp.concatenate([cmul(c2, i + 1) * sgn_re for i in range(t_sub)],
                                  axis=0).astype(wout_ref.dtype)

    n = t_sub * hgrp
    rt = lax.broadcasted_iota(jnp.int32, (n, n), 0) // hgrp
    ct = lax.broadcasted_iota(jnp.int32, (n, n), 1) // hgrp
    zero = jnp.zeros_like(c2)
    bbar_s = bbar * sgn_re
    cl = jnp.concatenate([c2] * t_sub, axis=0)
    bl = jnp.concatenate([bbar_s] * t_sub, axis=0)
    hp = lax.Precision.HIGHEST
    kbig = jnp.where(rt == ct, _dot_nt(cl, bl, hp), 0.0)
    h = t_sub // 2
    while h >= 1:
        blk = 2 * h
        cl = jnp.concatenate([cmul(c2, (t % blk) - h + 1) if (t % blk) >= h else zero
                              for t in range(t_sub)], axis=0)
        bl = jnp.concatenate([cmul(bbar, h - 1 - (s % blk)) * sgn_re if (s % blk) < h else zero
                              for s in range(t_sub)], axis=0)
        p = _dot_nt(cl, bl, hp)
        if blk == t_sub:
            kbig = kbig + p
        else:
            kbig = kbig + jnp.where((rt // blk) == (ct // blk), p, 0.0)
        h //= 2
    r = lax.broadcasted_iota(jnp.int32, (n, n), 0)
    c = lax.broadcasted_iota(jnp.int32, (n, n), 1)
    kbig = kbig + jnp.where(r == c, d_ref[0], 0.0)
    kbig_ref[0] = kbig.astype(kbig_ref.dtype)

    cur_re = l_re[t_sub:t_sub + 1, :]
    cur_im = l_im[t_sub:t_sub + 1, :]
    rows_re, rows_sw = [], []
    for _ in range(nsteps):
        rows_re.append(cur_re)
        rows_sw.append(cur_im * sgn_im)
        cur_re, cur_im = cur_re * cur_re - cur_im * cur_im, 2.0 * cur_re * cur_im
    pad = lscan_ref.shape[1] - 2 * nsteps
    parts = rows_re + rows_sw + ([jnp.zeros((pad, p2), F32)] if pad else [])
    lscan_ref[0] = jnp.concatenate(parts, axis=0)


def s5_prep(ar2, ai2, ldt2, b2, c2, drow, nsteps):
    g, _, p2 = ar2.shape
    hgrp = b2.shape[1]
    n = S5_T * hgrp
    lrows = ((2 * nsteps + 7) // 8) * 8
    vec = pl.BlockSpec((1, 1, p2), lambda i: (i, 0, 0))
    mat = pl.BlockSpec((1, hgrp, p2), lambda i: (i, 0, 0))
    return pl.pallas_call(
        functools.partial(_s5_prep_kernel, nsteps=nsteps),
        grid=(g,),
        in_specs=[vec, vec, vec, mat, mat, pl.BlockSpec((1, 1, n), lambda i: (i, 0, 0))],
        out_specs=[pl.BlockSpec((1, n, n), lambda i: (i, 0, 0)),
                   pl.BlockSpec((1, n, p2), lambda i: (i, 0, 0)),
                   pl.BlockSpec((1, n, p2), lambda i: (i, 0, 0)),
                   pl.BlockSpec((1, lrows, p2), lambda i: (i, 0, 0))],
        out_shape=[jax.ShapeDtypeStruct((g, n, n), BF16),
                   jax.ShapeDtypeStruct((g, n, p2), BF16),
                   jax.ShapeDtypeStruct((g, n, p2), BF16),
                   jax.ShapeDtypeStruct((g, lrows, p2), F32)],
        compiler_params=_cparams(1),
        name="s5_prep",
    )(ar2, ai2, ldt2, b2, c2, drow)


def _s5_main_kernel(u_ref, kbig_ref, win_ref, wout_ref, lscan_ref, y_ref, *, nsteps, per_batch):
    u = u_ref[0]
    nrows = u.shape[0]
    p2 = win_ref.shape[-1]
    half = p2 // 2
    x = _dot(u, win_ref[0])
    pos = lax.broadcasted_iota(jnp.int32, (nrows, p2), 0) % per_batch
    lscan = lscan_ref[0]
    for j in range(nsteps):
        d = 1 << j
        sh = jnp.where(pos >= d, pltpu.roll(x, d, axis=0), 0.0)
        x = x + sh * lscan[j:j + 1, :] + pltpu.roll(sh, half, axis=1) * lscan[nsteps + j:nsteps + j + 1, :]
    xprev = jnp.where(pos >= 1, pltpu.roll(x, 1, axis=0), 0.0)
    y_ref[0] = _dot_nt(u, kbig_ref[0]) + _dot_nt(xprev.astype(BF16), wout_ref[0])


def s5_main(uf, kbig, win, wout, lscan, *, nsteps, per_batch):
    g, nrows, n = uf.shape
    p2 = win.shape[-1]

    def spec(a):
        return pl.BlockSpec((1,) + a.shape[1:], lambda i: (i, 0, 0))

    return pl.pallas_call(
        functools.partial(_s5_main_kernel, nsteps=nsteps, per_batch=per_batch),
        grid=(g,),
        in_specs=[spec(uf), spec(kbig), spec(win), spec(wout), spec(lscan)],
        out_specs=pl.BlockSpec((1, nrows, n), lambda i: (i, 0, 0)),
        out_shape=jax.ShapeDtypeStruct((g, nrows, n), F32),
        compiler_params=_cparams(1),
        name="s5_main",
    )(uf, kbig, win, wout, lscan)


def _s5_glu_kernel(y_ref, w_ref, b_ref, o_ref, wbf_ref):
    @pl.when(pl.program_id(0) == 0)
    def _():
        wbf_ref[...] = w_ref[...].astype(BF16)

    y = y_ref[...]
    z = 0.5 * y * (1.0 + jnp.tanh(math.sqrt(2.0 / math.pi) * (y + 0.044715 * (y * y * y))))
    gate = _dot(z.astype(BF16), wbf_ref[...]) + b_ref[...]
    o_ref[...] = (z * _sigmoid(gate)).astype(o_ref.dtype)


def s5_glu(y, w_stack, layer, b, tm=512):
    n, wd = y.shape
    tm = min(tm, n)
    return pl.pallas_call(
        _s5_glu_kernel,
        grid=(n // tm,),
        in_specs=[pl.BlockSpec((tm, wd), lambda i: (i, 0)),
                  pl.BlockSpec((None, wd, wd), lambda i: (layer, 0, 0)),
                  pl.BlockSpec((1, wd), lambda i: (0, 0))],
        out_specs=pl.BlockSpec((tm, wd), lambda i: (i, 0)),
        out_shape=jax.ShapeDtypeStruct((n, wd), BF16),
        scratch_shapes=[pltpu.VMEM((wd, wd), BF16)],
        compiler_params=_cparams(1),
        name="s5_glu",
    )(y, w_stack, b)


def s5_mixer(u, a_re, a_im, log_dt, b_re, b_im, c_re, c_im, d_skip, glu_w_stack, layer, glu_b,
             *, batch, seq):
    n, wd = u.shape
    g, p = a_re.shape
    hgrp = wd // g
    assert hgrp == S5_GROUP and seq % S5_T == 0
    per_batch = seq // S5_T
    nsteps = _s5_scan_steps(per_batch)
    dup = lambda a: jnp.concatenate([a, a], axis=-1)[:, None, :]
    ar2, ai2 = dup(a_re), dup(a_im)
    ldt2 = jnp.broadcast_to(log_dt[:, None, None], (g, 1, 2 * p))
    b2 = jnp.concatenate([b_re.transpose(0, 2, 1), b_im.transpose(0, 2, 1)], axis=-1)
    c2 = jnp.concatenate([c_re, c_im], axis=-1)
    drow = jnp.tile(d_skip.reshape(g, 1, hgrp), (1, 1, S5_T))
    kbig, win, wout, lscan = s5_prep(ar2, ai2, ldt2, b2, c2, drow, nsteps)
    nr = n // S5_T
    uf = u.reshape(nr, S5_T, g, hgrp).transpose(2, 0, 1, 3).reshape(g, nr, S5_T * hgrp).astype(BF16)
    yf = s5_main(uf, kbig, win, wout, lscan, nsteps=nsteps, per_batch=per_batch)
    y = yf.reshape(g, nr, S5_T, hgrp).transpose(1, 2, 0, 3).reshape(n, wd)
    return s5_glu(y, glu_w_stack, layer, glu_b[None, :])


def _pad_heads(w, heads, width, new):
    r = w.shape[0]
    return jnp.pad(w.reshape(r, heads, width), ((0, 0), (0, 0), (0, new - width))).reshape(r, heads * new)


def kernel(x, c, w_ada, b_ada, ada_table, w_in, w_out, s5_a_re, s5_a_im, s5_log_dt, s5_b_re, s5_b_im, s5_c_re, s5_c_im, s5_d, s5_glu_w, s5_glu_b, hg_lb_raw, hg_norm_gain, hg_norm_bias, gla_w_gate, gla_b_gate, gla_norm_gain, w_ffn_gate, w_ffn_up, w_ffn_down, ln1_gain, ln1_bias, ln2_gain, ln2_bias):
    bsz, seq, d = x.shape
    depth = w_in.shape[0]
    n = bsz * seq
    s5_w = s5_d.shape[1]
    hg_w = hg_lb_raw.shape[1]
    gla_kw = gla_b_gate.shape[1]
    gla_vw = gla_norm_gain.shape[1]
    rank = gla_w_gate.shape[1]
    gla_dk = gla_kw // GLA_HEADS
    gla_dv = gla_vw // GLA_HEADS
    dk_pad = ((gla_dk + GLA_DK_PAD - 1) // GLA_DK_PAD) * GLA_DK_PAD
    assert dk_pad - gla_dk >= rank
    alpha = (2.0 * depth) ** 0.25

    rows = ((bsz + 7) // 8) * 8
    c_pad = jnp.pad(c, ((0, rows - bsz), (0, 0)))
    mod = cond_table(c_pad, w_ada, b_ada[None, :], ada_table.reshape(depth, N_MOD * d))
    mod = mod[:, :bsz].reshape(depth, bsz, N_MOD, d)

    o_u = 0
    o_hg = o_u + s5_w
    o_q = o_hg + 4 * hg_w
    o_k = o_q + gla_kw
    o_v = o_k + gla_kw
    o_lr = o_v + 2 * gla_vw

    x2 = x.reshape(n, d)
    h = modulate(x2, mod[0], seq)
    for l in range(depth):
        wl = w_in[l]
        u_a = matmul_w32([h], w_in, l, o_u, s5_w, F32)
        p_b = matmul_w32([h], w_in, l, o_hg, 4 * hg_w, F32)
        vg_c = matmul_w32([h], w_in, l, o_v, 2 * gla_vw, F32)
        w_q = jnp.concatenate(
            [wl[:, o_q:o_k].reshape(d, GLA_HEADS, gla_dk),
             jnp.broadcast_to(wl[:, None, o_lr:o_lr + rank], (d, GLA_HEADS, rank)),
             jnp.zeros((d, GLA_HEADS, dk_pad - gla_dk - rank), F32)], axis=2).reshape(d, GLA_HEADS * dk_pad)
        w_qk = jnp.concatenate([w_q, _pad_heads(wl[:, o_k:o_v], GLA_HEADS, gla_dk, dk_pad)], axis=1)
        qk_c = matmul(h, w_qk.astype(BF16), F32)

        y_a = s5_mixer(u_a, s5_a_re[l], s5_a_im[l], s5_log_dt[l], s5_b_re[l], s5_b_im[l],
                       s5_c_re[l], s5_c_im[l], s5_d[l], s5_glu_w, l, s5_glu_b[l],
                       batch=bsz, seq=seq)
        y_b = hgrn2(p_b, hg_lb_raw, hg_norm_gain[l][None, :], hg_norm_bias[l][None, :],
                    layer=l, batch=bsz, seq=seq)
        w_gate = jnp.pad(
            jnp.pad(gla_w_gate[l].reshape(rank, GLA_HEADS, gla_dk), ((0, 0), (0, 0), (0, dk_pad - gla_dk))),
            ((gla_dk, dk_pad - gla_dk - rank), (0, 0), (0, 0))).reshape(dk_pad, GLA_HEADS * dk_pad).astype(BF16)
        b_gate = _pad_heads(gla_b_gate[l][None, :], GLA_HEADS, gla_dk, dk_pad)
        y_c = gla(qk_c, vg_c, w_gate, b_gate, gla_norm_gain[l][None, :],
                  batch=bsz, seq=seq, dk=dk_pad, dv=gla_dv, q_scale=float(gla_dk) ** -0.5)
        mixed = matmul_w32([y_a, y_b, y_c], w_out, l, 0, d, F32)
        x2, h = ln_mod(x2, mixed, mod[l], mod[l], ln1_gain[l][None, :], ln1_bias[l][None, :], seq,
                       alpha=alpha, gate_row=2, next_row=3, with_h=True)

        act = ffn_up(h, w_ffn_gate, w_ffn_up, l)
        ffn = matmul(act, w_ffn_down[l].astype(BF16), F32, tn_cap=512)
        last = l == depth - 1
        x2, h = ln_mod(x2, ffn, mod[l], mod[l if last else l + 1],
                       ln2_gain[l][None, :], ln2_bias[l][None, :], seq,
                       alpha=alpha, gate_row=5, next_row=0, with_h=not last)
    return x2.reshape(bsz, seq, d)
```

```python
import functools
import math

import numpy as np
import jax
import jax.numpy as jnp
from jax import lax
from jax.experimental import pallas as pl
from jax.experimental.pallas import tpu as pltpu

F32 = jnp.float32
BF16 = jnp.bfloat16

LANES = 128
V7X_VMEM_BYTES = 64 * 1024 * 1024
VMEM_LIMIT = V7X_VMEM_BYTES - 8 * 1024 * 1024

S5_GROUP = 16
HG_HEAD = 128
GLA_HEADS = 4
GLA_GATE_TEMP = 16.0
N_MOD = 6
LN_EPS = 1e-5

S5_T = 16
GLA_CHUNK = 128
GLA_DK_PAD = 256


def _cparams(n_axes):
    return pltpu.CompilerParams(
        dimension_semantics=("arbitrary",) * n_axes, vmem_limit_bytes=VMEM_LIMIT)


def _sigmoid(x):
    return 1.0 / (1.0 + jnp.exp(-x))


def _dot(a, b):
    return jnp.dot(a, b, preferred_element_type=F32)


def _dot_nt(a, b, precision=None):
    return lax.dot_general(a, b, (((1,), (1,)), ((), ())),
                           preferred_element_type=F32, precision=precision)


def _dot_tn(a, b):
    return lax.dot_general(a, b, (((0,), (0,)), ((), ())), preferred_element_type=F32)


def _pick_tile(n, cap):
    best = None
    for t in range(LANES, min(n, cap) + 1, LANES):
        if n % t == 0:
            best = t
    assert best is not None, (n, cap)
    return best


def _mm_ws_kernel(*refs, n_a):
    a_refs, w_ref, o_ref = refs[:n_a], refs[n_a], refs[n_a + 1]
    acc = None
    r0 = 0
    for a_ref in a_refs:
        k = a_ref.shape[1]
        part = _dot(a_ref[...], w_ref[r0:r0 + k, :])
        acc = part if acc is None else acc + part
        r0 += k
    o_ref[...] = acc.astype(o_ref.dtype)


def matmul_ws(a_list, w_stack, layer, out_dtype, tm=512, tn=1024):
    m = a_list[0].shape[0]
    _, k, n = w_stack.shape
    assert sum(a.shape[1] for a in a_list) == k
    tn = _pick_tile(n, tn)
    tm = min(tm, m)
    return pl.pallas_call(
        functools.partial(_mm_ws_kernel, n_a=len(a_list)),
        grid=(n // tn, m // tm),
        in_specs=[pl.BlockSpec((tm, a.shape[1]), lambda j, i: (i, 0)) for a in a_list]
        + [pl.BlockSpec((None, k, tn), lambda j, i: (layer, 0, j))],
        out_specs=pl.BlockSpec((tm, tn), lambda j, i: (i, j)),
        out_shape=jax.ShapeDtypeStruct((m, n), out_dtype),
        compiler_params=_cparams(2),
        name="matmul_ws",
    )(*a_list, w_stack)


def _ffn_up_kernel(a_ref, wg_ref, wu_ref, o_ref):
    a = a_ref[...]
    g = _dot(a, wg_ref[...])
    u = _dot(a, wu_ref[...])
    o_ref[...] = (g * _sigmoid(g) * u).astype(o_ref.dtype)


def ffn_up(h, wg_stack, wu_stack, layer, tm=2048, tn=256):
    m, k = h.shape
    n = wg_stack.shape[2]
    tn = _pick_tile(n, tn)
    tm = min(tm, m)
    wspec = pl.BlockSpec((None, k, tn), lambda i, j: (layer, 0, j))
    return pl.pallas_call(
        _ffn_up_kernel,
        grid=(m // tm, n // tn),
        in_specs=[pl.BlockSpec((tm, k), lambda i, j: (i, 0)), wspec, wspec],
        out_specs=pl.BlockSpec((tm, tn), lambda i, j: (i, j)),
        out_shape=jax.ShapeDtypeStruct((m, n), BF16),
        compiler_params=_cparams(2),
        name="ffn_up",
    )(h, wg_stack, wu_stack)


def _cond_kernel(c_ref, w_ref, b_ref, tab_ref, o_ref):
    c = c_ref[...]
    act = (c * _sigmoid(c)).astype(BF16)
    cond = _dot(act, w_ref[...].astype(BF16)) + b_ref[...]
    for l in range(tab_ref.shape[0]):
        o_ref[l] = cond + tab_ref[l:l + 1, :]


def cond_table(c_pad, w_ada, b_ada, ada_table2, tn=1024):
    rows, d = c_pad.shape
    n = w_ada.shape[1]
    depth = ada_table2.shape[0]
    tn = _pick_tile(n, tn)
    return pl.pallas_call(
        _cond_kernel,
        grid=(n // tn,),
        in_specs=[pl.BlockSpec((rows, d), lambda j: (0, 0)),
                  pl.BlockSpec((d, tn), lambda j: (0, j)),
                  pl.BlockSpec((1, tn), lambda j: (0, j)),
                  pl.BlockSpec((depth, tn), lambda j: (0, j))],
        out_specs=pl.BlockSpec((depth, rows, tn), lambda j: (0, 0, j)),
        out_shape=jax.ShapeDtypeStruct((depth, rows, n), F32),
        compiler_params=_cparams(1),
        name="cond_table",
    )(c_pad, w_ada, b_ada, ada_table2)


def _modulate_kernel(x_ref, mod_ref, h_ref):
    m = mod_ref[0]
    h_ref[...] = (x_ref[...] * (1.0 + m[1:2, :]) + m[0:1, :]).astype(h_ref.dtype)


def modulate(x2, mod_l, seq, tm=512):
    n, d = x2.shape
    tm = min(tm, seq)
    per_b = seq // tm
    return pl.pallas_call(
        _modulate_kernel,
        grid=(n // tm,),
        in_specs=[pl.BlockSpec((tm, d), lambda i: (i, 0)),
                  pl.BlockSpec((1, N_MOD, d), lambda i: (i // per_b, 0, 0))],
        out_specs=pl.BlockSpec((tm, d), lambda i: (i, 0)),
        out_shape=jax.ShapeDtypeStruct((n, d), BF16),
        compiler_params=_cparams(1),
        name="modulate",
    )(x2, mod_l)


def _ln_mod_kernel(x_ref, mm_ref, mod_ref, nmod_ref, gain_ref, bias_ref, xo_ref, *h_refs,
                   alpha, gate_row, next_row):
    m = mod_ref[0]
    z = alpha * x_ref[...] + (1.0 + m[gate_row:gate_row + 1, :]) * mm_ref[...].astype(F32)
    mu = jnp.mean(z, axis=-1, keepdims=True)
    zc = z - mu
    var = jnp.mean(zc * zc, axis=-1, keepdims=True)
    y = zc * lax.rsqrt(var + LN_EPS) * gain_ref[...] + bias_ref[...]
    xo_ref[...] = y
    if h_refs:
        nm = nmod_ref[0]
        h_refs[0][...] = (y * (1.0 + nm[next_row + 1:next_row + 2, :])
                          + nm[next_row:next_row + 1, :]).astype(BF16)


def ln_mod(x2, mm, mod_l, mod_next, gain, bias, seq, *, alpha, gate_row, next_row, with_h, tm=256):
    n, d = x2.shape
    tm = min(tm, seq)
    per_b = seq // tm
    row = pl.BlockSpec((tm, d), lambda i: (i, 0))
    modspec = pl.BlockSpec((1, N_MOD, d), lambda i: (i // per_b, 0, 0))
    vec = pl.BlockSpec((1, d), lambda i: (0, 0))
    out_shape = [jax.ShapeDtypeStruct((n, d), F32)]
    out_specs = [row]
    if with_h:
        out_shape.append(jax.ShapeDtypeStruct((n, d), BF16))
        out_specs.append(row)
    res = pl.pallas_call(
        functools.partial(_ln_mod_kernel, alpha=alpha, gate_row=gate_row, next_row=next_row),
        grid=(n // tm,),
        in_specs=[row, row, modspec, modspec, vec, vec],
        out_specs=out_specs,
        out_shape=out_shape,
        compiler_params=_cparams(1),
        name="ln_mod",
    )(x2, mm, mod_l, mod_next, gain, bias)
    return (res[0], res[1]) if with_h else (res[0], None)


def _gla_levels(chunk):
    lv = []
    h = chunk // 2
    while h >= 1:
        lv.append(h)
        h //= 2
    return lv


def _gla_sum_matrix(chunk):
    blocks = []
    idx = np.arange(chunk)
    for h in _gla_levels(chunk):
        m = np.zeros((chunk, chunk), np.float32)
        for i in range(chunk):
            r = (i // (2 * h)) * 2 * h + h - 1
            if i % (2 * h) >= h:
                m[i, r + 1:i + 1] = 1.0
            else:
                m[i, i + 1:r + 1] = 1.0
        blocks.append(m)
    blocks.append((idx[None, :] <= idx[:, None]).astype(np.float32))
    blocks.append((idx[None, :] > idx[:, None]).astype(np.float32))
    m = np.concatenate(blocks, axis=0)
    return np.concatenate([m, m], axis=1)


def _gla_chunk_heads(q, k, v, g, s_ref, msum, chunk, dk, dv, heads):
    levels = _gla_levels(chunk)
    g_hi = g.astype(BF16)
    g_lo = (g - g_hi.astype(F32)).astype(BF16)
    expo = _dot(msum, jnp.concatenate([g_hi, g_lo], axis=0))
    e_all = jnp.exp(expo)

    row = lax.broadcasted_iota(jnp.int32, (chunk, chunk), 0)
    col = lax.broadcasted_iota(jnp.int32, (chunk, chunk), 1)
    outs = []
    for hd in range(heads):
        ks = slice(hd * dk, (hd + 1) * dk)
        qh, kh = q[:, ks], k[:, ks]
        vh = v[:, hd * dv:(hd + 1) * dv].astype(BF16)
        a = jnp.where(row == col, _dot_nt(qh.astype(BF16), kh.astype(BF16)), 0.0)
        for li, h in enumerate(levels):
            e = e_all[li * chunk:(li + 1) * chunk, ks]
            p = _dot_nt((qh * e).astype(BF16), (kh * e).astype(BF16))
            blk = 2 * h
            if blk == chunk:
                mask = (row >= h) & (col < h)
            else:
                mask = ((row // blk) == (col // blk)) & ((row % blk) >= h) & ((col % blk) < h)
            a = jnp.where(mask, p, a)
        nl = len(levels)
        e_cum = e_all[nl * chunk:(nl + 1) * chunk, ks]
        e_rev = e_all[(nl + 1) * chunk:(nl + 2) * chunk, ks]
        s_t = s_ref[hd]
        o = _dot(a.astype(BF16), vh) + _dot_nt((qh * e_cum).astype(BF16), s_t.astype(BF16))
        kb = (kh * e_rev).astype(BF16)
        s_ref[hd] = s_t * e_cum[chunk - 1:chunk, :] + _dot_tn(vh, kb)
        outs.append(o)
    return outs


def _hgrn_kernel(q_ref, f_ref, i_ref, g_ref, lbraw_ref, gain_ref, bias_ref, msum_ref,
                 o_ref, s_ref, *, layer, heads, chunk):
    dk = dv = HG_HEAD
    tb = q_ref.shape[0]

    @pl.when(pl.program_id(2) == 0)
    def _():
        s_ref[...] = jnp.zeros_like(s_ref)

    raw = lbraw_ref[...]
    ex = jnp.exp(raw - jnp.max(raw, axis=0, keepdims=True))
    sm = ex / jnp.sum(ex, axis=0, keepdims=True)
    lb = jnp.zeros_like(sm[0:1, :])
    for m in range(1, layer + 1):
        lb = lb + sm[m:m + 1, :]
    gain = gain_ref[...]
    bias = bias_ref[...]
    msum = msum_ref[...]

    def body(c, carry):
        rows = pl.ds(c * chunk, chunk)
        q_in = q_ref[rows, :]
        f = lb + (1.0 - lb) * _sigmoid(f_ref[rows, :])
        q = q_in * _sigmoid(q_in)
        outs = _gla_chunk_heads(q, 1.0 - f, i_ref[rows, :], jnp.log(f), s_ref, msum,
                                chunk, dk, dv, heads)
        gate = _sigmoid(g_ref[rows, :])
        for hd, o in enumerate(outs):
            cs = slice(hd * dv, (hd + 1) * dv)
            y = gate[:, cs] * o
            mu = jnp.mean(y, axis=-1, keepdims=True)
            yc = y - mu
            var = jnp.mean(yc * yc, axis=-1, keepdims=True)
            yn = yc * lax.rsqrt(var + LN_EPS)
            o_ref[rows, cs] = (yn * gain[:, cs] + bias[:, cs]).astype(o_ref.dtype)
        return carry

    for c in range(tb // chunk):
        body(c, 0)


def hgrn2(proj, col0, lb_raw, gain, bias, *, layer, batch, seq, heads_per_block=4, tb=512):
    n = proj.shape[0]
    w = lb_raw.shape[1]
    nheads = w // HG_HEAD
    hpb = heads_per_block
    while nheads % hpb:
        hpb -= 1
    bw = hpb * HG_HEAD
    ngrp = nheads // hpb
    assert col0 % bw == 0
    cb = col0 // bw
    tb = min(tb, seq)
    chunk = min(GLA_CHUNK, tb)
    nt = seq // tb
    msum = jnp.asarray(_gla_sum_matrix(chunk), BF16)

    def sec(s):
        return pl.BlockSpec((tb, bw), lambda b, hg, t: (b * nt + t, cb + s * ngrp + hg))

    vec = pl.BlockSpec((1, bw), lambda b, hg, t: (0, hg))
    return pl.pallas_call(
        functools.partial(_hgrn_kernel, layer=layer, heads=hpb, chunk=chunk),
        grid=(batch, ngrp, nt),
        in_specs=[sec(0), sec(1), sec(2), sec(3),
                  pl.BlockSpec((lb_raw.shape[0], bw), lambda b, hg, t: (0, hg)),
                  vec, vec,
                  pl.BlockSpec(msum.shape, lambda b, hg, t: (0, 0))],
        out_specs=pl.BlockSpec((tb, bw), lambda b, hg, t: (b * nt + t, hg)),
        out_shape=jax.ShapeDtypeStruct((n, w), BF16),
        scratch_shapes=[pltpu.VMEM((hpb, HG_HEAD, HG_HEAD), F32)],
        compiler_params=_cparams(3),
        name="hgrn2",
    )(proj, proj, proj, proj, lb_raw, gain, bias, msum)


def _gla_kernel(q_ref, k_ref, v_ref, g_ref, wg_ref, bg_ref, gain_ref, msum_ref,
                o_ref, s_ref, *, heads, chunk, dk, dv, q_scale):
    tb = q_ref.shape[0]

    @pl.when(pl.program_id(2) == 0)
    def _():
        s_ref[...] = jnp.zeros_like(s_ref)

    wg = wg_ref[...]
    bg = bg_ref[...]
    gain = gain_ref[...]
    msum = msum_ref[...]

    def body(c, carry):
        rows = pl.ds(c * chunk, chunk)
        q_raw = q_ref[rows, :]
        q_bf = q_raw.astype(BF16)
        pre = jnp.concatenate([_dot(q_bf[:, hd * dk:(hd + 1) * dk], wg[:, hd * dk:(hd + 1) * dk])
                               for hd in range(heads)], axis=1) + bg
        log_a = (jnp.minimum(pre, 0.0) - jnp.log(1.0 + jnp.exp(-jnp.abs(pre)))) / GLA_GATE_TEMP
        outs = _gla_chunk_heads(q_raw * q_scale, k_ref[rows, :], v_ref[rows, :], log_a,
                                s_ref, msum, chunk, dk, dv, heads)
        g_in = g_ref[rows, :]
        swish = g_in * _sigmoid(g_in)
        for hd, o in enumerate(outs):
            cs = slice(hd * dv, (hd + 1) * dv)
            y = o * lax.rsqrt(jnp.mean(o * o, axis=-1, keepdims=True) + LN_EPS)
            o_ref[rows, cs] = (y * gain[:, cs] * swish[:, cs]).astype(o_ref.dtype)
        return carry

    for c in range(tb // chunk):
        body(c, 0)


def gla(proj, q0, k0, v0, g0, w_gate, b_gate, gain, *, batch, seq, dk, dv, q_scale,
        heads_per_block=4, tb=512):
    n = proj.shape[0]
    nheads = w_gate.shape[1] // dk
    hpb = min(heads_per_block, nheads)
    ngrp = nheads // hpb
    tb = min(tb, seq)
    chunk = min(GLA_CHUNK, tb)
    nt = seq // tb
    msum = jnp.asarray(_gla_sum_matrix(chunk), BF16)
    kw, vw = hpb * dk, hpb * dv
    assert q0 % kw == 0 and k0 % kw == 0 and v0 % vw == 0 and g0 % vw == 0

    def rows(width, col0):
        cb = col0 // width
        return pl.BlockSpec((tb, width), lambda b, hg, t: (b * nt + t, cb + hg))

    return pl.pallas_call(
        functools.partial(_gla_kernel, heads=hpb, chunk=chunk, dk=dk, dv=dv, q_scale=q_scale),
        grid=(batch, ngrp, nt),
        in_specs=[rows(kw, q0), rows(kw, k0), rows(vw, v0), rows(vw, g0),
                  pl.BlockSpec((w_gate.shape[0], kw), lambda b, hg, t: (0, hg)),
                  pl.BlockSpec((1, kw), lambda b, hg, t: (0, hg)),
                  pl.BlockSpec((1, vw), lambda b, hg, t: (0, hg)),
                  pl.BlockSpec(msum.shape, lambda b, hg, t: (0, 0))],
        out_specs=pl.BlockSpec((tb, vw), lambda b, hg, t: (b * nt + t, hg)),
        out_shape=jax.ShapeDtypeStruct((n, nheads * dv), BF16),
        scratch_shapes=[pltpu.VMEM((hpb, dv, dk), F32)],
        compiler_params=_cparams(3),
        name="gla",
    )(proj, proj, proj, proj, w_gate, b_gate, gain, msum)


def _s5_scan_steps(nchunks):
    return max(1, int(math.ceil(math.log2(nchunks)))) if nchunks > 1 else 0


def _s5_prep_kernel(ar_ref, ai_ref, ldt_ref, b2_ref, c2_ref, d_ref,
                    kbig_ref, win_ref, wout_ref, lscan_ref, *, nsteps):
    p2 = ar_ref.shape[-1]
    half = p2 // 2
    t_sub = S5_T
    hgrp = b2_ref.shape[1]
    ar = ar_ref[0]
    ai = ai_ref[0]
    dt = jnp.exp(ldt_ref[0])
    lane = lax.broadcasted_iota(jnp.int32, (1, p2), 1)
    sgn_im = jnp.where(lane < half, -1.0, 1.0)
    sgn_re = -sgn_im

    nrow = ((t_sub + 1 + 7) // 8) * 8
    kf = lax.broadcasted_iota(jnp.int32, (nrow, p2), 0).astype(F32)
    mag = jnp.exp(kf * (ar * dt))
    th = kf * (ai * dt)
    l_re = mag * jnp.cos(th)
    l_im = mag * jnp.sin(th)
    l_sw = l_im * sgn_im

    def cmul(x, kpow):
        return (x * l_re[kpow:kpow + 1, :]
                + pltpu.roll(x, half, axis=1) * l_sw[kpow:kpow + 1, :])

    lam_re = l_re[1:2, :]
    lam_im = l_im[1:2, :]
    den = ar * ar + ai * ai
    nr = lam_re - 1.0
    ni = lam_im
    coef_re = (nr * ar + ni * ai) / den
    coef_im = (ni * ar - nr * ai) / den
    b2 = b2_ref[0]
    bbar = b2 * coef_re + pltpu.roll(b2, half, axis=1) * (coef_im * sgn_im)
    c2 = c2_ref[0]

    win_ref[0] = jnp.concatenate([cmul(bbar, t_sub - 1 - s) for s in range(t_sub)],
                                 axis=0).astype(win_ref.dtype)
    wout_ref[0] = jnp.concatenate([cmul(c2, i + 1) * sgn_re for i in range(t_sub)],
                                  axis=0).astype(wout_ref.dtype)

    n = t_sub * hgrp
    rt = lax.broadcasted_iota(jnp.int32, (n, n), 0) // hgrp
    ct = lax.broadcasted_iota(jnp.int32, (n, n), 1) // hgrp
    zero = jnp.zeros_like(c2)
    bbar_s = bbar * sgn_re
    cl = jnp.concatenate([c2] * t_sub, axis=0)
    bl = jnp.concatenate([bbar_s] * t_sub, axis=0)
    hp = lax.Precision.HIGHEST
    kbig = jnp.where(rt == ct, _dot_nt(cl, bl, hp), 0.0)
    h = t_sub // 2
    while h >= 1:
        blk = 2 * h
        cl = jnp.concatenate([cmul(c2, (t % blk) - h + 1) if (t % blk) >= h else zero
                              for t in range(t_sub)], axis=0)
        bl = jnp.concatenate([cmul(bbar, h - 1 - (s % blk)) * sgn_re if (s % blk) < h else zero
                              for s in range(t_sub)], axis=0)
        p = _dot_nt(cl, bl, hp)
        if blk == t_sub:
            kbig = kbig + p
        else:
            kbig = kbig + jnp.where((rt // blk) == (ct // blk), p, 0.0)
        h //= 2
    r = lax.broadcasted_iota(jnp.int32, (n, n), 0)
    c = lax.broadcasted_iota(jnp.int32, (n, n), 1)
    kbig = kbig + jnp.where(r == c, d_ref[0], 0.0)
    kbig_ref[0] = kbig.astype(kbig_ref.dtype)

    cur_re = l_re[t_sub:t_sub + 1, :]
    cur_im = l_im[t_sub:t_sub + 1, :]
    rows_re, rows_sw = [], []
    for _ in range(nsteps):
        rows_re.append(cur_re)
        rows_sw.append(cur_im * sgn_im)
        cur_re, cur_im = cur_re * cur_re - cur_im * cur_im, 2.0 * cur_re * cur_im
    pad = lscan_ref.shape[1] - 2 * nsteps
    parts = rows_re + rows_sw + ([jnp.zeros((pad, p2), F32)] if pad else [])
    lscan_ref[0] = jnp.concatenate(parts, axis=0)


def s5_prep(ar2, ai2, ldt2, b2, c2, drow, nsteps):
    g, _, p2 = ar2.shape
    hgrp = b2.shape[1]
    n = S5_T * hgrp
    lrows = ((2 * nsteps + 7) // 8) * 8
    vec = pl.BlockSpec((1, 1, p2), lambda i: (i, 0, 0))
    mat = pl.BlockSpec((1, hgrp, p2), lambda i: (i, 0, 0))
    return pl.pallas_call(
        functools.partial(_s5_prep_kernel, nsteps=nsteps),
        grid=(g,),
        in_specs=[vec, vec, vec, mat, mat, pl.BlockSpec((1, 1, n), lambda i: (i, 0, 0))],
        out_specs=[pl.BlockSpec((1, n, n), lambda i: (i, 0, 0)),
                   pl.BlockSpec((1, n, p2), lambda i: (i, 0, 0)),
                   pl.BlockSpec((1, n, p2), lambda i: (i, 0, 0)),
                   pl.BlockSpec((1, lrows, p2), lambda i: (i, 0, 0))],
        out_shape=[jax.ShapeDtypeStruct((g, n, n), BF16),
                   jax.ShapeDtypeStruct((g, n, p2), BF16),
                   jax.ShapeDtypeStruct((g, n, p2), BF16),
                   jax.ShapeDtypeStruct((g, lrows, p2), F32)],
        compiler_params=_cparams(1),
        name="s5_prep",
    )(ar2, ai2, ldt2, b2, c2, drow)


def _s5_main_kernel(u_ref, kbig_ref, win_ref, wout_ref, lscan_ref, y_ref, *, nsteps, per_batch):
    u = u_ref[0]
    nrows = u.shape[0]
    p2 = win_ref.shape[-1]
    half = p2 // 2
    x = _dot(u, win_ref[0])
    pos = lax.broadcasted_iota(jnp.int32, (nrows, p2), 0) % per_batch
    lscan = lscan_ref[0]
    for j in range(nsteps):
        d = 1 << j
        sh = jnp.where(pos >= d, pltpu.roll(x, d, axis=0), 0.0)
        x = x + sh * lscan[j:j + 1, :] + pltpu.roll(sh, half, axis=1) * lscan[nsteps + j:nsteps + j + 1, :]
    xprev = jnp.where(pos >= 1, pltpu.roll(x, 1, axis=0), 0.0)
    y_ref[0] = _dot_nt(u, kbig_ref[0]) + _dot_nt(xprev.astype(BF16), wout_ref[0])


def s5_main(uf, kbig, win, wout, lscan, *, nsteps, per_batch):
    g, nrows, n = uf.shape
    p2 = win.shape[-1]

    def spec(a):
        return pl.BlockSpec((1,) + a.shape[1:], lambda i: (i, 0, 0))

    return pl.pallas_call(
        functools.partial(_s5_main_kernel, nsteps=nsteps, per_batch=per_batch),
        grid=(g,),
        in_specs=[spec(uf), spec(kbig), spec(win), spec(wout), spec(lscan)],
        out_specs=pl.BlockSpec((1, nrows, n), lambda i: (i, 0, 0)),
        out_shape=jax.ShapeDtypeStruct((g, nrows, n), F32),
        compiler_params=_cparams(1),
        name="s5_main",
    )(uf, kbig, win, wout, lscan)


def _s5_glu_kernel(y_ref, w_ref, b_ref, o_ref, wbf_ref):
    @pl.when(pl.program_id(0) == 0)
    def _():
        wbf_ref[...] = w_ref[...].astype(BF16)

    y = y_ref[...]
    z = 0.5 * y * (1.0 + jnp.tanh(math.sqrt(2.0 / math.pi) * (y + 0.044715 * (y * y * y))))
    gate = _dot(z.astype(BF16), wbf_ref[...]) + b_ref[...]
    o_ref[...] = (z * _sigmoid(gate)).astype(o_ref.dtype)


def s5_glu(y, w_stack, layer, b, tm=512):
    n, wd = y.shape
    tm = min(tm, n)
    return pl.pallas_call(
        _s5_glu_kernel,
        grid=(n // tm,),
        in_specs=[pl.BlockSpec((tm, wd), lambda i: (i, 0)),
                  pl.BlockSpec((None, wd, wd), lambda i: (layer, 0, 0)),
                  pl.BlockSpec((1, wd), lambda i: (0, 0))],
        out_specs=pl.BlockSpec((tm, wd), lambda i: (i, 0)),
        out_shape=jax.ShapeDtypeStruct((n, wd), BF16),
        scratch_shapes=[pltpu.VMEM((wd, wd), BF16)],
        compiler_params=_cparams(1),
        name="s5_glu",
    )(y, w_stack, b)


def s5_mixer(u, a_re, a_im, log_dt, b_re, b_im, c_re, c_im, d_skip, glu_w_stack, layer, glu_b,
             *, batch, seq):
    n, wd = u.shape
    g, p = a_re.shape
    hgrp = wd // g
    assert hgrp == S5_GROUP and seq % S5_T == 0
    per_batch = seq // S5_T
    nsteps = _s5_scan_steps(per_batch)
    dup = lambda a: jnp.concatenate([a, a], axis=-1)[:, None, :]
    ar2, ai2 = dup(a_re), dup(a_im)
    ldt2 = jnp.broadcast_to(log_dt[:, None, None], (g, 1, 2 * p))
    b2 = jnp.concatenate([b_re.transpose(0, 2, 1), b_im.transpose(0, 2, 1)], axis=-1)
    c2 = jnp.concatenate([c_re, c_im], axis=-1)
    drow = jnp.tile(d_skip.reshape(g, 1, hgrp), (1, 1, S5_T))
    kbig, win, wout, lscan = s5_prep(ar2, ai2, ldt2, b2, c2, drow, nsteps)
    nr = n // S5_T
    uf = u.reshape(nr, S5_T, g, hgrp).transpose(2, 0, 1, 3).reshape(g, nr, S5_T * hgrp).astype(BF16)
    yf = s5_main(uf, kbig, win, wout, lscan, nsteps=nsteps, per_batch=per_batch)
    y = yf.reshape(g, nr, S5_T, hgrp).transpose(1, 2, 0, 3).reshape(n, wd)
    return s5_glu(y, glu_w_stack, layer, glu_b[None, :])


def _pad_heads(w, heads, width, new):
    r = w.shape[0]
    return jnp.pad(w.reshape(r, heads, width), ((0, 0), (0, 0), (0, new - width))).reshape(r, heads * new)


def kernel(x, c, w_ada, b_ada, ada_table, w_in, w_out, s5_a_re, s5_a_im, s5_log_dt, s5_b_re, s5_b_im, s5_c_re, s5_c_im, s5_d, s5_glu_w, s5_glu_b, hg_lb_raw, hg_norm_gain, hg_norm_bias, gla_w_gate, gla_b_gate, gla_norm_gain, w_ffn_gate, w_ffn_up, w_ffn_down, ln1_gain, ln1_bias, ln2_gain, ln2_bias):
    bsz, seq, d = x.shape
    depth = w_in.shape[0]
    n = bsz * seq
    s5_w = s5_d.shape[1]
    hg_w = hg_lb_raw.shape[1]
    gla_kw = gla_b_gate.shape[1]
    gla_vw = gla_norm_gain.shape[1]
    rank = gla_w_gate.shape[1]
    gla_dk = gla_kw // GLA_HEADS
    gla_dv = gla_vw // GLA_HEADS
    dk_pad = ((gla_dk + GLA_DK_PAD - 1) // GLA_DK_PAD) * GLA_DK_PAD
    assert dk_pad - gla_dk >= rank
    alpha = (2.0 * depth) ** 0.25

    rows = ((bsz + 7) // 8) * 8
    c_pad = jnp.pad(c, ((0, rows - bsz), (0, 0)))
    mod = cond_table(c_pad, w_ada, b_ada[None, :], ada_table.reshape(depth, N_MOD * d))
    mod = mod[:, :bsz].reshape(depth, bsz, N_MOD, d)

    o_u = 0
    o_hg = o_u + s5_w
    o_q = o_hg + 4 * hg_w
    o_k = o_q + gla_kw
    o_v = o_k + gla_kw
    o_lr = o_v + 2 * gla_vw

    hk = GLA_HEADS * dk_pad
    w_q = jnp.concatenate(
        [w_in[:, :, o_q:o_k].reshape(depth, d, GLA_HEADS, gla_dk),
         jnp.broadcast_to(w_in[:, :, None, o_lr:o_lr + rank], (depth, d, GLA_HEADS, rank)),
         jnp.zeros((depth, d, GLA_HEADS, dk_pad - gla_dk - rank), F32)], axis=3).reshape(depth, d, hk)
    w_k = jnp.pad(w_in[:, :, o_k:o_v].reshape(depth, d, GLA_HEADS, gla_dk),
                  ((0, 0), (0, 0), (0, 0), (0, dk_pad - gla_dk))).reshape(depth, d, hk)
    w_all = jnp.concatenate([w_in[:, :, o_v:o_lr], w_in[:, :, o_hg:o_q], w_in[:, :, o_u:o_hg], w_q, w_k],
                            axis=2).astype(BF16)
    p_v = 0
    p_g = p_v + gla_vw
    p_hg = p_g + gla_vw
    p_u = p_hg + 4 * hg_w
    p_q = p_u + s5_w
    p_k = p_q + hk
    w_out_bf = w_out.astype(BF16)
    w_gate_bf = w_ffn_gate.astype(BF16)
    w_up_bf = w_ffn_up.astype(BF16)
    w_down_bf = w_ffn_down.astype(BF16)

    x2 = x.reshape(n, d)
    h = modulate(x2, mod[0], seq)
    for l in range(depth):
        proj = matmul_ws([h], w_all, l, F32)
        y_a = s5_mixer(proj[:, p_u:p_u + s5_w], s5_a_re[l], s5_a_im[l], s5_log_dt[l], s5_b_re[l],
                       s5_b_im[l], s5_c_re[l], s5_c_im[l], s5_d[l], s5_glu_w, l, s5_glu_b[l],
                       batch=bsz, seq=seq)
        y_b = hgrn2(proj, p_hg, hg_lb_raw, hg_norm_gain[l][None, :], hg_norm_bias[l][None, :],
                    layer=l, batch=bsz, seq=seq)
        w_gate = jnp.pad(
            jnp.pad(gla_w_gate[l].reshape(rank, GLA_HEADS, gla_dk), ((0, 0), (0, 0), (0, dk_pad - gla_dk))),
            ((gla_dk, dk_pad - gla_dk - rank), (0, 0), (0, 0))).reshape(dk_pad, hk).astype(BF16)
        b_gate = _pad_heads(gla_b_gate[l][None, :], GLA_HEADS, gla_dk, dk_pad)
        y_c = gla(proj, p_q, p_k, p_v, p_g, w_gate, b_gate, gla_norm_gain[l][None, :],
                  batch=bsz, seq=seq, dk=dk_pad, dv=gla_dv, q_scale=float(gla_dk) ** -0.5)
        mixed = matmul_ws([y_a, y_b, y_c], w_out_bf, l, BF16)
        x2, h = ln_mod(x2, mixed, mod[l], mod[l], ln1_gain[l][None, :], ln1_bias[l][None, :], seq,
                       alpha=alpha, gate_row=2, next_row=3, with_h=True)

        act = ffn_up(h, w_gate_bf, w_up_bf, l)
        ffn = matmul_ws([act], w_down_bf, l, BF16, tn=512)
        last = l == depth - 1
        x2, h = ln_mod(x2, ffn, mod[l], mod[l if last else l + 1],
                       ln2_gain[l][None, :], ln2_bias[l][None, :], seq,
                       alpha=alpha, gate_row=5, next_row=0, with_h=not last)
    return x2.reshape(bsz, seq, d)
```

```python
import functools
import math

import numpy as np
import jax
import jax.numpy as jnp
from jax import lax
from jax.experimental import pallas as pl
from jax.experimental.pallas import tpu as pltpu

F32 = jnp.float32
BF16 = jnp.bfloat16

LANES = 128
V7X_VMEM_BYTES = 64 * 1024 * 1024
VMEM_LIMIT = V7X_VMEM_BYTES - 8 * 1024 * 1024

S5_GROUP = 16
HG_HEAD = 128
GLA_HEADS = 4
GLA_GATE_TEMP = 16.0
N_MOD = 6
LN_EPS = 1e-5

S5_T = 16
GLA_CHUNK = 128
GLA_DK_PAD = 256


def _cparams(n_axes):
    return pltpu.CompilerParams(
        dimension_semantics=("arbitrary",) * n_axes, vmem_limit_bytes=VMEM_LIMIT)


def _sigmoid(x):
    return 1.0 / (1.0 + jnp.exp(-x))


def _dot(a, b):
    return jnp.dot(a, b, preferred_element_type=F32)


def _dot_nt(a, b, precision=None):
    return lax.dot_general(a, b, (((1,), (1,)), ((), ())),
                           preferred_element_type=F32, precision=precision)


def _dot_tn(a, b):
    return lax.dot_general(a, b, (((0,), (0,)), ((), ())), preferred_element_type=F32)


def _pick_tile(n, cap):
    best = None
    for t in range(LANES, min(n, cap) + 1, LANES):
        if n % t == 0:
            best = t
    assert best is not None, (n, cap)
    return best


def _mm_ws_kernel(*refs, n_a):
    a_refs, w_ref, o_ref = refs[:n_a], refs[n_a], refs[n_a + 1]
    acc = None
    r0 = 0
    for a_ref in a_refs:
        k = a_ref.shape[1]
        part = _dot(a_ref[...], w_ref[r0:r0 + k, :])
        acc = part if acc is None else acc + part
        r0 += k
    o_ref[...] = acc.astype(o_ref.dtype)


def matmul_ws(a_list, w_stack, layer, out_dtype, tm=512, tn=1024):
    m = a_list[0].shape[0]
    _, k, n = w_stack.shape
    assert sum(a.shape[1] for a in a_list) == k
    tn = _pick_tile(n, tn)
    tm = min(tm, m)
    return pl.pallas_call(
        functools.partial(_mm_ws_kernel, n_a=len(a_list)),
        grid=(n // tn, m // tm),
        in_specs=[pl.BlockSpec((tm, a.shape[1]), lambda j, i: (i, 0)) for a in a_list]
        + [pl.BlockSpec((None, k, tn), lambda j, i: (layer, 0, j))],
        out_specs=pl.BlockSpec((tm, tn), lambda j, i: (i, j)),
        out_shape=jax.ShapeDtypeStruct((m, n), out_dtype),
        compiler_params=_cparams(2),
        name="matmul_ws",
    )(*a_list, w_stack)


def _ffn_up_kernel(a_ref, wg_ref, wu_ref, o_ref):
    a = a_ref[...]
    g = _dot(a, wg_ref[...])
    u = _dot(a, wu_ref[...])
    o_ref[...] = (g * _sigmoid(g) * u).astype(o_ref.dtype)


def ffn_up(h, wg_stack, wu_stack, layer, tm=2048, tn=256):
    m, k = h.shape
    n = wg_stack.shape[2]
    tn = _pick_tile(n, tn)
    tm = min(tm, m)
    wspec = pl.BlockSpec((None, k, tn), lambda i, j: (layer, 0, j))
    return pl.pallas_call(
        _ffn_up_kernel,
        grid=(m // tm, n // tn),
        in_specs=[pl.BlockSpec((tm, k), lambda i, j: (i, 0)), wspec, wspec],
        out_specs=pl.BlockSpec((tm, tn), lambda i, j: (i, j)),
        out_shape=jax.ShapeDtypeStruct((m, n), BF16),
        compiler_params=_cparams(2),
        name="ffn_up",
    )(h, wg_stack, wu_stack)


def _cond_kernel(c_ref, w_ref, b_ref, tab_ref, o_ref):
    c = c_ref[...]
    act = (c * _sigmoid(c)).astype(BF16)
    cond = _dot(act, w_ref[...].astype(BF16)) + b_ref[...]
    for l in range(tab_ref.shape[0]):
        o_ref[l] = cond + tab_ref[l:l + 1, :]


def cond_table(c_pad, w_ada, b_ada, ada_table2, tn=1024):
    rows, d = c_pad.shape
    n = w_ada.shape[1]
    depth = ada_table2.shape[0]
    tn = _pick_tile(n, tn)
    return pl.pallas_call(
        _cond_kernel,
        grid=(n // tn,),
        in_specs=[pl.BlockSpec((rows, d), lambda j: (0, 0)),
                  pl.BlockSpec((d, tn), lambda j: (0, j)),
                  pl.BlockSpec((1, tn), lambda j: (0, j)),
                  pl.BlockSpec((depth, tn), lambda j: (0, j))],
        out_specs=pl.BlockSpec((depth, rows, tn), lambda j: (0, 0, j)),
        out_shape=jax.ShapeDtypeStruct((depth, rows, n), F32),
        compiler_params=_cparams(1),
        name="cond_table",
    )(c_pad, w_ada, b_ada, ada_table2)


def _modulate_kernel(x_ref, mod_ref, h_ref):
    m = mod_ref[0]
    h_ref[...] = (x_ref[...] * (1.0 + m[1:2, :]) + m[0:1, :]).astype(h_ref.dtype)


def modulate(x2, mod_l, seq, tm=512):
    n, d = x2.shape
    tm = min(tm, seq)
    per_b = seq // tm
    return pl.pallas_call(
        _modulate_kernel,
        grid=(n // tm,),
        in_specs=[pl.BlockSpec((tm, d), lambda i: (i, 0)),
                  pl.BlockSpec((1, N_MOD, d), lambda i: (i // per_b, 0, 0))],
        out_specs=pl.BlockSpec((tm, d), lambda i: (i, 0)),
        out_shape=jax.ShapeDtypeStruct((n, d), BF16),
        compiler_params=_cparams(1),
        name="modulate",
    )(x2, mod_l)


def _ln_mod_kernel(x_ref, mm_ref, mod_ref, nmod_ref, gain_ref, bias_ref, xo_ref, *h_refs,
                   alpha, gate_row, next_row):
    m = mod_ref[0]
    z = alpha * x_ref[...] + (1.0 + m[gate_row:gate_row + 1, :]) * mm_ref[...].astype(F32)
    mu = jnp.mean(z, axis=-1, keepdims=True)
    zc = z - mu
    var = jnp.mean(zc * zc, axis=-1, keepdims=True)
    y = zc * lax.rsqrt(var + LN_EPS) * gain_ref[...] + bias_ref[...]
    xo_ref[...] = y
    if h_refs:
        nm = nmod_ref[0]
        h_refs[0][...] = (y * (1.0 + nm[next_row + 1:next_row + 2, :])
                          + nm[next_row:next_row + 1, :]).astype(BF16)


def ln_mod(x2, mm, mod_l, mod_next, gain, bias, seq, *, alpha, gate_row, next_row, with_h, tm=256):
    n, d = x2.shape
    tm = min(tm, seq)
    per_b = seq // tm
    row = pl.BlockSpec((tm, d), lambda i: (i, 0))
    modspec = pl.BlockSpec((1, N_MOD, d), lambda i: (i // per_b, 0, 0))
    vec = pl.BlockSpec((1, d), lambda i: (0, 0))
    out_shape = [jax.ShapeDtypeStruct((n, d), F32)]
    out_specs = [row]
    if with_h:
        out_shape.append(jax.ShapeDtypeStruct((n, d), BF16))
        out_specs.append(row)
    res = pl.pallas_call(
        functools.partial(_ln_mod_kernel, alpha=alpha, gate_row=gate_row, next_row=next_row),
        grid=(n // tm,),
        in_specs=[row, row, modspec, modspec, vec, vec],
        out_specs=out_specs,
        out_shape=out_shape,
        compiler_params=_cparams(1),
        name="ln_mod",
    )(x2, mm, mod_l, mod_next, gain, bias)
    return (res[0], res[1]) if with_h else (res[0], None)


def _gla_levels(chunk):
    lv = []
    h = chunk // 2
    while h >= 1:
        lv.append(h)
        h //= 2
    return lv


def _gla_sum_matrix(chunk):
    blocks = []
    idx = np.arange(chunk)
    for h in _gla_levels(chunk):
        m = np.zeros((chunk, chunk), np.float32)
        for i in range(chunk):
            r = (i // (2 * h)) * 2 * h + h - 1
            if i % (2 * h) >= h:
                m[i, r + 1:i + 1] = 1.0
            else:
                m[i, i + 1:r + 1] = 1.0
        blocks.append(m)
    blocks.append((idx[None, :] <= idx[:, None]).astype(np.float32))
    blocks.append((idx[None, :] > idx[:, None]).astype(np.float32))
    m = np.concatenate(blocks, axis=0)
    return np.concatenate([m, m], axis=1)


def _gla_chunk_heads(q, k, v, g, s_ref, msum, chunk, dk, dv, heads):
    levels = _gla_levels(chunk)
    g_hi = g.astype(BF16)
    g_lo = (g - g_hi.astype(F32)).astype(BF16)
    expo = _dot(msum, jnp.concatenate([g_hi, g_lo], axis=0))
    e_all = jnp.exp(expo)

    row = lax.broadcasted_iota(jnp.int32, (chunk, chunk), 0)
    col = lax.broadcasted_iota(jnp.int32, (chunk, chunk), 1)
    outs = []
    for hd in range(heads):
        ks = slice(hd * dk, (hd + 1) * dk)
        qh, kh = q[:, ks], k[:, ks]
        vh = v[:, hd * dv:(hd + 1) * dv].astype(BF16)
        a = jnp.where(row == col, _dot_nt(qh.astype(BF16), kh.astype(BF16)), 0.0)
        for li, h in enumerate(levels):
            e = e_all[li * chunk:(li + 1) * chunk, ks]
            p = _dot_nt((qh * e).astype(BF16), (kh * e).astype(BF16))
            blk = 2 * h
            if blk == chunk:
                mask = (row >= h) & (col < h)
            else:
                mask = ((row // blk) == (col // blk)) & ((row % blk) >= h) & ((col % blk) < h)
            a = jnp.where(mask, p, a)
        nl = len(levels)
        e_cum = e_all[nl * chunk:(nl + 1) * chunk, ks]
        e_rev = e_all[(nl + 1) * chunk:(nl + 2) * chunk, ks]
        s_t = s_ref[hd]
        o = _dot(a.astype(BF16), vh) + _dot_nt((qh * e_cum).astype(BF16), s_t.astype(BF16))
        kb = (kh * e_rev).astype(BF16)
        s_ref[hd] = s_t * e_cum[chunk - 1:chunk, :] + _dot_tn(vh, kb)
        outs.append(o)
    return outs


def _hgrn_kernel(q_ref, f_ref, i_ref, g_ref, lbraw_ref, gain_ref, bias_ref, msum_ref,
                 o_ref, s_ref, *, layer, heads, chunk):
    dk = dv = HG_HEAD
    tb = q_ref.shape[0]

    @pl.when(pl.program_id(2) == 0)
    def _():
        s_ref[...] = jnp.zeros_like(s_ref)

    raw = lbraw_ref[...]
    ex = jnp.exp(raw - jnp.max(raw, axis=0, keepdims=True))
    sm = ex / jnp.sum(ex, axis=0, keepdims=True)
    lb = jnp.zeros_like(sm[0:1, :])
    for m in range(1, layer + 1):
        lb = lb + sm[m:m + 1, :]
    gain = gain_ref[...]
    bias = bias_ref[...]
    msum = msum_ref[...]

    def body(c, carry):
        rows = pl.ds(c * chunk, chunk)
        q_in = q_ref[rows, :]
        f = lb + (1.0 - lb) * _sigmoid(f_ref[rows, :])
        q = q_in * _sigmoid(q_in)
        outs = _gla_chunk_heads(q, 1.0 - f, i_ref[rows, :], jnp.log(f), s_ref, msum,
                                chunk, dk, dv, heads)
        gate = _sigmoid(g_ref[rows, :])
        for hd, o in enumerate(outs):
            cs = slice(hd * dv, (hd + 1) * dv)
            y = gate[:, cs] * o
            mu = jnp.mean(y, axis=-1, keepdims=True)
            yc = y - mu
            var = jnp.mean(yc * yc, axis=-1, keepdims=True)
            yn = yc * lax.rsqrt(var + LN_EPS)
            o_ref[rows, cs] = (yn * gain[:, cs] + bias[:, cs]).astype(o_ref.dtype)
        return carry

    for c in range(tb // chunk):
        body(c, 0)


def hgrn2(proj, col0, lb_raw, gain, bias, *, layer, batch, seq, heads_per_block=4, tb=512):
    n = proj.shape[0]
    w = lb_raw.shape[1]
    nheads = w // HG_HEAD
    hpb = heads_per_block
    while nheads % hpb:
        hpb -= 1
    bw = hpb * HG_HEAD
    ngrp = nheads // hpb
    assert col0 % bw == 0
    cb = col0 // bw
    tb = min(tb, seq)
    chunk = min(GLA_CHUNK, tb)
    nt = seq // tb
    msum = jnp.asarray(_gla_sum_matrix(chunk), BF16)

    def sec(s):
        return pl.BlockSpec((tb, bw), lambda b, hg, t: (b * nt + t, cb + s * ngrp + hg))

    vec = pl.BlockSpec((1, bw), lambda b, hg, t: (0, hg))
    return pl.pallas_call(
        functools.partial(_hgrn_kernel, layer=layer, heads=hpb, chunk=chunk),
        grid=(batch, ngrp, nt),
        in_specs=[sec(0), sec(1), sec(2), sec(3),
                  pl.BlockSpec((lb_raw.shape[0], bw), lambda b, hg, t: (0, hg)),
                  vec, vec,
                  pl.BlockSpec(msum.shape, lambda b, hg, t: (0, 0))],
        out_specs=pl.BlockSpec((tb, bw), lambda b, hg, t: (b * nt + t, hg)),
        out_shape=jax.ShapeDtypeStruct((n, w), BF16),
        scratch_shapes=[pltpu.VMEM((hpb, HG_HEAD, HG_HEAD), F32)],
        compiler_params=_cparams(3),
        name="hgrn2",
    )(proj, proj, proj, proj, lb_raw, gain, bias, msum)


def _gla_kernel(q_ref, k_ref, v_ref, g_ref, wg_ref, bg_ref, gain_ref, msum_ref,
                o_ref, s_ref, *, heads, chunk, dk, dv, q_scale):
    tb = q_ref.shape[0]

    @pl.when(pl.program_id(2) == 0)
    def _():
        s_ref[...] = jnp.zeros_like(s_ref)

    wg = wg_ref[...]
    bg = bg_ref[...]
    gain = gain_ref[...]
    msum = msum_ref[...]

    def body(c, carry):
        rows = pl.ds(c * chunk, chunk)
        q_raw = q_ref[rows, :]
        q_bf = q_raw.astype(BF16)
        pre = jnp.concatenate([_dot(q_bf[:, hd * dk:(hd + 1) * dk], wg[:, hd * dk:(hd + 1) * dk])
                               for hd in range(heads)], axis=1) + bg
        log_a = (jnp.minimum(pre, 0.0) - jnp.log(1.0 + jnp.exp(-jnp.abs(pre)))) / GLA_GATE_TEMP
        outs = _gla_chunk_heads(q_raw * q_scale, k_ref[rows, :], v_ref[rows, :], log_a,
                                s_ref, msum, chunk, dk, dv, heads)
        g_in = g_ref[rows, :]
        swish = g_in * _sigmoid(g_in)
        for hd, o in enumerate(outs):
            cs = slice(hd * dv, (hd + 1) * dv)
            y = o * lax.rsqrt(jnp.mean(o * o, axis=-1, keepdims=True) + LN_EPS)
            o_ref[rows, cs] = (y * gain[:, cs] * swish[:, cs]).astype(o_ref.dtype)
        return carry

    for c in range(tb // chunk):
        body(c, 0)


def gla(proj, q0, k0, v0, g0, w_gate, b_gate, gain, *, batch, seq, dk, dv, q_scale,
        heads_per_block=4, tb=512):
    n = proj.shape[0]
    nheads = w_gate.shape[1] // dk
    hpb = min(heads_per_block, nheads)
    ngrp = nheads // hpb
    tb = min(tb, seq)
    chunk = min(GLA_CHUNK, tb)
    nt = seq // tb
    msum = jnp.asarray(_gla_sum_matrix(chunk), BF16)
    kw, vw = hpb * dk, hpb * dv
    assert q0 % kw == 0 and k0 % kw == 0 and v0 % vw == 0 and g0 % vw == 0

    def rows(width, col0):
        cb = col0 // width
        return pl.BlockSpec((tb, width), lambda b, hg, t: (b * nt + t, cb + hg))

    return pl.pallas_call(
        functools.partial(_gla_kernel, heads=hpb, chunk=chunk, dk=dk, dv=dv, q_scale=q_scale),
        grid=(batch, ngrp, nt),
        in_specs=[rows(kw, q0), rows(kw, k0), rows(vw, v0), rows(vw, g0),
                  pl.BlockSpec((w_gate.shape[0], kw), lambda b, hg, t: (0, hg)),
                  pl.BlockSpec((1, kw), lambda b, hg, t: (0, hg)),
                  pl.BlockSpec((1, vw), lambda b, hg, t: (0, hg)),
                  pl.BlockSpec(msum.shape, lambda b, hg, t: (0, 0))],
        out_specs=pl.BlockSpec((tb, vw), lambda b, hg, t: (b * nt + t, hg)),
        out_shape=jax.ShapeDtypeStruct((n, nheads * dv), BF16),
        scratch_shapes=[pltpu.VMEM((hpb, dv, dk), F32)],
        compiler_params=_cparams(3),
        name="gla",
    )(proj, proj, proj, proj, w_gate, b_gate, gain, msum)


S5_GB = LANES // S5_GROUP


def _s5_scan_steps(nchunks):
    return max(1, int(math.ceil(math.log2(nchunks)))) if nchunks > 1 else 0


def _s5_prep_kernel(ar_ref, ai_ref, ldt_ref, b2_ref, c2_ref, d_ref,
                    ktoep_ref, win_ref, wo_ref, lscan_ref, *, nsteps):
    rows, p2 = ar_ref.shape
    half = p2 // 2
    t_sub = S5_T
    ar = ar_ref[...]
    ai = ai_ref[...]
    dt = jnp.exp(ldt_ref[...])
    lane = lax.broadcasted_iota(jnp.int32, (1, p2), 1)
    sgn_im = jnp.where(lane < half, -1.0, 1.0)
    sgn_re = -sgn_im

    def lam_pow(k):
        mag = jnp.exp(float(k) * (ar * dt))
        th = float(k) * (ai * dt)
        return mag * jnp.cos(th), mag * jnp.sin(th)

    pows = [lam_pow(k) for k in range(t_sub + 1)]

    def cmul(x, k):
        l_re, l_im = pows[k]
        return x * l_re + pltpu.roll(x, half, axis=1) * (l_im * sgn_im)

    lam_re, lam_im = pows[1]
    den = ar * ar + ai * ai
    nr = lam_re - 1.0
    ni = lam_im
    coef_re = (nr * ar + ni * ai) / den
    coef_im = (ni * ar - nr * ai) / den
    b2 = b2_ref[...]
    bbar = b2 * coef_re + pltpu.roll(b2, half, axis=1) * (coef_im * sgn_im)
    c2 = c2_ref[...]

    rgrp = lax.broadcasted_iota(jnp.int32, (rows, rows), 0) // S5_GROUP
    cgrp = lax.broadcasted_iota(jnp.int32, (rows, rows), 1) // S5_GROUP
    same_grp = rgrp == cgrp
    r_i = lax.broadcasted_iota(jnp.int32, (rows, rows), 0)
    c_i = lax.broadcasted_iota(jnp.int32, (rows, rows), 1)
    hp = lax.Precision.HIGHEST

    for j in range(t_sub):
        k = t_sub - 1 - j
        tap = jnp.where(same_grp, _dot_nt(cmul(bbar, k) * sgn_re, c2, hp), 0.0)
        if k == 0:
            tap = tap + jnp.where(r_i == c_i, d_ref[...], 0.0)
        ktoep_ref[j * rows:(j + 1) * rows, :] = tap.astype(ktoep_ref.dtype)

    grp_of_row = lax.broadcasted_iota(jnp.int32, (rows, p2), 0) // S5_GROUP

    def block_diag(tile):
        return jnp.concatenate([jnp.where(grp_of_row == gg, tile, 0.0) for gg in range(S5_GB)], axis=1)

    for s in range(t_sub):
        win_ref[s * rows:(s + 1) * rows, :] = block_diag(cmul(bbar, t_sub - 1 - s)).astype(win_ref.dtype)
        wo_ref[s * rows:(s + 1) * rows, :] = block_diag(cmul(c2, s + 1) * sgn_re).astype(wo_ref.dtype)

    def group_rows(tile):
        return jnp.concatenate([tile[gg * S5_GROUP:gg * S5_GROUP + 1, :] for gg in range(S5_GB)], axis=1)

    cur_re, cur_im = pows[t_sub]
    rows_re, rows_sw = [], []
    for _ in range(nsteps):
        rows_re.append(group_rows(cur_re))
        rows_sw.append(group_rows(cur_im * sgn_im))
        cur_re, cur_im = cur_re * cur_re - cur_im * cur_im, 2.0 * cur_re * cur_im
    pad = lscan_ref.shape[0] - 2 * nsteps
    parts = rows_re + rows_sw + ([jnp.zeros((pad, S5_GB * p2), F32)] if pad else [])
    lscan_ref[...] = jnp.concatenate(parts, axis=0)


def s5_prep(ar_rows, ai_rows, ldt_rows, b2_rows, c2_rows, d_row, nsteps):
    wd, p2 = ar_rows.shape
    nblk = wd // LANES
    lrows = ((2 * nsteps + 7) // 8) * 8
    tile = pl.BlockSpec((LANES, p2), lambda i: (i, 0))
    return pl.pallas_call(
        functools.partial(_s5_prep_kernel, nsteps=nsteps),
        grid=(nblk,),
        in_specs=[tile, tile, tile, tile, tile, pl.BlockSpec((1, LANES), lambda i: (0, i))],
        out_specs=[pl.BlockSpec((None, S5_T * LANES, LANES), lambda i: (i, 0, 0)),
                   pl.BlockSpec((None, S5_T * LANES, S5_GB * p2), lambda i: (i, 0, 0)),
                   pl.BlockSpec((None, S5_T * LANES, S5_GB * p2), lambda i: (i, 0, 0)),
                   pl.BlockSpec((None, lrows, S5_GB * p2), lambda i: (i, 0, 0))],
        out_shape=[jax.ShapeDtypeStruct((nblk, S5_T * LANES, LANES), BF16),
                   jax.ShapeDtypeStruct((nblk, S5_T * LANES, S5_GB * p2), BF16),
                   jax.ShapeDtypeStruct((nblk, S5_T * LANES, S5_GB * p2), BF16),
                   jax.ShapeDtypeStruct((nblk, lrows, S5_GB * p2), F32)],
        compiler_params=_cparams(1),
        name="s5_prep",
    )(ar_rows, ai_rows, ldt_rows, b2_rows, c2_rows, d_row)


def _s5_main_kernel(u_ref, ktoep_ref, win_ref, wo_ref, lscan_ref, y_ref, *, nsteps):
    t_sub = S5_T
    nch = u_ref.shape[0] // t_sub
    p2 = win_ref.shape[1] // S5_GB
    half = p2 // 2
    xcat = jnp.concatenate([u_ref[pl.ds(s, nch, stride=t_sub), :].astype(BF16) for s in range(t_sub)],
                           axis=1)
    z = _dot(xcat, win_ref[...])
    pos = lax.broadcasted_iota(jnp.int32, (nch, p2), 0)
    lscan = lscan_ref[...]
    xprev = []
    for gg in range(S5_GB):
        cols = slice(gg * p2, (gg + 1) * p2)
        x = z[:, cols]
        for j in range(nsteps):
            d = 1 << j
            sh = jnp.where(pos >= d, pltpu.roll(x, d, axis=0), 0.0)
            x = (x + sh * lscan[j:j + 1, cols]
                 + pltpu.roll(sh, half, axis=1) * lscan[nsteps + j:nsteps + j + 1, cols])
        xprev.append(jnp.where(pos >= 1, pltpu.roll(x, 1, axis=0), 0.0).astype(BF16))
    y_state = _dot_nt(jnp.concatenate(xprev, axis=1), wo_ref[...])
    for t in range(t_sub):
        y_t = y_state[:, t * LANES:(t + 1) * LANES] + _dot(
            xcat[:, :(t + 1) * LANES], ktoep_ref[(t_sub - 1 - t) * LANES:, :])
        y_ref[pl.ds(t, nch, stride=t_sub), :] = y_t


def s5_main(proj, col0, wd, ktoep, win, wo, lscan, *, nsteps, batch, seq):
    n = proj.shape[0]
    nblk = wd // LANES
    assert col0 % LANES == 0
    cb = col0 // LANES

    def wspec(a):
        return pl.BlockSpec((None,) + a.shape[1:], lambda i, b: (i, 0, 0))

    return pl.pallas_call(
        functools.partial(_s5_main_kernel, nsteps=nsteps),
        grid=(nblk, batch),
        in_specs=[pl.BlockSpec((seq, LANES), lambda i, b: (b, cb + i)),
                  wspec(ktoep), wspec(win), wspec(wo), wspec(lscan)],
        out_specs=pl.BlockSpec((seq, LANES), lambda i, b: (b, i)),
        out_shape=jax.ShapeDtypeStruct((n, wd), F32),
        compiler_params=_cparams(2),
        name="s5_main",
    )(proj, ktoep, win, wo, lscan)


def _s5_glu_kernel(y_ref, w_ref, b_ref, o_ref, wbf_ref):
    @pl.when(pl.program_id(0) == 0)
    def _():
        wbf_ref[...] = w_ref[...].astype(BF16)

    y = y_ref[...]
    z = 0.5 * y * (1.0 + jnp.tanh(math.sqrt(2.0 / math.pi) * (y + 0.044715 * (y * y * y))))
    gate = _dot(z.astype(BF16), wbf_ref[...]) + b_ref[...]
    o_ref[...] = (z * _sigmoid(gate)).astype(o_ref.dtype)


def s5_glu(y, w_stack, layer, b, tm=512):
    n, wd = y.shape
    tm = min(tm, n)
    return pl.pallas_call(
        _s5_glu_kernel,
        grid=(n // tm,),
        in_specs=[pl.BlockSpec((tm, wd), lambda i: (i, 0)),
                  pl.BlockSpec((None, wd, wd), lambda i: (layer, 0, 0)),
                  pl.BlockSpec((1, wd), lambda i: (0, 0))],
        out_specs=pl.BlockSpec((tm, wd), lambda i: (i, 0)),
        out_shape=jax.ShapeDtypeStruct((n, wd), BF16),
        scratch_shapes=[pltpu.VMEM((wd, wd), BF16)],
        compiler_params=_cparams(1),
        name="s5_glu",
    )(y, w_stack, b)


def s5_mixer(proj, col0, a_re, a_im, log_dt, b_re, b_im, c_re, c_im, d_skip, glu_w_stack, layer, glu_b,
             *, batch, seq):
    wd = d_skip.shape[0]
    g, p = a_re.shape
    assert wd // g == S5_GROUP and seq % S5_T == 0 and wd % LANES == 0
    nsteps = _s5_scan_steps(seq // S5_T)
    per_row = lambda a: jnp.repeat(jnp.concatenate([a, a], axis=-1), S5_GROUP, axis=0)
    ldt_rows = jnp.broadcast_to(jnp.repeat(log_dt, S5_GROUP)[:, None], (wd, 2 * p))
    b2_rows = jnp.concatenate([b_re.transpose(0, 2, 1), b_im.transpose(0, 2, 1)], axis=-1).reshape(wd, 2 * p)
    c2_rows = jnp.concatenate([c_re, c_im], axis=-1).reshape(wd, 2 * p)
    ktoep, win, wo, lscan = s5_prep(per_row(a_re), per_row(a_im), ldt_rows, b2_rows, c2_rows,
                                    d_skip[None, :], nsteps)
    y = s5_main(proj, col0, wd, ktoep, win, wo, lscan, nsteps=nsteps, batch=batch, seq=seq)
    return s5_glu(y, glu_w_stack, layer, glu_b[None, :])


def _pad_heads(w, heads, width, new):
    r = w.shape[0]
    return jnp.pad(w.reshape(r, heads, width), ((0, 0), (0, 0), (0, new - width))).reshape(r, heads * new)


def kernel(x, c, w_ada, b_ada, ada_table, w_in, w_out, s5_a_re, s5_a_im, s5_log_dt, s5_b_re, s5_b_im, s5_c_re, s5_c_im, s5_d, s5_glu_w, s5_glu_b, hg_lb_raw, hg_norm_gain, hg_norm_bias, gla_w_gate, gla_b_gate, gla_norm_gain, w_ffn_gate, w_ffn_up, w_ffn_down, ln1_gain, ln1_bias, ln2_gain, ln2_bias):
    bsz, seq, d = x.shape
    depth = w_in.shape[0]
    n = bsz * seq
    s5_w = s5_d.shape[1]
    hg_w = hg_lb_raw.shape[1]
    gla_kw = gla_b_gate.shape[1]
    gla_vw = gla_norm_gain.shape[1]
    rank = gla_w_gate.shape[1]
    gla_dk = gla_kw // GLA_HEADS
    gla_dv = gla_vw // GLA_HEADS
    dk_pad = ((gla_dk + GLA_DK_PAD - 1) // GLA_DK_PAD) * GLA_DK_PAD
    assert dk_pad - gla_dk >= rank
    alpha = (2.0 * depth) ** 0.25

    rows = ((bsz + 7) // 8) * 8
    c_pad = jnp.pad(c, ((0, rows - bsz), (0, 0)))
    mod = cond_table(c_pad, w_ada, b_ada[None, :], ada_table.reshape(depth, N_MOD * d))
    mod = mod[:, :bsz].reshape(depth, bsz, N_MOD, d)

    o_u = 0
    o_hg = o_u + s5_w
    o_q = o_hg + 4 * hg_w
    o_k = o_q + gla_kw
    o_v = o_k + gla_kw
    o_lr = o_v + 2 * gla_vw

    hk = GLA_HEADS * dk_pad
    w_q = jnp.concatenate(
        [w_in[:, :, o_q:o_k].reshape(depth, d, GLA_HEADS, gla_dk),
         jnp.broadcast_to(w_in[:, :, None, o_lr:o_lr + rank], (depth, d, GLA_HEADS, rank)),
         jnp.zeros((depth, d, GLA_HEADS, dk_pad - gla_dk - rank), F32)], axis=3).reshape(depth, d, hk)
    w_k = jnp.pad(w_in[:, :, o_k:o_v].reshape(depth, d, GLA_HEADS, gla_dk),
                  ((0, 0), (0, 0), (0, 0), (0, dk_pad - gla_dk))).reshape(depth, d, hk)
    w_all = jnp.concatenate([w_in[:, :, o_v:o_lr], w_in[:, :, o_hg:o_q], w_in[:, :, o_u:o_hg], w_q, w_k],
                            axis=2).astype(BF16)
    p_v = 0
    p_g = p_v + gla_vw
    p_hg = p_g + gla_vw
    p_u = p_hg + 4 * hg_w
    p_q = p_u + s5_w
    p_k = p_q + hk
    w_out_bf = w_out.astype(BF16)
    w_gate_bf = w_ffn_gate.astype(BF16)
    w_up_bf = w_ffn_up.astype(BF16)
    w_down_bf = w_ffn_down.astype(BF16)

    x2 = x.reshape(n, d)
    h = modulate(x2, mod[0], seq)
    for l in range(depth):
        proj = matmul_ws([h], w_all, l, F32)
        y_a = s5_mixer(proj, p_u, s5_a_re[l], s5_a_im[l], s5_log_dt[l], s5_b_re[l],
                       s5_b_im[l], s5_c_re[l], s5_c_im[l], s5_d[l], s5_glu_w, l, s5_glu_b[l],
                       batch=bsz, seq=seq)
        y_b = hgrn2(proj, p_hg, hg_lb_raw, hg_norm_gain[l][None, :], hg_norm_bias[l][None, :],
                    layer=l, batch=bsz, seq=seq)
        w_gate = jnp.pad(
            jnp.pad(gla_w_gate[l].reshape(rank, GLA_HEADS, gla_dk), ((0, 0), (0, 0), (0, dk_pad - gla_dk))),
            ((gla_dk, dk_pad - gla_dk - rank), (0, 0), (0, 0))).reshape(dk_pad, hk).astype(BF16)
        b_gate = _pad_heads(gla_b_gate[l][None, :], GLA_HEADS, gla_dk, dk_pad)
        y_c = gla(proj, p_q, p_k, p_v, p_g, w_gate, b_gate, gla_norm_gain[l][None, :],
                  batch=bsz, seq=seq, dk=dk_pad, dv=gla_dv, q_scale=float(gla_dk) ** -0.5)
        mixed = matmul_ws([y_a, y_b, y_c], w_out_bf, l, BF16)
        x2, h = ln_mod(x2, mixed, mod[l], mod[l], ln1_gain[l][None, :], ln1_bias[l][None, :], seq,
                       alpha=alpha, gate_row=2, next_row=3, with_h=True)

        act = ffn_up(h, w_gate_bf, w_up_bf, l)
        ffn = matmul_ws([act], w_down_bf, l, BF16, tn=512)
        last = l == depth - 1
        x2, h = ln_mod(x2, ffn, mod[l], mod[l if last else l + 1],
                       ln2_gain[l][None, :], ln2_bias[l][None, :], seq,
                       alpha=alpha, gate_row=5, next_row=0, with_h=not last)
    return x2.reshape(bsz, seq, d)
```

```python
import functools
import math

import numpy as np
import jax
import jax.numpy as jnp
from jax import lax
from jax.experimental import pallas as pl
from jax.experimental.pallas import tpu as pltpu

F32 = jnp.float32
BF16 = jnp.bfloat16

LANES = 128
V7X_VMEM_BYTES = 64 * 1024 * 1024
VMEM_LIMIT = V7X_VMEM_BYTES - 8 * 1024 * 1024

S5_GROUP = 16
HG_HEAD = 128
GLA_HEADS = 4
GLA_GATE_TEMP = 16.0
N_MOD = 6
LN_EPS = 1e-5

S5_T = 16
GLA_CHUNK = 128
GLA_DK_PAD = 256


def _cparams(n_axes):
    return pltpu.CompilerParams(
        dimension_semantics=("arbitrary",) * n_axes, vmem_limit_bytes=VMEM_LIMIT)


def _sigmoid(x):
    return 1.0 / (1.0 + jnp.exp(-x))


def _dot(a, b):
    return jnp.dot(a, b, preferred_element_type=F32)


def _dot_nt(a, b, precision=None):
    return lax.dot_general(a, b, (((1,), (1,)), ((), ())),
                           preferred_element_type=F32, precision=precision)


def _dot_tn(a, b):
    return lax.dot_general(a, b, (((0,), (0,)), ((), ())), preferred_element_type=F32)


def _pick_tile(n, cap):
    best = None
    for t in range(LANES, min(n, cap) + 1, LANES):
        if n % t == 0:
            best = t
    assert best is not None, (n, cap)
    return best


def _mm_ws_kernel(*refs, n_a):
    a_refs, w_ref, o_ref = refs[:n_a], refs[n_a], refs[n_a + 1]
    acc = None
    r0 = 0
    for a_ref in a_refs:
        k = a_ref.shape[1]
        part = _dot(a_ref[...], w_ref[r0:r0 + k, :])
        acc = part if acc is None else acc + part
        r0 += k
    o_ref[...] = acc.astype(o_ref.dtype)


def matmul_ws(a_list, w_stack, layer, out_dtype, tm=512, tn=1024):
    m = a_list[0].shape[0]
    _, k, n = w_stack.shape
    assert sum(a.shape[1] for a in a_list) == k
    tn = _pick_tile(n, tn)
    tm = min(tm, m)
    return pl.pallas_call(
        functools.partial(_mm_ws_kernel, n_a=len(a_list)),
        grid=(n // tn, m // tm),
        in_specs=[pl.BlockSpec((tm, a.shape[1]), lambda j, i: (i, 0)) for a in a_list]
        + [pl.BlockSpec((None, k, tn), lambda j, i: (layer, 0, j))],
        out_specs=pl.BlockSpec((tm, tn), lambda j, i: (i, j)),
        out_shape=jax.ShapeDtypeStruct((m, n), out_dtype),
        compiler_params=_cparams(2),
        name="matmul_ws",
    )(*a_list, w_stack)


def _ffn_up_kernel(a_ref, wg_ref, wu_ref, *refs, n_side):
    side_in, o_ref, side_out = refs[:n_side], refs[n_side], refs[n_side + 1:]
    a = a_ref[...]
    g = _dot(a, wg_ref[...])
    u = _dot(a, wu_ref[...])
    o_ref[...] = (g * _sigmoid(g) * u).astype(o_ref.dtype)
    for s_in, s_out in zip(side_in, side_out):
        s_out[...] = s_in[...].astype(s_out.dtype)


def _slab_specs(shape, layer, gi, gj):
    r, c = shape
    if r % gi == 0 and c % gj == 0 and (r // gi) % 16 == 0 and (c // gj) % LANES == 0:
        blk = (r // gi, c // gj)
        return (pl.BlockSpec((None,) + blk, lambda i, j: (layer, i, j)), pl.BlockSpec(blk, lambda i, j: (i, j)))
    steps = gi * gj
    if r % steps == 0 and (r // steps) % 16 == 0:
        blk = (r // steps, c)
        return (pl.BlockSpec((None,) + blk, lambda i, j: (layer, i * gj + j, 0)),
                pl.BlockSpec(blk, lambda i, j: (i * gj + j, 0)))
    return None


def ffn_up(h, wg, wu, cast_jobs, tm=2048, tn=256):
    m, k = h.shape
    n = wg.shape[1]
    tn = _pick_tile(n, tn)
    tm = min(tm, m)
    gi, gj = m // tm, n // tn
    specs = [_slab_specs(w.shape[1:], layer, gi, gj) for w, layer in cast_jobs]
    riding = [job for job, sp in zip(cast_jobs, specs) if sp is not None]
    rspecs = [sp for sp in specs if sp is not None]
    wspec = pl.BlockSpec((k, tn), lambda i, j: (0, j))
    res = pl.pallas_call(
        functools.partial(_ffn_up_kernel, n_side=len(riding)),
        grid=(gi, gj),
        in_specs=[pl.BlockSpec((tm, k), lambda i, j: (i, 0)), wspec, wspec] + [sp[0] for sp in rspecs],
        out_specs=[pl.BlockSpec((tm, tn), lambda i, j: (i, j))] + [sp[1] for sp in rspecs],
        out_shape=[jax.ShapeDtypeStruct((m, n), BF16)]
        + [jax.ShapeDtypeStruct(w.shape[1:], BF16) for w, _ in riding],
        compiler_params=_cparams(2),
        name="ffn_up",
    )(h, wg, wu, *[w for w, _ in riding])
    casts, it = [], iter(res[1:])
    for (w, layer), sp in zip(cast_jobs, specs):
        casts.append(next(it) if sp is not None else w[layer].astype(BF16))
    return res[0], casts


def _cond_kernel(c_ref, w_ref, b_ref, tab_ref, o_ref):
    c = c_ref[...]
    act = (c * _sigmoid(c)).astype(BF16)
    cond = _dot(act, w_ref[...].astype(BF16)) + b_ref[...]
    for l in range(tab_ref.shape[0]):
        o_ref[l] = cond + tab_ref[l:l + 1, :]


def cond_table(c_pad, w_ada, b_ada, ada_table2, tn=1024):
    rows, d = c_pad.shape
    n = w_ada.shape[1]
    depth = ada_table2.shape[0]
    tn = _pick_tile(n, tn)
    return pl.pallas_call(
        _cond_kernel,
        grid=(n // tn,),
        in_specs=[pl.BlockSpec((rows, d), lambda j: (0, 0)),
                  pl.BlockSpec((d, tn), lambda j: (0, j)),
                  pl.BlockSpec((1, tn), lambda j: (0, j)),
                  pl.BlockSpec((depth, tn), lambda j: (0, j))],
        out_specs=pl.BlockSpec((depth, rows, tn), lambda j: (0, 0, j)),
        out_shape=jax.ShapeDtypeStruct((depth, rows, n), F32),
        compiler_params=_cparams(1),
        name="cond_table",
    )(c_pad, w_ada, b_ada, ada_table2)


def _modulate_kernel(x_ref, mod_ref, h_ref):
    m = mod_ref[0]
    h_ref[...] = (x_ref[...] * (1.0 + m[1:2, :]) + m[0:1, :]).astype(h_ref.dtype)


def modulate(x2, mod_l, seq, tm=512):
    n, d = x2.shape
    tm = min(tm, seq)
    per_b = seq // tm
    return pl.pallas_call(
        _modulate_kernel,
        grid=(n // tm,),
        in_specs=[pl.BlockSpec((tm, d), lambda i: (i, 0)),
                  pl.BlockSpec((1, N_MOD, d), lambda i: (i // per_b, 0, 0))],
        out_specs=pl.BlockSpec((tm, d), lambda i: (i, 0)),
        out_shape=jax.ShapeDtypeStruct((n, d), BF16),
        compiler_params=_cparams(1),
        name="modulate",
    )(x2, mod_l)


def _ln_mod_kernel(x_ref, mm_ref, mod_ref, nmod_ref, gain_ref, bias_ref, xo_ref, *h_refs,
                   alpha, gate_row, next_row):
    m = mod_ref[0]
    z = alpha * x_ref[...] + (1.0 + m[gate_row:gate_row + 1, :]) * mm_ref[...].astype(F32)
    mu = jnp.mean(z, axis=-1, keepdims=True)
    zc = z - mu
    var = jnp.mean(zc * zc, axis=-1, keepdims=True)
    y = zc * lax.rsqrt(var + LN_EPS) * gain_ref[...] + bias_ref[...]
    xo_ref[...] = y
    if h_refs:
        nm = nmod_ref[0]
        h_refs[0][...] = (y * (1.0 + nm[next_row + 1:next_row + 2, :])
                          + nm[next_row:next_row + 1, :]).astype(BF16)


def ln_mod(x2, mm, mod_l, mod_next, gain, bias, seq, *, alpha, gate_row, next_row, with_h, tm=256):
    n, d = x2.shape
    tm = min(tm, seq)
    per_b = seq // tm
    row = pl.BlockSpec((tm, d), lambda i: (i, 0))
    modspec = pl.BlockSpec((1, N_MOD, d), lambda i: (i // per_b, 0, 0))
    vec = pl.BlockSpec((1, d), lambda i: (0, 0))
    out_shape = [jax.ShapeDtypeStruct((n, d), F32)]
    out_specs = [row]
    if with_h:
        out_shape.append(jax.ShapeDtypeStruct((n, d), BF16))
        out_specs.append(row)
    res = pl.pallas_call(
        functools.partial(_ln_mod_kernel, alpha=alpha, gate_row=gate_row, next_row=next_row),
        grid=(n // tm,),
        in_specs=[row, row, modspec, modspec, vec, vec],
        out_specs=out_specs,
        out_shape=out_shape,
        compiler_params=_cparams(1),
        name="ln_mod",
    )(x2, mm, mod_l, mod_next, gain, bias)
    return (res[0], res[1]) if with_h else (res[0], None)


def _gla_levels(chunk):
    lv = []
    h = chunk // 2
    while h >= 1:
        lv.append(h)
        h //= 2
    return lv


def _gla_sum_matrix(chunk):
    blocks = []
    idx = np.arange(chunk)
    for h in _gla_levels(chunk):
        m = np.zeros((chunk, chunk), np.float32)
        for i in range(chunk):
            r = (i // (2 * h)) * 2 * h + h - 1
            if i % (2 * h) >= h:
                m[i, r + 1:i + 1] = 1.0
            else:
                m[i, i + 1:r + 1] = 1.0
        blocks.append(m)
    blocks.append((idx[None, :] <= idx[:, None]).astype(np.float32))
    blocks.append((idx[None, :] > idx[:, None]).astype(np.float32))
    m = np.concatenate(blocks, axis=0)
    return np.concatenate([m, m], axis=1)


def _gla_chunk_heads(q, k, v, g, s_ref, msum, chunk, dk, dv, heads):
    levels = _gla_levels(chunk)
    g_hi = g.astype(BF16)
    g_lo = (g - g_hi.astype(F32)).astype(BF16)
    expo = _dot(msum, jnp.concatenate([g_hi, g_lo], axis=0))
    e_all = jnp.exp(expo)

    row = lax.broadcasted_iota(jnp.int32, (chunk, chunk), 0)
    col = lax.broadcasted_iota(jnp.int32, (chunk, chunk), 1)
    outs = []
    for hd in range(heads):
        ks = slice(hd * dk, (hd + 1) * dk)
        qh, kh = q[:, ks], k[:, ks]
        vh = v[:, hd * dv:(hd + 1) * dv].astype(BF16)
        a = jnp.where(row == col, _dot_nt(qh.astype(BF16), kh.astype(BF16)), 0.0)
        for li, h in enumerate(levels):
            e = e_all[li * chunk:(li + 1) * chunk, ks]
            p = _dot_nt((qh * e).astype(BF16), (kh * e).astype(BF16))
            blk = 2 * h
            if blk == chunk:
                mask = (row >= h) & (col < h)
            else:
                mask = ((row // blk) == (col // blk)) & ((row % blk) >= h) & ((col % blk) < h)
            a = jnp.where(mask, p, a)
        nl = len(levels)
        e_cum = e_all[nl * chunk:(nl + 1) * chunk, ks]
        e_rev = e_all[(nl + 1) * chunk:(nl + 2) * chunk, ks]
        s_t = s_ref[hd]
        o = _dot(a.astype(BF16), vh) + _dot_nt((qh * e_cum).astype(BF16), s_t.astype(BF16))
        kb = (kh * e_rev).astype(BF16)
        s_ref[hd] = s_t * e_cum[chunk - 1:chunk, :] + _dot_tn(vh, kb)
        outs.append(o)
    return outs


def _hgrn_kernel(q_ref, f_ref, i_ref, g_ref, lbraw_ref, gain_ref, bias_ref, msum_ref,
                 o_ref, s_ref, *, layer, heads, chunk):
    dk = dv = HG_HEAD
    tb = q_ref.shape[0]

    @pl.when(pl.program_id(2) == 0)
    def _():
        s_ref[...] = jnp.zeros_like(s_ref)

    raw = lbraw_ref[...]
    ex = jnp.exp(raw - jnp.max(raw, axis=0, keepdims=True))
    sm = ex / jnp.sum(ex, axis=0, keepdims=True)
    lb = jnp.zeros_like(sm[0:1, :])
    for m in range(1, layer + 1):
        lb = lb + sm[m:m + 1, :]
    gain = gain_ref[...]
    bias = bias_ref[...]
    msum = msum_ref[...]

    def body(c, carry):
        rows = pl.ds(c * chunk, chunk)
        q_in = q_ref[rows, :]
        f = lb + (1.0 - lb) * _sigmoid(f_ref[rows, :])
        q = q_in * _sigmoid(q_in)
        outs = _gla_chunk_heads(q, 1.0 - f, i_ref[rows, :], jnp.log(f), s_ref, msum,
                                chunk, dk, dv, heads)
        gate = _sigmoid(g_ref[rows, :])
        for hd, o in enumerate(outs):
            cs = slice(hd * dv, (hd + 1) * dv)
            y = gate[:, cs] * o
            mu = jnp.mean(y, axis=-1, keepdims=True)
            yc = y - mu
            var = jnp.mean(yc * yc, axis=-1, keepdims=True)
            yn = yc * lax.rsqrt(var + LN_EPS)
            o_ref[rows, cs] = (yn * gain[:, cs] + bias[:, cs]).astype(o_ref.dtype)
        return carry

    for c in range(tb // chunk):
        body(c, 0)


def hgrn2(proj, col0, lb_raw, gain, bias, *, layer, batch, seq, heads_per_block=4, tb=512):
    n = proj.shape[0]
    w = lb_raw.shape[1]
    nheads = w // HG_HEAD
    hpb = heads_per_block
    while nheads % hpb:
        hpb -= 1
    bw = hpb * HG_HEAD
    ngrp = nheads // hpb
    assert col0 % bw == 0
    cb = col0 // bw
    tb = min(tb, seq)
    chunk = min(GLA_CHUNK, tb)
    nt = seq // tb
    msum = jnp.asarray(_gla_sum_matrix(chunk), BF16)

    def sec(s):
        return pl.BlockSpec((tb, bw), lambda b, hg, t: (b * nt + t, cb + s * ngrp + hg))

    vec = pl.BlockSpec((1, bw), lambda b, hg, t: (0, hg))
    return pl.pallas_call(
        functools.partial(_hgrn_kernel, layer=layer, heads=hpb, chunk=chunk),
        grid=(batch, ngrp, nt),
        in_specs=[sec(0), sec(1), sec(2), sec(3),
                  pl.BlockSpec((lb_raw.shape[0], bw), lambda b, hg, t: (0, hg)),
                  vec, vec,
                  pl.BlockSpec(msum.shape, lambda b, hg, t: (0, 0))],
        out_specs=pl.BlockSpec((tb, bw), lambda b, hg, t: (b * nt + t, hg)),
        out_shape=jax.ShapeDtypeStruct((n, w), BF16),
        scratch_shapes=[pltpu.VMEM((hpb, HG_HEAD, HG_HEAD), F32)],
        compiler_params=_cparams(3),
        name="hgrn2",
    )(proj, proj, proj, proj, lb_raw, gain, bias, msum)


def _gla_kernel(q_ref, k_ref, v_ref, g_ref, wg_ref, bg_ref, gain_ref, msum_ref,
                o_ref, s_ref, *, heads, chunk, dk, dv, q_scale):
    tb = q_ref.shape[0]

    @pl.when(pl.program_id(2) == 0)
    def _():
        s_ref[...] = jnp.zeros_like(s_ref)

    wg = wg_ref[...]
    bg = bg_ref[...]
    gain = gain_ref[...]
    msum = msum_ref[...]

    def body(c, carry):
        rows = pl.ds(c * chunk, chunk)
        q_raw = q_ref[rows, :]
        q_bf = q_raw.astype(BF16)
        pre = jnp.concatenate([_dot(q_bf[:, hd * dk:(hd + 1) * dk], wg[:, hd * dk:(hd + 1) * dk])
                               for hd in range(heads)], axis=1) + bg
        log_a = (jnp.minimum(pre, 0.0) - jnp.log(1.0 + jnp.exp(-jnp.abs(pre)))) / GLA_GATE_TEMP
        outs = _gla_chunk_heads(q_raw * q_scale, k_ref[rows, :], v_ref[rows, :], log_a,
                                s_ref, msum, chunk, dk, dv, heads)
        g_in = g_ref[rows, :]
        swish = g_in * _sigmoid(g_in)
        for hd, o in enumerate(outs):
            cs = slice(hd * dv, (hd + 1) * dv)
            y = o * lax.rsqrt(jnp.mean(o * o, axis=-1, keepdims=True) + LN_EPS)
            o_ref[rows, cs] = (y * gain[:, cs] * swish[:, cs]).astype(o_ref.dtype)
        return carry

    for c in range(tb // chunk):
        body(c, 0)


def gla(proj, q0, k0, v0, g0, w_gate, b_gate, gain, *, batch, seq, dk, dv, q_scale,
        heads_per_block=4, tb=512):
    n = proj.shape[0]
    nheads = w_gate.shape[1] // dk
    hpb = min(heads_per_block, nheads)
    ngrp = nheads // hpb
    tb = min(tb, seq)
    chunk = min(GLA_CHUNK, tb)
    nt = seq // tb
    msum = jnp.asarray(_gla_sum_matrix(chunk), BF16)
    kw, vw = hpb * dk, hpb * dv
    assert q0 % kw == 0 and k0 % kw == 0 and v0 % vw == 0 and g0 % vw == 0

    def rows(width, col0):
        cb = col0 // width
        return pl.BlockSpec((tb, width), lambda b, hg, t: (b * nt + t, cb + hg))

    return pl.pallas_call(
        functools.partial(_gla_kernel, heads=hpb, chunk=chunk, dk=dk, dv=dv, q_scale=q_scale),
        grid=(batch, ngrp, nt),
        in_specs=[rows(kw, q0), rows(kw, k0), rows(vw, v0), rows(vw, g0),
                  pl.BlockSpec((w_gate.shape[0], kw), lambda b, hg, t: (0, hg)),
                  pl.BlockSpec((1, kw), lambda b, hg, t: (0, hg)),
                  pl.BlockSpec((1, vw), lambda b, hg, t: (0, hg)),
                  pl.BlockSpec(msum.shape, lambda b, hg, t: (0, 0))],
        out_specs=pl.BlockSpec((tb, vw), lambda b, hg, t: (b * nt + t, hg)),
        out_shape=jax.ShapeDtypeStruct((n, nheads * dv), BF16),
        scratch_shapes=[pltpu.VMEM((hpb, dv, dk), F32)],
        compiler_params=_cparams(3),
        name="gla",
    )(proj, proj, proj, proj, w_gate, b_gate, gain, msum)


S5_GB = LANES // S5_GROUP


def _s5_scan_steps(nchunks):
    return max(1, int(math.ceil(math.log2(nchunks)))) if nchunks > 1 else 0


def _s5_prep_kernel(ar_ref, ai_ref, ldt_ref, b2_ref, c2_ref, d_ref,
                    ktoep_ref, win_ref, wo_ref, lscan_ref, *, nsteps):
    rows, p2 = ar_ref.shape
    half = p2 // 2
    t_sub = S5_T
    ar = ar_ref[...]
    ai = ai_ref[...]
    dt = jnp.exp(ldt_ref[...])
    lane = lax.broadcasted_iota(jnp.int32, (1, p2), 1)
    sgn_im = jnp.where(lane < half, -1.0, 1.0)
    sgn_re = -sgn_im

    def lam_pow(k):
        mag = jnp.exp(float(k) * (ar * dt))
        th = float(k) * (ai * dt)
        return mag * jnp.cos(th), mag * jnp.sin(th)

    pows = [lam_pow(k) for k in range(t_sub + 1)]

    def cmul(x, k):
        l_re, l_im = pows[k]
        return x * l_re + pltpu.roll(x, half, axis=1) * (l_im * sgn_im)

    lam_re, lam_im = pows[1]
    den = ar * ar + ai * ai
    nr = lam_re - 1.0
    ni = lam_im
    coef_re = (nr * ar + ni * ai) / den
    coef_im = (ni * ar - nr * ai) / den
    b2 = b2_ref[...]
    bbar = b2 * coef_re + pltpu.roll(b2, half, axis=1) * (coef_im * sgn_im)
    c2 = c2_ref[...]

    rgrp = lax.broadcasted_iota(jnp.int32, (rows, rows), 0) // S5_GROUP
    cgrp = lax.broadcasted_iota(jnp.int32, (rows, rows), 1) // S5_GROUP
    same_grp = rgrp == cgrp
    r_i = lax.broadcasted_iota(jnp.int32, (rows, rows), 0)
    c_i = lax.broadcasted_iota(jnp.int32, (rows, rows), 1)
    hp = lax.Precision.HIGHEST

    for j in range(t_sub):
        k = t_sub - 1 - j
        tap = jnp.where(same_grp, _dot_nt(cmul(bbar, k) * sgn_re, c2, hp), 0.0)
        if k == 0:
            tap = tap + jnp.where(r_i == c_i, d_ref[...], 0.0)
        ktoep_ref[j * rows:(j + 1) * rows, :] = tap.astype(ktoep_ref.dtype)

    grp_of_row = lax.broadcasted_iota(jnp.int32, (rows, p2), 0) // S5_GROUP

    def block_diag(tile):
        return jnp.concatenate([jnp.where(grp_of_row == gg, tile, 0.0) for gg in range(S5_GB)], axis=1)

    for s in range(t_sub):
        win_ref[s * rows:(s + 1) * rows, :] = block_diag(cmul(bbar, t_sub - 1 - s)).astype(win_ref.dtype)
        wo_ref[s * rows:(s + 1) * rows, :] = block_diag(cmul(c2, s + 1) * sgn_re).astype(wo_ref.dtype)

    def group_rows(tile):
        return jnp.concatenate([tile[gg * S5_GROUP:gg * S5_GROUP + 1, :] for gg in range(S5_GB)], axis=1)

    cur_re, cur_im = pows[t_sub]
    rows_re, rows_sw = [], []
    for _ in range(nsteps):
        rows_re.append(group_rows(cur_re))
        rows_sw.append(group_rows(cur_im * sgn_im))
        cur_re, cur_im = cur_re * cur_re - cur_im * cur_im, 2.0 * cur_re * cur_im
    pad = lscan_ref.shape[0] - 2 * nsteps
    parts = rows_re + rows_sw + ([jnp.zeros((pad, S5_GB * p2), F32)] if pad else [])
    lscan_ref[...] = jnp.concatenate(parts, axis=0)


def s5_prep(ar_rows, ai_rows, ldt_rows, b2_rows, c2_rows, d_row, nsteps):
    wd, p2 = ar_rows.shape
    nblk = wd // LANES
    lrows = ((2 * nsteps + 7) // 8) * 8
    tile = pl.BlockSpec((LANES, p2), lambda i: (i, 0))
    return pl.pallas_call(
        functools.partial(_s5_prep_kernel, nsteps=nsteps),
        grid=(nblk,),
        in_specs=[tile, tile, tile, tile, tile, pl.BlockSpec((1, LANES), lambda i: (0, i))],
        out_specs=[pl.BlockSpec((None, S5_T * LANES, LANES), lambda i: (i, 0, 0)),
                   pl.BlockSpec((None, S5_T * LANES, S5_GB * p2), lambda i: (i, 0, 0)),
                   pl.BlockSpec((None, S5_T * LANES, S5_GB * p2), lambda i: (i, 0, 0)),
                   pl.BlockSpec((None, lrows, S5_GB * p2), lambda i: (i, 0, 0))],
        out_shape=[jax.ShapeDtypeStruct((nblk, S5_T * LANES, LANES), BF16),
                   jax.ShapeDtypeStruct((nblk, S5_T * LANES, S5_GB * p2), BF16),
                   jax.ShapeDtypeStruct((nblk, S5_T * LANES, S5_GB * p2), BF16),
                   jax.ShapeDtypeStruct((nblk, lrows, S5_GB * p2), F32)],
        compiler_params=_cparams(1),
        name="s5_prep",
    )(ar_rows, ai_rows, ldt_rows, b2_rows, c2_rows, d_row)


def _s5_main_kernel(u_ref, ktoep_ref, win_ref, wo_ref, lscan_ref, y_ref, *, nsteps):
    t_sub = S5_T
    nch = u_ref.shape[0] // t_sub
    p2 = win_ref.shape[1] // S5_GB
    half = p2 // 2
    xcat = jnp.concatenate([u_ref[pl.ds(s, nch, stride=t_sub), :].astype(BF16) for s in range(t_sub)],
                           axis=1)
    z = _dot(xcat, win_ref[...])
    pos = lax.broadcasted_iota(jnp.int32, (nch, p2), 0)
    lscan = lscan_ref[...]
    xprev = []
    for gg in range(S5_GB):
        cols = slice(gg * p2, (gg + 1) * p2)
        x = z[:, cols]
        for j in range(nsteps):
            d = 1 << j
            sh = jnp.where(pos >= d, pltpu.roll(x, d, axis=0), 0.0)
            x = (x + sh * lscan[j:j + 1, cols]
                 + pltpu.roll(sh, half, axis=1) * lscan[nsteps + j:nsteps + j + 1, cols])
        xprev.append(jnp.where(pos >= 1, pltpu.roll(x, 1, axis=0), 0.0).astype(BF16))
    y_state = _dot_nt(jnp.concatenate(xprev, axis=1), wo_ref[...])
    for t in range(t_sub):
        y_t = y_state[:, t * LANES:(t + 1) * LANES] + _dot(
            xcat[:, :(t + 1) * LANES], ktoep_ref[(t_sub - 1 - t) * LANES:, :])
        y_ref[pl.ds(t, nch, stride=t_sub), :] = y_t


def s5_main(proj, col0, wd, ktoep, win, wo, lscan, *, nsteps, batch, seq):
    n = proj.shape[0]
    nblk = wd // LANES
    assert col0 % LANES == 0
    cb = col0 // LANES

    def wspec(a):
        return pl.BlockSpec((None,) + a.shape[1:], lambda i, b: (i, 0, 0))

    return pl.pallas_call(
        functools.partial(_s5_main_kernel, nsteps=nsteps),
        grid=(nblk, batch),
        in_specs=[pl.BlockSpec((seq, LANES), lambda i, b: (b, cb + i)),
                  wspec(ktoep), wspec(win), wspec(wo), wspec(lscan)],
        out_specs=pl.BlockSpec((seq, LANES), lambda i, b: (b, i)),
        out_shape=jax.ShapeDtypeStruct((n, wd), F32),
        compiler_params=_cparams(2),
        name="s5_main",
    )(proj, ktoep, win, wo, lscan)


def _s5_glu_kernel(y_ref, w_ref, b_ref, o_ref, wbf_ref):
    @pl.when(pl.program_id(0) == 0)
    def _():
        wbf_ref[...] = w_ref[...].astype(BF16)

    y = y_ref[...]
    z = 0.5 * y * (1.0 + jnp.tanh(math.sqrt(2.0 / math.pi) * (y + 0.044715 * (y * y * y))))
    gate = _dot(z.astype(BF16), wbf_ref[...]) + b_ref[...]
    o_ref[...] = (z * _sigmoid(gate)).astype(o_ref.dtype)


def s5_glu(y, w_stack, layer, b, tm=512):
    n, wd = y.shape
    tm = min(tm, n)
    return pl.pallas_call(
        _s5_glu_kernel,
        grid=(n // tm,),
        in_specs=[pl.BlockSpec((tm, wd), lambda i: (i, 0)),
                  pl.BlockSpec((None, wd, wd), lambda i: (layer, 0, 0)),
                  pl.BlockSpec((1, wd), lambda i: (0, 0))],
        out_specs=pl.BlockSpec((tm, wd), lambda i: (i, 0)),
        out_shape=jax.ShapeDtypeStruct((n, wd), BF16),
        scratch_shapes=[pltpu.VMEM((wd, wd), BF16)],
        compiler_params=_cparams(1),
        name="s5_glu",
    )(y, w_stack, b)


def s5_mixer(proj, col0, a_re, a_im, log_dt, b_re, b_im, c_re, c_im, d_skip, glu_w_stack, layer, glu_b,
             *, batch, seq):
    wd = d_skip.shape[0]
    g, p = a_re.shape
    assert wd // g == S5_GROUP and seq % S5_T == 0 and wd % LANES == 0
    nsteps = _s5_scan_steps(seq // S5_T)
    per_row = lambda a: jnp.repeat(jnp.concatenate([a, a], axis=-1), S5_GROUP, axis=0)
    ldt_rows = jnp.broadcast_to(jnp.repeat(log_dt, S5_GROUP)[:, None], (wd, 2 * p))
    b2_rows = jnp.concatenate([b_re.transpose(0, 2, 1), b_im.transpose(0, 2, 1)], axis=-1).reshape(wd, 2 * p)
    c2_rows = jnp.concatenate([c_re, c_im], axis=-1).reshape(wd, 2 * p)
    ktoep, win, wo, lscan = s5_prep(per_row(a_re), per_row(a_im), ldt_rows, b2_rows, c2_rows,
                                    d_skip[None, :], nsteps)
    y = s5_main(proj, col0, wd, ktoep, win, wo, lscan, nsteps=nsteps, batch=batch, seq=seq)
    return s5_glu(y, glu_w_stack, layer, glu_b[None, :])


def _pad_heads(w, heads, width, new):
    r = w.shape[0]
    return jnp.pad(w.reshape(r, heads, width), ((0, 0), (0, 0), (0, new - width))).reshape(r, heads * new)


def kernel(x, c, w_ada, b_ada, ada_table, w_in, w_out, s5_a_re, s5_a_im, s5_log_dt, s5_b_re, s5_b_im, s5_c_re, s5_c_im, s5_d, s5_glu_w, s5_glu_b, hg_lb_raw, hg_norm_gain, hg_norm_bias, gla_w_gate, gla_b_gate, gla_norm_gain, w_ffn_gate, w_ffn_up, w_ffn_down, ln1_gain, ln1_bias, ln2_gain, ln2_bias):
    bsz, seq, d = x.shape
    depth = w_in.shape[0]
    n = bsz * seq
    s5_w = s5_d.shape[1]
    hg_w = hg_lb_raw.shape[1]
    gla_kw = gla_b_gate.shape[1]
    gla_vw = gla_norm_gain.shape[1]
    rank = gla_w_gate.shape[1]
    gla_dk = gla_kw // GLA_HEADS
    gla_dv = gla_vw // GLA_HEADS
    dk_pad = ((gla_dk + GLA_DK_PAD - 1) // GLA_DK_PAD) * GLA_DK_PAD
    assert dk_pad - gla_dk >= rank
    alpha = (2.0 * depth) ** 0.25

    rows = ((bsz + 7) // 8) * 8
    c_pad = jnp.pad(c, ((0, rows - bsz), (0, 0)))
    mod = cond_table(c_pad, w_ada, b_ada[None, :], ada_table.reshape(depth, N_MOD * d))
    mod = mod[:, :bsz].reshape(depth, bsz, N_MOD, d)

    o_u = 0
    o_hg = o_u + s5_w
    o_q = o_hg + 4 * hg_w
    o_k = o_q + gla_kw
    o_v = o_k + gla_kw
    o_lr = o_v + 2 * gla_vw

    hk = GLA_HEADS * dk_pad
    w_q = jnp.concatenate(
        [w_in[:, :, o_q:o_k].reshape(depth, d, GLA_HEADS, gla_dk),
         jnp.broadcast_to(w_in[:, :, None, o_lr:o_lr + rank], (depth, d, GLA_HEADS, rank)),
         jnp.zeros((depth, d, GLA_HEADS, dk_pad - gla_dk - rank), F32)], axis=3).reshape(depth, d, hk)
    w_k = jnp.pad(w_in[:, :, o_k:o_v].reshape(depth, d, GLA_HEADS, gla_dk),
                  ((0, 0), (0, 0), (0, 0), (0, dk_pad - gla_dk))).reshape(depth, d, hk)
    w_all = jnp.concatenate([w_in[:, :, o_v:o_lr], w_in[:, :, o_hg:o_q], w_in[:, :, o_u:o_hg], w_q, w_k],
                            axis=2).astype(BF16)
    p_v = 0
    p_g = p_v + gla_vw
    p_hg = p_g + gla_vw
    p_u = p_hg + 4 * hg_w
    p_q = p_u + s5_w
    p_k = p_q + hk
    w_out_bf = w_out.astype(BF16)
    w_gate_bf = w_ffn_gate[0].astype(BF16)
    w_up_bf = w_ffn_up[0].astype(BF16)

    x2 = x.reshape(n, d)
    h = modulate(x2, mod[0], seq)
    for l in range(depth):
        proj = matmul_ws([h], w_all, l, F32, tm=1024)
        y_a = s5_mixer(proj, p_u, s5_a_re[l], s5_a_im[l], s5_log_dt[l], s5_b_re[l],
                       s5_b_im[l], s5_c_re[l], s5_c_im[l], s5_d[l], s5_glu_w, l, s5_glu_b[l],
                       batch=bsz, seq=seq)
        y_b = hgrn2(proj, p_hg, hg_lb_raw, hg_norm_gain[l][None, :], hg_norm_bias[l][None, :],
                    layer=l, batch=bsz, seq=seq)
        w_gate = jnp.pad(
            jnp.pad(gla_w_gate[l].reshape(rank, GLA_HEADS, gla_dk), ((0, 0), (0, 0), (0, dk_pad - gla_dk))),
            ((gla_dk, dk_pad - gla_dk - rank), (0, 0), (0, 0))).reshape(dk_pad, hk).astype(BF16)
        b_gate = _pad_heads(gla_b_gate[l][None, :], GLA_HEADS, gla_dk, dk_pad)
        y_c = gla(proj, p_q, p_k, p_v, p_g, w_gate, b_gate, gla_norm_gain[l][None, :],
                  batch=bsz, seq=seq, dk=dk_pad, dv=gla_dv, q_scale=float(gla_dk) ** -0.5)
        mixed = matmul_ws([y_a, y_b, y_c], w_out_bf, l, BF16, tm=1024)
        x2, h = ln_mod(x2, mixed, mod[l], mod[l], ln1_gain[l][None, :], ln1_bias[l][None, :], seq,
                       alpha=alpha, gate_row=2, next_row=3, with_h=True)

        last = l == depth - 1
        jobs = [(w_ffn_down, l)] + ([] if last else [(w_ffn_gate, l + 1), (w_ffn_up, l + 1)])
        act, casts = ffn_up(h, w_gate_bf, w_up_bf, jobs)
        if not last:
            w_gate_bf, w_up_bf = casts[1], casts[2]
        ffn = matmul_ws([act], casts[0][None], 0, BF16, tn=512)
        x2, h = ln_mod(x2, ffn, mod[l], mod[l if last else l + 1],
                       ln2_gain[l][None, :], ln2_bias[l][None, :], seq,
                       alpha=alpha, gate_row=5, next_row=0, with_h=not last)
    return x2.reshape(bsz, seq, d)
```

```python
import functools
import math

import numpy as np
import jax
import jax.numpy as jnp
from jax import lax
from jax.experimental import pallas as pl
from jax.experimental.pallas import tpu as pltpu

F32 = jnp.float32
BF16 = jnp.bfloat16

LANES = 128
SUBLANES = 8
LOG2_E = 1.0 / math.log(2.0)
V7X_VMEM_BYTES = 64 * 1024 * 1024
VMEM_LIMIT = V7X_VMEM_BYTES - 8 * 1024 * 1024

S5_GROUP = 16
HG_HEAD = 128
GLA_HEADS = 4
GLA_GATE_TEMP = 16.0
N_MOD = 6
LN_EPS = 1e-5

S5_T = 16
GLA_CHUNK = 128
GLA_DK_PAD = 256


def _cparams(n_axes):
    return pltpu.CompilerParams(
        dimension_semantics=("arbitrary",) * n_axes, vmem_limit_bytes=VMEM_LIMIT)


def _sigmoid(x):
    return 1.0 / (1.0 + jnp.exp(-x))


def _dot(a, b):
    return jnp.dot(a, b, preferred_element_type=F32)


def _dot_nt(a, b, precision=None):
    return lax.dot_general(a, b, (((1,), (1,)), ((), ())),
                           preferred_element_type=F32, precision=precision)


def _dot_tn(a, b):
    return lax.dot_general(a, b, (((0,), (0,)), ((), ())), preferred_element_type=F32)


def _pick_tile(n, cap):
    best = None
    for t in range(LANES, min(n, cap) + 1, LANES):
        if n % t == 0:
            best = t
    assert best is not None, (n, cap)
    return best


def _mm_ws_kernel(*refs, n_a, w_is_nk):
    a_refs, w_ref, o_ref = refs[:n_a], refs[n_a], refs[n_a + 1]
    if w_is_nk:
        acc = _dot_nt(a_refs[0][...], w_ref[...])
    else:
        acc = None
        r0 = 0
        for a_ref in a_refs:
            k = a_ref.shape[1]
            part = _dot(a_ref[...], w_ref[r0:r0 + k, :])
            acc = part if acc is None else acc + part
            r0 += k
    o_ref[...] = acc.astype(o_ref.dtype)


def matmul_ws(a_list, w_stack, layer, out_dtype, tm=512, tn=1024, w_is_nk=False):
    m = a_list[0].shape[0]
    if w_is_nk:
        assert len(a_list) == 1
        _, n, k = w_stack.shape
    else:
        _, k, n = w_stack.shape
    assert sum(a.shape[1] for a in a_list) == k
    tn = _pick_tile(n, tn)
    tm = min(tm, m)
    if w_is_nk:
        w_spec = pl.BlockSpec((None, tn, k), lambda j, i: (layer, j, 0))
    else:
        w_spec = pl.BlockSpec((None, k, tn), lambda j, i: (layer, 0, j))
    return pl.pallas_call(
        functools.partial(_mm_ws_kernel, n_a=len(a_list), w_is_nk=w_is_nk),
        grid=(n // tn, m // tm),
        in_specs=[pl.BlockSpec((tm, a.shape[1]), lambda j, i: (i, 0)) for a in a_list] + [w_spec],
        out_specs=pl.BlockSpec((tm, tn), lambda j, i: (i, j)),
        out_shape=jax.ShapeDtypeStruct((m, n), out_dtype),
        compiler_params=_cparams(2),
        name="matmul_ws",
    )(*a_list, w_stack)


def _ffn_up_kernel(a_ref, wg_ref, wu_ref, *refs, n_side):
    side_in, o_ref, side_out = refs[:n_side], refs[n_side], refs[n_side + 1:]
    a = a_ref[...]
    g = _dot(a, wg_ref[...])
    u = _dot(a, wu_ref[...])
    o_ref[...] = (g * _sigmoid(g) * u).astype(o_ref.dtype)
    for s_in, s_out in zip(side_in, side_out):
        s_out[...] = s_in[...].astype(s_out.dtype)


def _slab_specs(shape, layer, gi, gj):
    r, c = shape
    cw = -(-(-(-c // gj)) // LANES) * LANES
    if r % gi == 0 and (r // gi) % 16 == 0 and -(-c // cw) == gj:
        blk = (r // gi, cw)
        return (pl.BlockSpec((None,) + blk, lambda i, j: (layer, i, j)), pl.BlockSpec(blk, lambda i, j: (i, j)))
    steps = gi * gj
    for rs in range(16, r + 1, 16):
        if r % rs == 0 and r // rs <= steps:
            last = r // rs - 1
            blk = (rs, c)
            return (pl.BlockSpec((None,) + blk, lambda i, j: (layer, jnp.minimum(i * gj + j, last), 0)),
                    pl.BlockSpec(blk, lambda i, j: (jnp.minimum(i * gj + j, last), 0)))
    return None


def ffn_up(h, wg, wu, cast_jobs, tm=2048, tn=256):
    m, k = h.shape
    n = wg.shape[1]
    tm = min(tm, m)
    gi, gj = m // tm, pl.cdiv(n, tn)
    specs = [_slab_specs(w.shape[1:], layer, gi, gj) for w, layer in cast_jobs]
    riding = [job for job, sp in zip(cast_jobs, specs) if sp is not None]
    rspecs = [sp for sp in specs if sp is not None]
    wspec = pl.BlockSpec((k, tn), lambda i, j: (0, j))
    res = pl.pallas_call(
        functools.partial(_ffn_up_kernel, n_side=len(riding)),
        grid=(gi, gj),
        in_specs=[pl.BlockSpec((tm, k), lambda i, j: (i, 0)), wspec, wspec] + [sp[0] for sp in rspecs],
        out_specs=[pl.BlockSpec((tm, tn), lambda i, j: (i, j))] + [sp[1] for sp in rspecs],
        out_shape=[jax.ShapeDtypeStruct((m, n), BF16)]
        + [jax.ShapeDtypeStruct(w.shape[1:], BF16) for w, _ in riding],
        compiler_params=_cparams(2),
        name="ffn_up",
    )(h, wg, wu, *[w for w, _ in riding])
    casts, it = [], iter(res[1:])
    for (w, layer), sp in zip(cast_jobs, specs):
        casts.append(next(it) if sp is not None else w[layer].astype(BF16))
    return res[0], casts


def _cond_kernel(c_ref, w_ref, b_ref, tab_ref, o_ref):
    c = c_ref[...]
    act = (c * _sigmoid(c)).astype(BF16)
    cond = _dot(act, w_ref[...].astype(BF16)) + b_ref[...]
    for l in range(tab_ref.shape[0]):
        o_ref[l] = cond + tab_ref[l:l + 1, :]


def cond_table(c_pad, w_ada, b_ada, ada_table2, tn=1024):
    rows, d = c_pad.shape
    n = w_ada.shape[1]
    depth = ada_table2.shape[0]
    tn = _pick_tile(n, tn)
    return pl.pallas_call(
        _cond_kernel,
        grid=(n // tn,),
        in_specs=[pl.BlockSpec((rows, d), lambda j: (0, 0)),
                  pl.BlockSpec((d, tn), lambda j: (0, j)),
                  pl.BlockSpec((1, tn), lambda j: (0, j)),
                  pl.BlockSpec((depth, tn), lambda j: (0, j))],
        out_specs=pl.BlockSpec((depth, rows, tn), lambda j: (0, 0, j)),
        out_shape=jax.ShapeDtypeStruct((depth, rows, n), F32),
        compiler_params=_cparams(1),
        name="cond_table",
    )(c_pad, w_ada, b_ada, ada_table2)


def _modulate_kernel(x_ref, mod_ref, h_ref):
    m = mod_ref[0]
    h_ref[...] = (x_ref[...] * (1.0 + m[1:2, :]) + m[0:1, :]).astype(h_ref.dtype)


def modulate(x2, mod_l, seq, tm=512):
    n, d = x2.shape
    tm = min(tm, seq)
    per_b = seq // tm
    return pl.pallas_call(
        _modulate_kernel,
        grid=(n // tm,),
        in_specs=[pl.BlockSpec((tm, d), lambda i: (i, 0)),
                  pl.BlockSpec((1, N_MOD, d), lambda i: (i // per_b, 0, 0))],
        out_specs=pl.BlockSpec((tm, d), lambda i: (i, 0)),
        out_shape=jax.ShapeDtypeStruct((n, d), BF16),
        compiler_params=_cparams(1),
        name="modulate",
    )(x2, mod_l)


def _ln_mod_kernel(x_ref, mm_ref, mod_ref, nmod_ref, gain_ref, bias_ref, xo_ref, *h_refs,
                   alpha, gate_row, next_row):
    m = mod_ref[0]
    z = alpha * x_ref[...] + (1.0 + m[gate_row:gate_row + 1, :]) * mm_ref[...].astype(F32)
    mu = jnp.mean(z, axis=-1, keepdims=True)
    zc = z - mu
    var = jnp.mean(zc * zc, axis=-1, keepdims=True)
    y = zc * lax.rsqrt(var + LN_EPS) * gain_ref[...] + bias_ref[...]
    xo_ref[...] = y
    if h_refs:
        nm = nmod_ref[0]
        h_refs[0][...] = (y * (1.0 + nm[next_row + 1:next_row + 2, :])
                          + nm[next_row:next_row + 1, :]).astype(BF16)


def ln_mod(x2, mm, mod_l, mod_next, gain, bias, seq, *, alpha, gate_row, next_row, with_h, tm=256):
    n, d = x2.shape
    tm = min(tm, seq)
    per_b = seq // tm
    row = pl.BlockSpec((tm, d), lambda i: (i, 0))
    modspec = pl.BlockSpec((1, N_MOD, d), lambda i: (i // per_b, 0, 0))
    vec = pl.BlockSpec((1, d), lambda i: (0, 0))
    out_shape = [jax.ShapeDtypeStruct((n, d), F32)]
    out_specs = [row]
    if with_h:
        out_shape.append(jax.ShapeDtypeStruct((n, d), BF16))
        out_specs.append(row)
    res = pl.pallas_call(
        functools.partial(_ln_mod_kernel, alpha=alpha, gate_row=gate_row, next_row=next_row),
        grid=(n // tm,),
        in_specs=[row, row, modspec, modspec, vec, vec],
        out_specs=out_specs,
        out_shape=out_shape,
        compiler_params=_cparams(1),
        name="ln_mod",
    )(x2, mm, mod_l, mod_next, gain, bias)
    return (res[0], res[1]) if with_h else (res[0], None)


def _gla_levels(chunk):
    lv = []
    h = chunk // 2
    while h >= 1:
        lv.append(h)
        h //= 2
    return lv


def _gla_sum_matrix(chunk):
    blocks = []
    idx = np.arange(chunk)
    for h in _gla_levels(chunk):
        m = np.zeros((chunk, chunk), np.float32)
        for i in range(chunk):
            r = (i // (2 * h)) * 2 * h + h - 1
            if i % (2 * h) >= h:
                m[i, r + 1:i + 1] = 1.0
            else:
                m[i, i + 1:r + 1] = 1.0
        blocks.append(m)
    blocks.append((idx[None, :] <= idx[:, None]).astype(np.float32))
    m = np.concatenate(blocks, axis=0)
    return np.concatenate([m, m], axis=1)


def _gla_pair_masks(chunk):
    ngrp = chunk // SUBLANES
    col = lax.broadcasted_iota(jnp.int32, (SUBLANES, chunk), 1)
    rows = [lax.broadcasted_iota(jnp.int32, (SUBLANES, chunk), 0) + r * SUBLANES for r in range(ngrp)]
    diag = [row == col for row in rows]
    pair = []
    for h in _gla_levels(chunk):
        blk = 2 * h
        pair.append([((row // blk) == (col // blk)) & ((row % blk) >= h) & ((col % blk) < h) for row in rows])
    rid = lax.broadcasted_iota(jnp.int32, (chunk, 1), 0)
    right = [(rid % (2 * h)) >= h for h in _gla_levels(chunk)]
    return diag, pair, right


def _row_groups(x, groups):
    parts, start, prev = [], None, None
    for r in groups:
        if start is None:
            start = r
        elif r != prev + 1:
            parts.append(x[start * SUBLANES:(prev + 1) * SUBLANES])
            start = r
        prev = r
    parts.append(x[start * SUBLANES:(prev + 1) * SUBLANES])
    return parts


def _gla_chunk_heads(q, k, v, g, s_ref, msum, masks, chunk, dk, dv, heads):
    levels = _gla_levels(chunk)
    nl = len(levels)
    ngrp = chunk // SUBLANES
    g2 = g * LOG2_E
    g_hi = g2.astype(BF16)
    g_lo = (g2 - g_hi.astype(F32)).astype(BF16)
    expo = _dot(msum, jnp.concatenate([g_hi, g_lo], axis=0))
    e_lv = jnp.exp2(expo[:nl * chunk, :])
    b_cum = expo[nl * chunk:, :]
    e_cum_all = jnp.exp2(b_cum)
    e_rev_all = jnp.exp2(b_cum[chunk - 1:chunk, :] - b_cum)

    diag, pair, right = masks
    outs = []
    for hd in range(heads):
        ks = slice(hd * dk, (hd + 1) * dk)
        qh, kh = q[:, ks], k[:, ks]
        vh = v[:, hd * dv:(hd + 1) * dv]
        dsum = jnp.sum(qh * kh, axis=-1, keepdims=True)
        a_rows = [jnp.where(diag[r], dsum[r * SUBLANES:(r + 1) * SUBLANES], 0.0) for r in range(ngrp)]
        for li, h in enumerate(levels):
            e = e_lv[li * chunk:(li + 1) * chunk, ks]
            if h >= SUBLANES:
                rgt = [r for r in range(ngrp) if (r * SUBLANES) % (2 * h) >= h]
                runs, cur = [], None
                for r in range(ngrp):
                    src = r in rgt
                    if cur is None or cur[0] != src:
                        cur = [src, r, r]
                        runs.append(cur)
                    else:
                        cur[2] = r
                sel = jnp.concatenate([(qh if src else kh)[a * SUBLANES:(b + 1) * SUBLANES]
                                       for src, a, b in runs], axis=0)
                prod = sel * e
                lhs = jnp.concatenate(_row_groups(prod, rgt), axis=0).astype(BF16)
                p = _dot_nt(lhs, prod.astype(BF16))
                for n_, r in enumerate(rgt):
                    a_rows[r] = jnp.where(pair[li][r], p[n_ * SUBLANES:(n_ + 1) * SUBLANES], a_rows[r])
            else:
                xb = (jnp.where(right[li], qh, kh) * e).astype(BF16)
                p = _dot_nt(xb, xb)
                a_rows = [jnp.where(pair[li][r], p[r * SUBLANES:(r + 1) * SUBLANES], a_rows[r])
                          for r in range(ngrp)]
        a = jnp.concatenate(a_rows, axis=0)
        e_cum = e_cum_all[:, ks]
        s_t = s_ref[hd]
        vt = vh.T.astype(BF16)
        o = _dot_nt(jnp.concatenate([a.astype(BF16), (qh * e_cum).astype(BF16)], axis=1),
                    jnp.concatenate([vt, s_t.astype(BF16)], axis=1))
        kb = (kh * e_rev_all[:, ks]).astype(BF16)
        s_ref[hd] = s_t * e_cum[chunk - 1:chunk, :] + _dot(vt, kb)
        outs.append(o)
    return outs


def _hgrn_kernel(q_ref, f_ref, i_ref, g_ref, lbraw_ref, gain_ref, bias_ref, msum_ref,
                 o_ref, s_ref, *, layer, heads, chunk):
    dk = dv = HG_HEAD
    tb = q_ref.shape[0]

    @pl.when(pl.program_id(2) == 0)
    def _():
        s_ref[...] = jnp.zeros_like(s_ref)

    raw = lbraw_ref[...]
    ex = jnp.exp(raw - jnp.max(raw, axis=0, keepdims=True))
    sm = ex / jnp.sum(ex, axis=0, keepdims=True)
    lb = jnp.zeros_like(sm[0:1, :])
    for m in range(1, layer + 1):
        lb = lb + sm[m:m + 1, :]
    gain = gain_ref[...]
    bias = bias_ref[...]
    msum = msum_ref[...]
    masks = _gla_pair_masks(chunk)

    def body(c, carry):
        rows = pl.ds(c * chunk, chunk)
        q_in = q_ref[rows, :]
        f = lb + (1.0 - lb) * _sigmoid(f_ref[rows, :])
        q = q_in * _sigmoid(q_in)
        outs = _gla_chunk_heads(q, 1.0 - f, i_ref[rows, :], jnp.log(f), s_ref, msum, masks,
                                chunk, dk, dv, heads)
        gate = _sigmoid(g_ref[rows, :])
        for hd, o in enumerate(outs):
            cs = slice(hd * dv, (hd + 1) * dv)
            y = gate[:, cs] * o
            mu = jnp.mean(y, axis=-1, keepdims=True)
            yc = y - mu
            var = jnp.mean(yc * yc, axis=-1, keepdims=True)
            yn = yc * lax.rsqrt(var + LN_EPS)
            o_ref[rows, cs] = (yn * gain[:, cs] + bias[:, cs]).astype(o_ref.dtype)
        return carry

    for c in range(tb // chunk):
        body(c, 0)


def hgrn2(proj, col0, lb_raw, gain, bias, *, layer, batch, seq, heads_per_block=4, tb=512):
    n = proj.shape[0]
    w = lb_raw.shape[1]
    nheads = w // HG_HEAD
    hpb = heads_per_block
    while nheads % hpb:
        hpb -= 1
    bw = hpb * HG_HEAD
    ngrp = nheads // hpb
    assert col0 % bw == 0
    cb = col0 // bw
    tb = min(tb, seq)
    chunk = min(GLA_CHUNK, tb)
    nt = seq // tb
    msum = jnp.asarray(_gla_sum_matrix(chunk), BF16)

    def sec(s):
        return pl.BlockSpec((tb, bw), lambda b, hg, t: (b * nt + t, cb + s * ngrp + hg))

    vec = pl.BlockSpec((1, bw), lambda b, hg, t: (0, hg))
    return pl.pallas_call(
        functools.partial(_hgrn_kernel, layer=layer, heads=hpb, chunk=chunk),
        grid=(batch, ngrp, nt),
        in_specs=[sec(0), sec(1), sec(2), sec(3),
                  pl.BlockSpec((lb_raw.shape[0], bw), lambda b, hg, t: (0, hg)),
                  vec, vec,
                  pl.BlockSpec(msum.shape, lambda b, hg, t: (0, 0))],
        out_specs=pl.BlockSpec((tb, bw), lambda b, hg, t: (b * nt + t, hg)),
        out_shape=jax.ShapeDtypeStruct((n, w), BF16),
        scratch_shapes=[pltpu.VMEM((hpb, HG_HEAD, HG_HEAD), F32)],
        compiler_params=_cparams(3),
        name="hgrn2",
    )(proj, proj, proj, proj, lb_raw, gain, bias, msum)


def _gla_kernel(q_ref, k_ref, v_ref, g_ref, wg_ref, bg_ref, gain_ref, msum_ref,
                o_ref, s_ref, *, heads, chunk, dk, dv, q_scale):
    tb = q_ref.shape[0]

    @pl.when(pl.program_id(2) == 0)
    def _():
        s_ref[...] = jnp.zeros_like(s_ref)

    wg = wg_ref[...]
    bg = bg_ref[...]
    gain = gain_ref[...]
    msum = msum_ref[...]
    masks = _gla_pair_masks(chunk)

    def body(c, carry):
        rows = pl.ds(c * chunk, chunk)
        q_raw = q_ref[rows, :]
        q_bf = q_raw.astype(BF16)
        pre = jnp.concatenate([_dot(q_bf[:, hd * dk:(hd + 1) * dk], wg[:, hd * dk:(hd + 1) * dk])
                               for hd in range(heads)], axis=1) + bg
        log_a = (jnp.minimum(pre, 0.0) - jnp.log(1.0 + jnp.exp(-jnp.abs(pre)))) / GLA_GATE_TEMP
        outs = _gla_chunk_heads(q_raw * q_scale, k_ref[rows, :], v_ref[rows, :], log_a,
                                s_ref, msum, masks, chunk, dk, dv, heads)
        g_in = g_ref[rows, :]
        swish = g_in * _sigmoid(g_in)
        for hd, o in enumerate(outs):
            cs = slice(hd * dv, (hd + 1) * dv)
            y = o * lax.rsqrt(jnp.mean(o * o, axis=-1, keepdims=True) + LN_EPS)
            o_ref[rows, cs] = (y * gain[:, cs] * swish[:, cs]).astype(o_ref.dtype)
        return carry

    for c in range(tb // chunk):
        body(c, 0)


def gla(proj, q0, k0, v0, g0, w_gate, b_gate, gain, *, batch, seq, dk, dv, q_scale,
        heads_per_block=4, tb=512):
    n = proj.shape[0]
    nheads = w_gate.shape[1] // dk
    hpb = min(heads_per_block, nheads)
    ngrp = nheads // hpb
    tb = min(tb, seq)
    chunk = min(GLA_CHUNK, tb)
    nt = seq // tb
    msum = jnp.asarray(_gla_sum_matrix(chunk), BF16)
    kw, vw = hpb * dk, hpb * dv
    assert q0 % kw == 0 and k0 % kw == 0 and v0 % vw == 0 and g0 % vw == 0

    def rows(width, col0):
        cb = col0 // width
        return pl.BlockSpec((tb, width), lambda b, hg, t: (b * nt + t, cb + hg))

    return pl.pallas_call(
        functools.partial(_gla_kernel, heads=hpb, chunk=chunk, dk=dk, dv=dv, q_scale=q_scale),
        grid=(batch, ngrp, nt),
        in_specs=[rows(kw, q0), rows(kw, k0), rows(vw, v0), rows(vw, g0),
                  pl.BlockSpec((w_gate.shape[0], kw), lambda b, hg, t: (0, hg)),
                  pl.BlockSpec((1, kw), lambda b, hg, t: (0, hg)),
                  pl.BlockSpec((1, vw), lambda b, hg, t: (0, hg)),
                  pl.BlockSpec(msum.shape, lambda b, hg, t: (0, 0))],
        out_specs=pl.BlockSpec((tb, vw), lambda b, hg, t: (b * nt + t, hg)),
        out_shape=jax.ShapeDtypeStruct((n, nheads * dv), BF16),
        scratch_shapes=[pltpu.VMEM((hpb, dv, dk), F32)],
        compiler_params=_cparams(3),
        name="gla",
    )(proj, proj, proj, proj, w_gate, b_gate, gain, msum)


S5_GB = LANES // S5_GROUP


def _s5_scan_steps(nchunks):
    return max(1, int(math.ceil(math.log2(nchunks)))) if nchunks > 1 else 0


def _s5_prep_kernel(ar_ref, ai_ref, ldt_ref, b2_ref, c2_ref, d_ref,
                    ktoep_ref, win_ref, wo_ref, lscan_ref, *, nsteps):
    rows, p2 = ar_ref.shape
    half = p2 // 2
    t_sub = S5_T
    ar = ar_ref[...]
    ai = ai_ref[...]
    dt = jnp.exp(ldt_ref[...])
    lane = lax.broadcasted_iota(jnp.int32, (1, p2), 1)
    sgn_im = jnp.where(lane < half, -1.0, 1.0)
    sgn_re = -sgn_im

    def lam_pow(k):
        mag = jnp.exp(float(k) * (ar * dt))
        th = float(k) * (ai * dt)
        return mag * jnp.cos(th), mag * jnp.sin(th)

    pows = [lam_pow(k) for k in range(t_sub + 1)]

    def cmul(x, k):
        l_re, l_im = pows[k]
        return x * l_re + pltpu.roll(x, half, axis=1) * (l_im * sgn_im)

    lam_re, lam_im = pows[1]
    den = ar * ar + ai * ai
    nr = lam_re - 1.0
    ni = lam_im
    coef_re = (nr * ar + ni * ai) / den
    coef_im = (ni * ar - nr * ai) / den
    b2 = b2_ref[...]
    bbar = b2 * coef_re + pltpu.roll(b2, half, axis=1) * (coef_im * sgn_im)
    c2 = c2_ref[...]

    rgrp = lax.broadcasted_iota(jnp.int32, (rows, rows), 0) // S5_GROUP
    cgrp = lax.broadcasted_iota(jnp.int32, (rows, rows), 1) // S5_GROUP
    same_grp = rgrp == cgrp
    r_i = lax.broadcasted_iota(jnp.int32, (rows, rows), 0)
    c_i = lax.broadcasted_iota(jnp.int32, (rows, rows), 1)
    hp = lax.Precision.HIGHEST

    for j in range(t_sub):
        k = t_sub - 1 - j
        tap = jnp.where(same_grp, _dot_nt(cmul(bbar, k) * sgn_re, c2, hp), 0.0)
        if k == 0:
            tap = tap + jnp.where(r_i == c_i, d_ref[...], 0.0)
        ktoep_ref[j * rows:(j + 1) * rows, :] = tap.astype(ktoep_ref.dtype)

    grp_of_row = lax.broadcasted_iota(jnp.int32, (rows, p2), 0) // S5_GROUP

    def block_diag(tile):
        return jnp.concatenate([jnp.where(grp_of_row == gg, tile, 0.0) for gg in range(S5_GB)], axis=1)

    for s in range(t_sub):
        win_ref[s * rows:(s + 1) * rows, :] = block_diag(cmul(bbar, t_sub - 1 - s)).astype(win_ref.dtype)
        wo_ref[s * rows:(s + 1) * rows, :] = block_diag(cmul(c2, s + 1) * sgn_re).astype(wo_ref.dtype)

    def group_rows(tile):
        return jnp.concatenate([tile[gg * S5_GROUP:gg * S5_GROUP + 1, :] for gg in range(S5_GB)], axis=1)

    cur_re, cur_im = pows[t_sub]
    rows_re, rows_sw = [], []
    for _ in range(nsteps):
        rows_re.append(group_rows(cur_re))
        rows_sw.append(group_rows(cur_im * sgn_im))
        cur_re, cur_im = cur_re * cur_re - cur_im * cur_im, 2.0 * cur_re * cur_im
    pad = lscan_ref.shape[0] - 2 * nsteps
    parts = rows_re + rows_sw + ([jnp.zeros((pad, S5_GB * p2), F32)] if pad else [])
    lscan_ref[...] = jnp.concatenate(parts, axis=0)


def s5_prep(ar_rows, ai_rows, ldt_rows, b2_rows, c2_rows, d_row, nsteps):
    wd, p2 = ar_rows.shape
    nblk = wd // LANES
    lrows = ((2 * nsteps + 7) // 8) * 8
    tile = pl.BlockSpec((LANES, p2), lambda i: (i, 0))
    return pl.pallas_call(
        functools.partial(_s5_prep_kernel, nsteps=nsteps),
        grid=(nblk,),
        in_specs=[tile, tile, tile, tile, tile, pl.BlockSpec((1, LANES), lambda i: (0, i))],
        out_specs=[pl.BlockSpec((None, S5_T * LANES, LANES), lambda i: (i, 0, 0)),
                   pl.BlockSpec((None, S5_T * LANES, S5_GB * p2), lambda i: (i, 0, 0)),
                   pl.BlockSpec((None, S5_T * LANES, S5_GB * p2), lambda i: (i, 0, 0)),
                   pl.BlockSpec((None, lrows, S5_GB * p2), lambda i: (i, 0, 0))],
        out_shape=[jax.ShapeDtypeStruct((nblk, S5_T * LANES, LANES), BF16),
                   jax.ShapeDtypeStruct((nblk, S5_T * LANES, S5_GB * p2), BF16),
                   jax.ShapeDtypeStruct((nblk, S5_T * LANES, S5_GB * p2), BF16),
                   jax.ShapeDtypeStruct((nblk, lrows, S5_GB * p2), F32)],
        compiler_params=_cparams(1),
        name="s5_prep",
    )(ar_rows, ai_rows, ldt_rows, b2_rows, c2_rows, d_row)


def _s5_main_kernel(u_ref, ktoep_ref, win_ref, wo_ref, lscan_ref, y_ref, *, nsteps):
    t_sub = S5_T
    nch = u_ref.shape[0] // t_sub
    p2 = win_ref.shape[1] // S5_GB
    half = p2 // 2
    xcat = jnp.concatenate([u_ref[pl.ds(s, nch, stride=t_sub), :].astype(BF16) for s in range(t_sub)],
                           axis=1)
    z = _dot(xcat, win_ref[...])
    pos = lax.broadcasted_iota(jnp.int32, (nch, p2), 0)
    lscan = lscan_ref[...]
    xprev = []
    for gg in range(S5_GB):
        cols = slice(gg * p2, (gg + 1) * p2)
        x = z[:, cols]
        for j in range(nsteps):
            d = 1 << j
            sh = jnp.where(pos >= d, pltpu.roll(x, d, axis=0), 0.0)
            x = (x + sh * lscan[j:j + 1, cols]
                 + pltpu.roll(sh, half, axis=1) * lscan[nsteps + j:nsteps + j + 1, cols])
        xprev.append(jnp.where(pos >= 1, pltpu.roll(x, 1, axis=0), 0.0).astype(BF16))
    y_state = _dot_nt(jnp.concatenate(xprev, axis=1), wo_ref[...])
    for t in range(t_sub):
        y_t = y_state[:, t * LANES:(t + 1) * LANES] + _dot(
            xcat[:, :(t + 1) * LANES], ktoep_ref[(t_sub - 1 - t) * LANES:, :])
        y_ref[pl.ds(t, nch, stride=t_sub), :] = y_t


def s5_main(proj, col0, wd, ktoep, win, wo, lscan, *, nsteps, batch, seq):
    n = proj.shape[0]
    nblk = wd // LANES
    assert col0 % LANES == 0
    cb = col0 // LANES

    def wspec(a):
        return pl.BlockSpec((None,) + a.shape[1:], lambda i, b: (i, 0, 0))

    return pl.pallas_call(
        functools.partial(_s5_main_kernel, nsteps=nsteps),
        grid=(nblk, batch),
        in_specs=[pl.BlockSpec((seq, LANES), lambda i, b: (b, cb + i)),
                  wspec(ktoep), wspec(win), wspec(wo), wspec(lscan)],
        out_specs=pl.BlockSpec((seq, LANES), lambda i, b: (b, i)),
        out_shape=jax.ShapeDtypeStruct((n, wd), F32),
        compiler_params=_cparams(2),
        name="s5_main",
    )(proj, ktoep, win, wo, lscan)


def _s5_glu_kernel(y_ref, w_ref, b_ref, o_ref, wbf_ref):
    @pl.when(pl.program_id(0) == 0)
    def _():
        wbf_ref[...] = w_ref[...].astype(BF16)

    y = y_ref[...]
    z = 0.5 * y * (1.0 + jnp.tanh(math.sqrt(2.0 / math.pi) * (y + 0.044715 * (y * y * y))))
    gate = _dot(z.astype(BF16), wbf_ref[...]) + b_ref[...]
    o_ref[...] = (z * _sigmoid(gate)).astype(o_ref.dtype)


def s5_glu(y, w_stack, layer, b, tm=512):
    n, wd = y.shape
    tm = min(tm, n)
    return pl.pallas_call(
        _s5_glu_kernel,
        grid=(n // tm,),
        in_specs=[pl.BlockSpec((tm, wd), lambda i: (i, 0)),
                  pl.BlockSpec((None, wd, wd), lambda i: (layer, 0, 0)),
                  pl.BlockSpec((1, wd), lambda i: (0, 0))],
        out_specs=pl.BlockSpec((tm, wd), lambda i: (i, 0)),
        out_shape=jax.ShapeDtypeStruct((n, wd), BF16),
        scratch_shapes=[pltpu.VMEM((wd, wd), BF16)],
        compiler_params=_cparams(1),
        name="s5_glu",
    )(y, w_stack, b)


def s5_mixer(proj, col0, a_re, a_im, log_dt, b_re, b_im, c_re, c_im, d_skip, glu_w_stack, layer, glu_b,
             *, batch, seq):
    wd = d_skip.shape[0]
    g, p = a_re.shape
    assert wd // g == S5_GROUP and seq % S5_T == 0 and wd % LANES == 0
    nsteps = _s5_scan_steps(seq // S5_T)
    per_row = lambda a: jnp.repeat(jnp.concatenate([a, a], axis=-1), S5_GROUP, axis=0)
    ldt_rows = jnp.broadcast_to(jnp.repeat(log_dt, S5_GROUP)[:, None], (wd, 2 * p))
    b2_rows = jnp.concatenate([b_re.transpose(0, 2, 1), b_im.transpose(0, 2, 1)], axis=-1).reshape(wd, 2 * p)
    c2_rows = jnp.concatenate([c_re, c_im], axis=-1).reshape(wd, 2 * p)
    ktoep, win, wo, lscan = s5_prep(per_row(a_re), per_row(a_im), ldt_rows, b2_rows, c2_rows,
                                    d_skip[None, :], nsteps)
    y = s5_main(proj, col0, wd, ktoep, win, wo, lscan, nsteps=nsteps, batch=batch, seq=seq)
    return s5_glu(y, glu_w_stack, layer, glu_b[None, :])


def _pad_heads(w, heads, width, new):
    r = w.shape[0]
    return jnp.pad(w.reshape(r, heads, width), ((0, 0), (0, 0), (0, new - width))).reshape(r, heads * new)


def kernel(x, c, w_ada, b_ada, ada_table, w_in, w_out, s5_a_re, s5_a_im, s5_log_dt, s5_b_re, s5_b_im, s5_c_re, s5_c_im, s5_d, s5_glu_w, s5_glu_b, hg_lb_raw, hg_norm_gain, hg_norm_bias, gla_w_gate, gla_b_gate, gla_norm_gain, w_ffn_gate, w_ffn_up, w_ffn_down, ln1_gain, ln1_bias, ln2_gain, ln2_bias):
    bsz, seq, d = x.shape
    depth = w_in.shape[0]
    n = bsz * seq
    s5_w = s5_d.shape[1]
    hg_w = hg_lb_raw.shape[1]
    gla_kw = gla_b_gate.shape[1]
    gla_vw = gla_norm_gain.shape[1]
    rank = gla_w_gate.shape[1]
    gla_dk = gla_kw // GLA_HEADS
    gla_dv = gla_vw // GLA_HEADS
    dk_pad = ((gla_dk + GLA_DK_PAD - 1) // GLA_DK_PAD) * GLA_DK_PAD
    assert dk_pad - gla_dk >= rank
    alpha = (2.0 * depth) ** 0.25

    rows = ((bsz + 7) // 8) * 8
    c_pad = jnp.pad(c, ((0, rows - bsz), (0, 0)))
    mod = cond_table(c_pad, w_ada, b_ada[None, :], ada_table.reshape(depth, N_MOD * d))
    mod = mod[:, :bsz].reshape(depth, bsz, N_MOD, d)

    o_u = 0
    o_hg = o_u + s5_w
    o_q = o_hg + 4 * hg_w
    o_k = o_q + gla_kw
    o_v = o_k + gla_kw
    o_lr = o_v + 2 * gla_vw

    hk = GLA_HEADS * dk_pad
    p_v = 0
    p_g = p_v + gla_vw
    p_hg = p_g + gla_vw
    p_u = p_hg + 4 * hg_w
    p_q = p_u + s5_w
    p_k = p_q + hk
    w_t = jnp.transpose(w_in, (0, 2, 1))
    lr_rows = w_t[:, o_lr:o_lr + rank]
    q_pad = jnp.zeros((depth, dk_pad - gla_dk - rank, d), F32)
    k_pad = jnp.zeros((depth, dk_pad - gla_dk, d), F32)
    q_rows, k_rows = [], []
    for hd in range(GLA_HEADS):
        q_rows += [w_t[:, o_q + hd * gla_dk:o_q + (hd + 1) * gla_dk], lr_rows, q_pad]
        k_rows += [w_t[:, o_k + hd * gla_dk:o_k + (hd + 1) * gla_dk], k_pad]
    w_all = jnp.concatenate([w_t[:, o_v:o_lr], w_t[:, o_hg:o_q], w_t[:, o_u:o_hg]] + q_rows + k_rows,
                            axis=1).astype(BF16)
    w_out_bf = w_out.astype(BF16)
    w_gate_bf = w_ffn_gate[0].astype(BF16)
    w_up_bf = w_ffn_up[0].astype(BF16)

    x2 = x.reshape(n, d)
    h = modulate(x2, mod[0], seq)
    for l in range(depth):
        proj = matmul_ws([h], w_all, l, F32, tm=1024, w_is_nk=True)
        y_a = s5_mixer(proj, p_u, s5_a_re[l], s5_a_im[l], s5_log_dt[l], s5_b_re[l],
                       s5_b_im[l], s5_c_re[l], s5_c_im[l], s5_d[l], s5_glu_w, l, s5_glu_b[l],
                       batch=bsz, seq=seq)
        y_b = hgrn2(proj, p_hg, hg_lb_raw, hg_norm_gain[l][None, :], hg_norm_bias[l][None, :],
                    layer=l, batch=bsz, seq=seq)
        w_gate = jnp.pad(
            jnp.pad(gla_w_gate[l].reshape(rank, GLA_HEADS, gla_dk), ((0, 0), (0, 0), (0, dk_pad - gla_dk))),
            ((gla_dk, dk_pad - gla_dk - rank), (0, 0), (0, 0))).reshape(dk_pad, hk).astype(BF16)
        b_gate = _pad_heads(gla_b_gate[l][None, :], GLA_HEADS, gla_dk, dk_pad)
        y_c = gla(proj, p_q, p_k, p_v, p_g, w_gate, b_gate, gla_norm_gain[l][None, :],
                  batch=bsz, seq=seq, dk=dk_pad, dv=gla_dv, q_scale=float(gla_dk) ** -0.5)
        mixed = matmul_ws([y_a, y_b, y_c], w_out_bf, l, BF16, tm=1024)
        x2, h = ln_mod(x2, mixed, mod[l], mod[l], ln1_gain[l][None, :], ln1_bias[l][None, :], seq,
                       alpha=alpha, gate_row=2, next_row=3, with_h=True)

        last = l == depth - 1
        jobs = [(w_ffn_down, l)] + ([] if last else [(w_ffn_gate, l + 1), (w_ffn_up, l + 1)])
        act, casts = ffn_up(h, w_gate_bf, w_up_bf, jobs)
        if not last:
            w_gate_bf, w_up_bf = casts[1], casts[2]
        ffn = matmul_ws([act], casts[0][None], 0, BF16, tn=512)
        x2, h = ln_mod(x2, ffn, mod[l], mod[l if last else l + 1],
                       ln2_gain[l][None, :], ln2_bias[l][None, :], seq,
                       alpha=alpha, gate_row=5, next_row=0, with_h=not last)
    return x2.reshape(bsz, seq, d)
```

```python
import functools
import math

import numpy as np
import jax
import jax.numpy as jnp
from jax import lax
from jax.experimental import pallas as pl
from jax.experimental.pallas import tpu as pltpu

F32 = jnp.float32
BF16 = jnp.bfloat16

LANES = 128
SUBLANES = 8
LOG2_E = 1.0 / math.log(2.0)
V7X_VMEM_BYTES = 64 * 1024 * 1024
VMEM_LIMIT = V7X_VMEM_BYTES - 8 * 1024 * 1024

S5_GROUP = 16
HG_HEAD = 128
GLA_HEADS = 4
GLA_GATE_TEMP = 16.0
N_MOD = 6
LN_EPS = 1e-5

S5_T = 16
GLA_CHUNK = 128
GLA_DK_PAD = 256
FFN_SUB_ROWS = 512


def _cparams(n_axes):
    return pltpu.CompilerParams(
        dimension_semantics=("arbitrary",) * n_axes, vmem_limit_bytes=VMEM_LIMIT)


def _sigmoid(x):
    return 1.0 / (1.0 + jnp.exp(-x))


def _dot(a, b):
    return jnp.dot(a, b, preferred_element_type=F32)


def _dot_nt(a, b, precision=None):
    return lax.dot_general(a, b, (((1,), (1,)), ((), ())),
                           preferred_element_type=F32, precision=precision)


def _dot_tn(a, b):
    return lax.dot_general(a, b, (((0,), (0,)), ((), ())), preferred_element_type=F32)


def _pick_tile(n, cap):
    best = None
    for t in range(LANES, min(n, cap) + 1, LANES):
        if n % t == 0:
            best = t
    assert best is not None, (n, cap)
    return best


def _mm_ws_kernel(*refs, n_a, w_is_nk):
    a_refs, w_ref, o_ref = refs[:n_a], refs[n_a], refs[n_a + 1]
    if w_is_nk:
        acc = _dot_nt(a_refs[0][...], w_ref[...])
    else:
        acc = None
        r0 = 0
        for a_ref in a_refs:
            k = a_ref.shape[1]
            part = _dot(a_ref[...], w_ref[r0:r0 + k, :])
            acc = part if acc is None else acc + part
            r0 += k
    o_ref[...] = acc.astype(o_ref.dtype)


def matmul_ws(a_list, w_stack, layer, out_dtype, tm=512, tn=1024, w_is_nk=False):
    m = a_list[0].shape[0]
    if w_is_nk:
        assert len(a_list) == 1
        _, n, k = w_stack.shape
    else:
        _, k, n = w_stack.shape
    assert sum(a.shape[1] for a in a_list) == k
    tn = _pick_tile(n, tn)
    tm = min(tm, m)
    if w_is_nk:
        w_spec = pl.BlockSpec((None, tn, k), lambda j, i: (layer, j, 0))
    else:
        w_spec = pl.BlockSpec((None, k, tn), lambda j, i: (layer, 0, j))
    return pl.pallas_call(
        functools.partial(_mm_ws_kernel, n_a=len(a_list), w_is_nk=w_is_nk),
        grid=(n // tn, m // tm),
        in_specs=[pl.BlockSpec((tm, a.shape[1]), lambda j, i: (i, 0)) for a in a_list] + [w_spec],
        out_specs=pl.BlockSpec((tm, tn), lambda j, i: (i, j)),
        out_shape=jax.ShapeDtypeStruct((m, n), out_dtype),
        compiler_params=_cparams(2),
        name="matmul_ws",
    )(*a_list, w_stack)


def _ffn_up_kernel(a_ref, wg_ref, wu_ref, *refs, n_side):
    side_in, o_ref, side_out = refs[:n_side], refs[n_side], refs[n_side + 1:]
    wg = wg_ref[...]
    wu = wu_ref[...]
    tm = a_ref.shape[0]
    sub = min(tm, FFN_SUB_ROWS)
    for r0 in range(0, tm, sub):
        a = a_ref[r0:r0 + sub, :]
        g = _dot(a, wg)
        u = _dot(a, wu)
        o_ref[r0:r0 + sub, :] = (g * _sigmoid(g) * u).astype(o_ref.dtype)
    for s_in, s_out in zip(side_in, side_out):
        s_out[...] = s_in[...].astype(s_out.dtype)


def _slab_specs(shape, layer, gi, gj):
    r, c = shape
    cw = -(-(-(-c // gj)) // LANES) * LANES
    if r % gi == 0 and (r // gi) % 16 == 0 and -(-c // cw) == gj:
        blk = (r // gi, cw)
        return (pl.BlockSpec((None,) + blk, lambda i, j: (layer, i, j)), pl.BlockSpec(blk, lambda i, j: (i, j)))
    steps = gi * gj
    for rs in range(16, r + 1, 16):
        if r % rs == 0 and r // rs <= steps:
            last = r // rs - 1
            blk = (rs, c)
            return (pl.BlockSpec((None,) + blk, lambda i, j: (layer, jnp.minimum(i * gj + j, last), 0)),
                    pl.BlockSpec(blk, lambda i, j: (jnp.minimum(i * gj + j, last), 0)))
    return None


def ffn_up(h, wg, wu, cast_jobs, tm=2048, tn=256):
    m, k = h.shape
    n = wg.shape[1]
    tm = min(tm, m)
    gi, gj = m // tm, pl.cdiv(n, tn)
    specs = [_slab_specs(w.shape[1:], layer, gi, gj) for w, layer in cast_jobs]
    riding = [job for job, sp in zip(cast_jobs, specs) if sp is not None]
    rspecs = [sp for sp in specs if sp is not None]
    wspec = pl.BlockSpec((k, tn), lambda i, j: (0, j))
    res = pl.pallas_call(
        functools.partial(_ffn_up_kernel, n_side=len(riding)),
        grid=(gi, gj),
        in_specs=[pl.BlockSpec((tm, k), lambda i, j: (i, 0)), wspec, wspec] + [sp[0] for sp in rspecs],
        out_specs=[pl.BlockSpec((tm, tn), lambda i, j: (i, j))] + [sp[1] for sp in rspecs],
        out_shape=[jax.ShapeDtypeStruct((m, n), BF16)]
        + [jax.ShapeDtypeStruct(w.shape[1:], BF16) for w, _ in riding],
        compiler_params=_cparams(2),
        name="ffn_up",
    )(h, wg, wu, *[w for w, _ in riding])
    casts, it = [], iter(res[1:])
    for (w, layer), sp in zip(cast_jobs, specs):
        casts.append(next(it) if sp is not None else w[layer].astype(BF16))
    return res[0], casts


def _cond_kernel(c_ref, w_ref, b_ref, tab_ref, o_ref):
    c = c_ref[...]
    act = (c * _sigmoid(c)).astype(BF16)
    cond = _dot(act, w_ref[...].astype(BF16)) + b_ref[...]
    for l in range(tab_ref.shape[0]):
        o_ref[l] = cond + tab_ref[l:l + 1, :]


def cond_table(c_pad, w_ada, b_ada, ada_table2, tn=1024):
    rows, d = c_pad.shape
    n = w_ada.shape[1]
    depth = ada_table2.shape[0]
    tn = _pick_tile(n, tn)
    return pl.pallas_call(
        _cond_kernel,
        grid=(n // tn,),
        in_specs=[pl.BlockSpec((rows, d), lambda j: (0, 0)),
                  pl.BlockSpec((d, tn), lambda j: (0, j)),
                  pl.BlockSpec((1, tn), lambda j: (0, j)),
                  pl.BlockSpec((depth, tn), lambda j: (0, j))],
        out_specs=pl.BlockSpec((depth, rows, tn), lambda j: (0, 0, j)),
        out_shape=jax.ShapeDtypeStruct((depth, rows, n), F32),
        compiler_params=_cparams(1),
        name="cond_table",
    )(c_pad, w_ada, b_ada, ada_table2)


def _modulate_kernel(x_ref, mod_ref, h_ref):
    m = mod_ref[0]
    h_ref[...] = (x_ref[...] * (1.0 + m[1:2, :]) + m[0:1, :]).astype(h_ref.dtype)


def modulate(x2, mod_l, seq, tm=512):
    n, d = x2.shape
    tm = min(tm, seq)
    per_b = seq // tm
    return pl.pallas_call(
        _modulate_kernel,
        grid=(n // tm,),
        in_specs=[pl.BlockSpec((tm, d), lambda i: (i, 0)),
                  pl.BlockSpec((1, N_MOD, d), lambda i: (i // per_b, 0, 0))],
        out_specs=pl.BlockSpec((tm, d), lambda i: (i, 0)),
        out_shape=jax.ShapeDtypeStruct((n, d), BF16),
        compiler_params=_cparams(1),
        name="modulate",
    )(x2, mod_l)


def _ln_mod_kernel(x_ref, mm_ref, mod_ref, nmod_ref, gain_ref, bias_ref, xo_ref, *h_refs,
                   alpha, gate_row, next_row):
    m = mod_ref[0]
    z = alpha * x_ref[...] + (1.0 + m[gate_row:gate_row + 1, :]) * mm_ref[...].astype(F32)
    mu = jnp.mean(z, axis=-1, keepdims=True)
    zc = z - mu
    var = jnp.mean(zc * zc, axis=-1, keepdims=True)
    y = zc * lax.rsqrt(var + LN_EPS) * gain_ref[...] + bias_ref[...]
    xo_ref[...] = y
    if h_refs:
        nm = nmod_ref[0]
        h_refs[0][...] = (y * (1.0 + nm[next_row + 1:next_row + 2, :])
                          + nm[next_row:next_row + 1, :]).astype(BF16)


def ln_mod(x2, mm, mod_l, mod_next, gain, bias, seq, *, alpha, gate_row, next_row, with_h, tm=256):
    n, d = x2.shape
    tm = min(tm, seq)
    per_b = seq // tm
    row = pl.BlockSpec((tm, d), lambda i: (i, 0))
    modspec = pl.BlockSpec((1, N_MOD, d), lambda i: (i // per_b, 0, 0))
    vec = pl.BlockSpec((1, d), lambda i: (0, 0))
    out_shape = [jax.ShapeDtypeStruct((n, d), F32)]
    out_specs = [row]
    if with_h:
        out_shape.append(jax.ShapeDtypeStruct((n, d), BF16))
        out_specs.append(row)
    res = pl.pallas_call(
        functools.partial(_ln_mod_kernel, alpha=alpha, gate_row=gate_row, next_row=next_row),
        grid=(n // tm,),
        in_specs=[row, row, modspec, modspec, vec, vec],
        out_specs=out_specs,
        out_shape=out_shape,
        compiler_params=_cparams(1),
        name="ln_mod",
    )(x2, mm, mod_l, mod_next, gain, bias)
    return (res[0], res[1]) if with_h else (res[0], None)


def _gla_levels(chunk):
    lv = []
    h = chunk // 2
    while h >= 1:
        lv.append(h)
        h //= 2
    return lv


def _gla_sum_matrix(chunk):
    blocks = []
    idx = np.arange(chunk)
    for h in _gla_levels(chunk):
        m = np.zeros((chunk, chunk), np.float32)
        for i in range(chunk):
            r = (i // (2 * h)) * 2 * h + h - 1
            if i % (2 * h) >= h:
                m[i, r + 1:i + 1] = 1.0
            else:
                m[i, i + 1:r + 1] = 1.0
        blocks.append(m)
    blocks.append((idx[None, :] <= idx[:, None]).astype(np.float32))
    m = np.concatenate(blocks, axis=0)
    return np.concatenate([m, m], axis=1)


def _gla_pair_masks(chunk):
    ngrp = chunk // SUBLANES
    col = lax.broadcasted_iota(jnp.int32, (SUBLANES, chunk), 1)
    rows = [lax.broadcasted_iota(jnp.int32, (SUBLANES, chunk), 0) + r * SUBLANES for r in range(ngrp)]
    diag = [row == col for row in rows]
    pair = []
    for h in _gla_levels(chunk):
        blk = 2 * h
        pair.append([((row // blk) == (col // blk)) & ((row % blk) >= h) & ((col % blk) < h) for row in rows])
    rid = lax.broadcasted_iota(jnp.int32, (chunk, 1), 0)
    right = [(rid % (2 * h)) >= h for h in _gla_levels(chunk)]
    return diag, pair, right


def _row_groups(x, groups):
    parts, start, prev = [], None, None
    for r in groups:
        if start is None:
            start = r
        elif r != prev + 1:
            parts.append(x[start * SUBLANES:(prev + 1) * SUBLANES])
            start = r
        prev = r
    parts.append(x[start * SUBLANES:(prev + 1) * SUBLANES])
    return parts


def _gla_chunk_heads(q, k, v, g, s_ref, msum, masks, chunk, dk, dv, heads):
    levels = _gla_levels(chunk)
    nl = len(levels)
    ngrp = chunk // SUBLANES
    g2 = g * LOG2_E
    g_hi = g2.astype(BF16)
    g_lo = (g2 - g_hi.astype(F32)).astype(BF16)
    expo = _dot(msum, jnp.concatenate([g_hi, g_lo], axis=0))
    e_lv = jnp.exp2(expo[:nl * chunk, :])
    b_cum = expo[nl * chunk:, :]
    e_cum_all = jnp.exp2(b_cum)
    e_rev_all = jnp.exp2(b_cum[chunk - 1:chunk, :] - b_cum)

    diag, pair, right = masks
    outs = []
    for hd in range(heads):
        ks = slice(hd * dk, (hd + 1) * dk)
        qh, kh = q[:, ks], k[:, ks]
        vh = v[:, hd * dv:(hd + 1) * dv]
        dsum = jnp.sum(qh * kh, axis=-1, keepdims=True)
        a_rows = [jnp.where(diag[r], dsum[r * SUBLANES:(r + 1) * SUBLANES], 0.0) for r in range(ngrp)]
        for li, h in enumerate(levels):
            e = e_lv[li * chunk:(li + 1) * chunk, ks]
            if h >= SUBLANES:
                rgt = [r for r in range(ngrp) if (r * SUBLANES) % (2 * h) >= h]
                runs, cur = [], None
                for r in range(ngrp):
                    src = r in rgt
                    if cur is None or cur[0] != src:
                        cur = [src, r, r]
                        runs.append(cur)
                    else:
                        cur[2] = r
                sel = jnp.concatenate([(qh if src else kh)[a * SUBLANES:(b + 1) * SUBLANES]
                                       for src, a, b in runs], axis=0)
                prod = sel * e
                lhs = jnp.concatenate(_row_groups(prod, rgt), axis=0).astype(BF16)
                p = _dot_nt(lhs, prod.astype(BF16))
                for n_, r in enumerate(rgt):
                    a_rows[r] = jnp.where(pair[li][r], p[n_ * SUBLANES:(n_ + 1) * SUBLANES], a_rows[r])
            else:
                xb = (jnp.where(right[li], qh, kh) * e).astype(BF16)
                p = _dot_nt(xb, xb)
                a_rows = [jnp.where(pair[li][r], p[r * SUBLANES:(r + 1) * SUBLANES], a_rows[r])
                          for r in range(ngrp)]
        a = jnp.concatenate(a_rows, axis=0)
        e_cum = e_cum_all[:, ks]
        s_t = s_ref[hd]
        vt = vh.T.astype(BF16)
        o = _dot_nt(jnp.concatenate([a.astype(BF16), (qh * e_cum).astype(BF16)], axis=1),
                    jnp.concatenate([vt, s_t.astype(BF16)], axis=1))
        kb = (kh * e_rev_all[:, ks]).astype(BF16)
        s_ref[hd] = s_t * e_cum[chunk - 1:chunk, :] + _dot(vt, kb)
        outs.append(o)
    return outs


def _hgrn_kernel(q_ref, f_ref, i_ref, g_ref, lbraw_ref, gain_ref, bias_ref, msum_ref,
                 o_ref, s_ref, *, layer, heads, chunk):
    dk = dv = HG_HEAD
    tb = q_ref.shape[0]

    @pl.when(pl.program_id(2) == 0)
    def _():
        s_ref[...] = jnp.zeros_like(s_ref)

    raw = lbraw_ref[...]
    ex = jnp.exp(raw - jnp.max(raw, axis=0, keepdims=True))
    sm = ex / jnp.sum(ex, axis=0, keepdims=True)
    lb = jnp.zeros_like(sm[0:1, :])
    for m in range(1, layer + 1):
        lb = lb + sm[m:m + 1, :]
    gain = gain_ref[...]
    bias = bias_ref[...]
    msum = msum_ref[...]
    masks = _gla_pair_masks(chunk)

    def body(c, carry):
        rows = pl.ds(c * chunk, chunk)
        q_in = q_ref[rows, :]
        f = lb + (1.0 - lb) * _sigmoid(f_ref[rows, :])
        q = q_in * _sigmoid(q_in)
        outs = _gla_chunk_heads(q, 1.0 - f, i_ref[rows, :], jnp.log(f), s_ref, msum, masks,
                                chunk, dk, dv, heads)
        gate = _sigmoid(g_ref[rows, :])
        for hd, o in enumerate(outs):
            cs = slice(hd * dv, (hd + 1) * dv)
            y = gate[:, cs] * o
            mu = jnp.mean(y, axis=-1, keepdims=True)
            yc = y - mu
            var = jnp.mean(yc * yc, axis=-1, keepdims=True)
            yn = yc * lax.rsqrt(var + LN_EPS)
            o_ref[rows, cs] = (yn * gain[:, cs] + bias[:, cs]).astype(o_ref.dtype)
        return carry

    for c in range(tb // chunk):
        body(c, 0)


def hgrn2(proj, col0, lb_raw, gain, bias, *, layer, batch, seq, heads_per_block=4, tb=512):
    n = proj.shape[0]
    w = lb_raw.shape[1]
    nheads = w // HG_HEAD
    hpb = heads_per_block
    while nheads % hpb:
        hpb -= 1
    bw = hpb * HG_HEAD
    ngrp = nheads // hpb
    assert col0 % bw == 0
    cb = col0 // bw
    tb = min(tb, seq)
    chunk = min(GLA_CHUNK, tb)
    nt = seq // tb
    msum = jnp.asarray(_gla_sum_matrix(chunk), BF16)

    def sec(s):
        return pl.BlockSpec((tb, bw), lambda b, hg, t: (b * nt + t, cb + s * ngrp + hg))

    vec = pl.BlockSpec((1, bw), lambda b, hg, t: (0, hg))
    return pl.pallas_call(
        functools.partial(_hgrn_kernel, layer=layer, heads=hpb, chunk=chunk),
        grid=(batch, ngrp, nt),
        in_specs=[sec(0), sec(1), sec(2), sec(3),
                  pl.BlockSpec((lb_raw.shape[0], bw), lambda b, hg, t: (0, hg)),
                  vec, vec,
                  pl.BlockSpec(msum.shape, lambda b, hg, t: (0, 0))],
        out_specs=pl.BlockSpec((tb, bw), lambda b, hg, t: (b * nt + t, hg)),
        out_shape=jax.ShapeDtypeStruct((n, w), BF16),
        scratch_shapes=[pltpu.VMEM((hpb, HG_HEAD, HG_HEAD), F32)],
        compiler_params=_cparams(3),
        name="hgrn2",
    )(proj, proj, proj, proj, lb_raw, gain, bias, msum)


def _gla_kernel(q_ref, k_ref, v_ref, g_ref, wg_ref, bg_ref, gain_ref, msum_ref,
                o_ref, s_ref, *, heads, chunk, dk, dv, q_scale):
    tb = q_ref.shape[0]

    @pl.when(pl.program_id(2) == 0)
    def _():
        s_ref[...] = jnp.zeros_like(s_ref)

    wg = wg_ref[...]
    bg = bg_ref[...]
    gain = gain_ref[...]
    msum = msum_ref[...]
    masks = _gla_pair_masks(chunk)

    def body(c, carry):
        rows = pl.ds(c * chunk, chunk)
        q_raw = q_ref[rows, :]
        q_bf = q_raw.astype(BF16)
        pre = jnp.concatenate([_dot(q_bf[:, hd * dk:(hd + 1) * dk], wg[:, hd * dk:(hd + 1) * dk])
                               for hd in range(heads)], axis=1) + bg
        log_a = (jnp.minimum(pre, 0.0) - jnp.log(1.0 + jnp.exp(-jnp.abs(pre)))) / GLA_GATE_TEMP
        outs = _gla_chunk_heads(q_raw * q_scale, k_ref[rows, :], v_ref[rows, :], log_a,
                                s_ref, msum, masks, chunk, dk, dv, heads)
        g_in = g_ref[rows, :]
        swish = g_in * _sigmoid(g_in)
        for hd, o in enumerate(outs):
            cs = slice(hd * dv, (hd + 1) * dv)
            y = o * lax.rsqrt(jnp.mean(o * o, axis=-1, keepdims=True) + LN_EPS)
            o_ref[rows, cs] = (y * gain[:, cs] * swish[:, cs]).astype(o_ref.dtype)
        return carry

    for c in range(tb // chunk):
        body(c, 0)


def gla(proj, q0, k0, v0, g0, w_gate, b_gate, gain, *, batch, seq, dk, dv, q_scale,
        heads_per_block=4, tb=512):
    n = proj.shape[0]
    nheads = w_gate.shape[1] // dk
    hpb = min(heads_per_block, nheads)
    ngrp = nheads // hpb
    tb = min(tb, seq)
    chunk = min(GLA_CHUNK, tb)
    nt = seq // tb
    msum = jnp.asarray(_gla_sum_matrix(chunk), BF16)
    kw, vw = hpb * dk, hpb * dv
    assert q0 % kw == 0 and k0 % kw == 0 and v0 % vw == 0 and g0 % vw == 0

    def rows(width, col0):
        cb = col0 // width
        return pl.BlockSpec((tb, width), lambda b, hg, t: (b * nt + t, cb + hg))

    return pl.pallas_call(
        functools.partial(_gla_kernel, heads=hpb, chunk=chunk, dk=dk, dv=dv, q_scale=q_scale),
        grid=(batch, ngrp, nt),
        in_specs=[rows(kw, q0), rows(kw, k0), rows(vw, v0), rows(vw, g0),
                  pl.BlockSpec((w_gate.shape[0], kw), lambda b, hg, t: (0, hg)),
                  pl.BlockSpec((1, kw), lambda b, hg, t: (0, hg)),
                  pl.BlockSpec((1, vw), lambda b, hg, t: (0, hg)),
                  pl.BlockSpec(msum.shape, lambda b, hg, t: (0, 0))],
        out_specs=pl.BlockSpec((tb, vw), lambda b, hg, t: (b * nt + t, hg)),
        out_shape=jax.ShapeDtypeStruct((n, nheads * dv), BF16),
        scratch_shapes=[pltpu.VMEM((hpb, dv, dk), F32)],
        compiler_params=_cparams(3),
        name="gla",
    )(proj, proj, proj, proj, w_gate, b_gate, gain, msum)


S5_GB = LANES // S5_GROUP


def _s5_scan_steps(nchunks):
    return max(1, int(math.ceil(math.log2(nchunks)))) if nchunks > 1 else 0


def _s5_prep_kernel(ar_ref, ai_ref, ldt_ref, b2_ref, c2_ref, d_ref,
                    ktoep_ref, win_ref, wo_ref, lscan_ref, *, nsteps):
    rows, p2 = ar_ref.shape
    half = p2 // 2
    t_sub = S5_T
    ar = ar_ref[...]
    ai = ai_ref[...]
    dt = jnp.exp(ldt_ref[...])
    lane = lax.broadcasted_iota(jnp.int32, (1, p2), 1)
    sgn_im = jnp.where(lane < half, -1.0, 1.0)
    sgn_re = -sgn_im

    def lam_pow(k):
        mag = jnp.exp(float(k) * (ar * dt))
        th = float(k) * (ai * dt)
        return mag * jnp.cos(th), mag * jnp.sin(th)

    pows = [lam_pow(k) for k in range(t_sub + 1)]

    def cmul(x, k):
        l_re, l_im = pows[k]
        return x * l_re + pltpu.roll(x, half, axis=1) * (l_im * sgn_im)

    lam_re, lam_im = pows[1]
    den = ar * ar + ai * ai
    nr = lam_re - 1.0
    ni = lam_im
    coef_re = (nr * ar + ni * ai) / den
    coef_im = (ni * ar - nr * ai) / den
    b2 = b2_ref[...]
    bbar = b2 * coef_re + pltpu.roll(b2, half, axis=1) * (coef_im * sgn_im)
    c2 = c2_ref[...]

    rgrp = lax.broadcasted_iota(jnp.int32, (rows, rows), 0) // S5_GROUP
    cgrp = lax.broadcasted_iota(jnp.int32, (rows, rows), 1) // S5_GROUP
    same_grp = rgrp == cgrp
    r_i = lax.broadcasted_iota(jnp.int32, (rows, rows), 0)
    c_i = lax.broadcasted_iota(jnp.int32, (rows, rows), 1)
    hp = lax.Precision.HIGHEST

    for j in range(t_sub):
        k = t_sub - 1 - j
        tap = jnp.where(same_grp, _dot_nt(cmul(bbar, k) * sgn_re, c2, hp), 0.0)
        if k == 0:
            tap = tap + jnp.where(r_i == c_i, d_ref[...], 0.0)
        ktoep_ref[j * rows:(j + 1) * rows, :] = tap.astype(ktoep_ref.dtype)

    grp_of_row = lax.broadcasted_iota(jnp.int32, (rows, p2), 0) // S5_GROUP

    def block_diag(tile):
        return jnp.concatenate([jnp.where(grp_of_row == gg, tile, 0.0) for gg in range(S5_GB)], axis=1)

    for s in range(t_sub):
        win_ref[s * rows:(s + 1) * rows, :] = block_diag(cmul(bbar, t_sub - 1 - s)).astype(win_ref.dtype)
        wo_ref[s * rows:(s + 1) * rows, :] = block_diag(cmul(c2, s + 1) * sgn_re).astype(wo_ref.dtype)

    def group_rows(tile):
        return jnp.concatenate([tile[gg * S5_GROUP:gg * S5_GROUP + 1, :] for gg in range(S5_GB)], axis=1)

    cur_re, cur_im = pows[t_sub]
    rows_re, rows_sw = [], []
    for _ in range(nsteps):
        rows_re.append(group_rows(cur_re))
        rows_sw.append(group_rows(cur_im * sgn_im))
        cur_re, cur_im = cur_re * cur_re - cur_im * cur_im, 2.0 * cur_re * cur_im
    pad = lscan_ref.shape[0] - 2 * nsteps
    parts = rows_re + rows_sw + ([jnp.zeros((pad, S5_GB * p2), F32)] if pad else [])
    lscan_ref[...] = jnp.concatenate(parts, axis=0)


def s5_prep(ar_rows, ai_rows, ldt_rows, b2_rows, c2_rows, d_row, nsteps):
    wd, p2 = ar_rows.shape
    nblk = wd // LANES
    lrows = ((2 * nsteps + 7) // 8) * 8
    tile = pl.BlockSpec((LANES, p2), lambda i: (i, 0))
    return pl.pallas_call(
        functools.partial(_s5_prep_kernel, nsteps=nsteps),
        grid=(nblk,),
        in_specs=[tile, tile, tile, tile, tile, pl.BlockSpec((1, LANES), lambda i: (0, i))],
        out_specs=[pl.BlockSpec((None, S5_T * LANES, LANES), lambda i: (i, 0, 0)),
                   pl.BlockSpec((None, S5_T * LANES, S5_GB * p2), lambda i: (i, 0, 0)),
                   pl.BlockSpec((None, S5_T * LANES, S5_GB * p2), lambda i: (i, 0, 0)),
                   pl.BlockSpec((None, lrows, S5_GB * p2), lambda i: (i, 0, 0))],
        out_shape=[jax.ShapeDtypeStruct((nblk, S5_T * LANES, LANES), BF16),
                   jax.ShapeDtypeStruct((nblk, S5_T * LANES, S5_GB * p2), BF16),
                   jax.ShapeDtypeStruct((nblk, S5_T * LANES, S5_GB * p2), BF16),
                   jax.ShapeDtypeStruct((nblk, lrows, S5_GB * p2), F32)],
        compiler_params=_cparams(1),
        name="s5_prep",
    )(ar_rows, ai_rows, ldt_rows, b2_rows, c2_rows, d_row)


def _s5_main_kernel(u_ref, ktoep_ref, win_ref, wo_ref, lscan_ref, y_ref, *, nsteps):
    t_sub = S5_T
    nch = u_ref.shape[0] // t_sub
    p2 = win_ref.shape[1] // S5_GB
    half = p2 // 2
    xcat = jnp.concatenate([u_ref[pl.ds(s, nch, stride=t_sub), :].astype(BF16) for s in range(t_sub)],
                           axis=1)
    z = _dot(xcat, win_ref[...])
    pos = lax.broadcasted_iota(jnp.int32, (nch, p2), 0)
    lscan = lscan_ref[...]
    xprev = []
    for gg in range(S5_GB):
        cols = slice(gg * p2, (gg + 1) * p2)
        x = z[:, cols]
        for j in range(nsteps):
            d = 1 << j
            sh = jnp.where(pos >= d, pltpu.roll(x, d, axis=0), 0.0)
            x = (x + sh * lscan[j:j + 1, cols]
                 + pltpu.roll(sh, half, axis=1) * lscan[nsteps + j:nsteps + j + 1, cols])
        xprev.append(jnp.where(pos >= 1, pltpu.roll(x, 1, axis=0), 0.0).astype(BF16))
    y_state = _dot_nt(jnp.concatenate(xprev, axis=1), wo_ref[...])
    zero_blk = jnp.zeros((LANES, LANES), ktoep_ref.dtype)
    for t in range(0, t_sub, 2):
        taps = jnp.concatenate(
            [jnp.concatenate([ktoep_ref[(t_sub - 1 - t) * LANES:, :], zero_blk], axis=0),
             ktoep_ref[(t_sub - 2 - t) * LANES:, :]], axis=1)
        y_pair = y_state[:, t * LANES:(t + 2) * LANES] + _dot(xcat[:, :(t + 2) * LANES], taps)
        y_ref[pl.ds(t, nch, stride=t_sub), :] = y_pair[:, :LANES]
        y_ref[pl.ds(t + 1, nch, stride=t_sub), :] = y_pair[:, LANES:]


def s5_main(proj, col0, wd, ktoep, win, wo, lscan, *, nsteps, batch, seq):
    n = proj.shape[0]
    nblk = wd // LANES
    assert col0 % LANES == 0
    cb = col0 // LANES

    def wspec(a):
        return pl.BlockSpec((None,) + a.shape[1:], lambda i, b: (i, 0, 0))

    return pl.pallas_call(
        functools.partial(_s5_main_kernel, nsteps=nsteps),
        grid=(nblk, batch),
        in_specs=[pl.BlockSpec((seq, LANES), lambda i, b: (b, cb + i)),
                  wspec(ktoep), wspec(win), wspec(wo), wspec(lscan)],
        out_specs=pl.BlockSpec((seq, LANES), lambda i, b: (b, i)),
        out_shape=jax.ShapeDtypeStruct((n, wd), F32),
        compiler_params=_cparams(2),
        name="s5_main",
    )(proj, ktoep, win, wo, lscan)


def _s5_glu_kernel(y_ref, w_ref, b_ref, o_ref, wbf_ref):
    @pl.when(pl.program_id(0) == 0)
    def _():
        wbf_ref[...] = w_ref[...].astype(BF16)

    y = y_ref[...]
    z = 0.5 * y * (1.0 + jnp.tanh(math.sqrt(2.0 / math.pi) * (y + 0.044715 * (y * y * y))))
    gate = _dot(z.astype(BF16), wbf_ref[...]) + b_ref[...]
    o_ref[...] = (z * _sigmoid(gate)).astype(o_ref.dtype)


def s5_glu(y, w_stack, layer, b, tm=512):
    n, wd = y.shape
    tm = min(tm, n)
    return pl.pallas_call(
        _s5_glu_kernel,
        grid=(n // tm,),
        in_specs=[pl.BlockSpec((tm, wd), lambda i: (i, 0)),
                  pl.BlockSpec((None, wd, wd), lambda i: (layer, 0, 0)),
                  pl.BlockSpec((1, wd), lambda i: (0, 0))],
        out_specs=pl.BlockSpec((tm, wd), lambda i: (i, 0)),
        out_shape=jax.ShapeDtypeStruct((n, wd), BF16),
        scratch_shapes=[pltpu.VMEM((wd, wd), BF16)],
        compiler_params=_cparams(1),
        name="s5_glu",
    )(y, w_stack, b)


def s5_mixer(proj, col0, a_re, a_im, log_dt, b_re, b_im, c_re, c_im, d_skip, glu_w_stack, layer, glu_b,
             *, batch, seq):
    wd = d_skip.shape[0]
    g, p = a_re.shape
    assert wd // g == S5_GROUP and seq % S5_T == 0 and wd % LANES == 0
    nsteps = _s5_scan_steps(seq // S5_T)
    per_row = lambda a: jnp.repeat(jnp.concatenate([a, a], axis=-1), S5_GROUP, axis=0)
    ldt_rows = jnp.broadcast_to(jnp.repeat(log_dt, S5_GROUP)[:, None], (wd, 2 * p))
    b2_rows = jnp.concatenate([b_re.transpose(0, 2, 1), b_im.transpose(0, 2, 1)], axis=-1).reshape(wd, 2 * p)
    c2_rows = jnp.concatenate([c_re, c_im], axis=-1).reshape(wd, 2 * p)
    ktoep, win, wo, lscan = s5_prep(per_row(a_re), per_row(a_im), ldt_rows, b2_rows, c2_rows,
                                    d_skip[None, :], nsteps)
    y = s5_main(proj, col0, wd, ktoep, win, wo, lscan, nsteps=nsteps, batch=batch, seq=seq)
    return s5_glu(y, glu_w_stack, layer, glu_b[None, :])


def _pad_heads(w, heads, width, new):
    r = w.shape[0]
    return jnp.pad(w.reshape(r, heads, width), ((0, 0), (0, 0), (0, new - width))).reshape(r, heads * new)


def kernel(x, c, w_ada, b_ada, ada_table, w_in, w_out, s5_a_re, s5_a_im, s5_log_dt, s5_b_re, s5_b_im, s5_c_re, s5_c_im, s5_d, s5_glu_w, s5_glu_b, hg_lb_raw, hg_norm_gain, hg_norm_bias, gla_w_gate, gla_b_gate, gla_norm_gain, w_ffn_gate, w_ffn_up, w_ffn_down, ln1_gain, ln1_bias, ln2_gain, ln2_bias):
    bsz, seq, d = x.shape
    depth = w_in.shape[0]
    n = bsz * seq
    s5_w = s5_d.shape[1]
    hg_w = hg_lb_raw.shape[1]
    gla_kw = gla_b_gate.shape[1]
    gla_vw = gla_norm_gain.shape[1]
    rank = gla_w_gate.shape[1]
    gla_dk = gla_kw // GLA_HEADS
    gla_dv = gla_vw // GLA_HEADS
    dk_pad = ((gla_dk + GLA_DK_PAD - 1) // GLA_DK_PAD) * GLA_DK_PAD
    assert dk_pad - gla_dk >= rank
    alpha = (2.0 * depth) ** 0.25

    rows = ((bsz + 7) // 8) * 8
    c_pad = jnp.pad(c, ((0, rows - bsz), (0, 0)))
    mod = cond_table(c_pad, w_ada, b_ada[None, :], ada_table.reshape(depth, N_MOD * d))
    mod = mod[:, :bsz].reshape(depth, bsz, N_MOD, d)

    o_u = 0
    o_hg = o_u + s5_w
    o_q = o_hg + 4 * hg_w
    o_k = o_q + gla_kw
    o_v = o_k + gla_kw
    o_lr = o_v + 2 * gla_vw

    hk = GLA_HEADS * dk_pad
    p_v = 0
    p_g = p_v + gla_vw
    p_hg = p_g + gla_vw
    p_u = p_hg + 4 * hg_w
    p_q = p_u + s5_w
    p_k = p_q + hk
    w_t = jnp.transpose(w_in, (0, 2, 1))
    lr_rows = jnp.broadcast_to(w_t[:, None, o_lr:o_lr + rank], (depth, GLA_HEADS, rank, d))
    q_rows = jnp.concatenate([w_t[:, o_q:o_k].reshape(depth, GLA_HEADS, gla_dk, d), lr_rows,
                              jnp.zeros((depth, GLA_HEADS, dk_pad - gla_dk - rank, d), F32)], axis=2)
    k_rows = jnp.pad(w_t[:, o_k:o_v].reshape(depth, GLA_HEADS, gla_dk, d),
                     ((0, 0), (0, 0), (0, dk_pad - gla_dk), (0, 0)))
    w_all = jnp.concatenate([w_t[:, o_v:o_lr], w_t[:, o_hg:o_q], w_t[:, o_u:o_hg],
                             q_rows.reshape(depth, hk, d), k_rows.reshape(depth, hk, d)], axis=1).astype(BF16)
    w_out_bf = w_out.astype(BF16)
    w_gate_bf = w_ffn_gate[0].astype(BF16)
    w_up_bf = w_ffn_up[0].astype(BF16)

    x2 = x.reshape(n, d)
    h = modulate(x2, mod[0], seq)
    for l in range(depth):
        proj = matmul_ws([h], w_all, l, F32, tm=1024, w_is_nk=True)
        y_a = s5_mixer(proj, p_u, s5_a_re[l], s5_a_im[l], s5_log_dt[l], s5_b_re[l],
                       s5_b_im[l], s5_c_re[l], s5_c_im[l], s5_d[l], s5_glu_w, l, s5_glu_b[l],
                       batch=bsz, seq=seq)
        y_b = hgrn2(proj, p_hg, hg_lb_raw, hg_norm_gain[l][None, :], hg_norm_bias[l][None, :],
                    layer=l, batch=bsz, seq=seq)
        w_gate = jnp.pad(
            jnp.pad(gla_w_gate[l].reshape(rank, GLA_HEADS, gla_dk), ((0, 0), (0, 0), (0, dk_pad - gla_dk))),
            ((gla_dk, dk_pad - gla_dk - rank), (0, 0), (0, 0))).reshape(dk_pad, hk).astype(BF16)
        b_gate = _pad_heads(gla_b_gate[l][None, :], GLA_HEADS, gla_dk, dk_pad)
        y_c = gla(proj, p_q, p_k, p_v, p_g, w_gate, b_gate, gla_norm_gain[l][None, :],
                  batch=bsz, seq=seq, dk=dk_pad, dv=gla_dv, q_scale=float(gla_dk) ** -0.5)
        mixed = matmul_ws([y_a, y_b, y_c], w_out_bf, l, BF16, tm=1024)
        x2, h = ln_mod(x2, mixed, mod[l], mod[l], ln1_gain[l][None, :], ln1_bias[l][None, :], seq,
                       alpha=alpha, gate_row=2, next_row=3, with_h=True)

        last = l == depth - 1
        jobs = [(w_ffn_down, l)] + ([] if last else [(w_ffn_gate, l + 1), (w_ffn_up, l + 1)])
        act, casts = ffn_up(h, w_gate_bf, w_up_bf, jobs)
        if not last:
            w_gate_bf, w_up_bf = casts[1], casts[2]
        ffn = matmul_ws([act], casts[0][None], 0, BF16, tn=512)
        x2, h = ln_mod(x2, ffn, mod[l], mod[l if last else l + 1],
                       ln2_gain[l][None, :], ln2_bias[l][None, :], seq,
                       alpha=alpha, gate_row=5, next_row=0, with_h=not last)
    return x2.reshape(bsz, seq, d)
```

```python
import functools
import math

import numpy as np
import jax
import jax.numpy as jnp
from jax import lax
from jax.experimental import pallas as pl
from jax.experimental.pallas import tpu as pltpu

F32 = jnp.float32
BF16 = jnp.bfloat16

LANES = 128
SUBLANES = 8
LOG2_E = 1.0 / math.log(2.0)
V7X_VMEM_BYTES = 64 * 1024 * 1024
VMEM_COMPILER_RESERVE = 8 * 1024 * 1024
VMEM_LIMIT = V7X_VMEM_BYTES - VMEM_COMPILER_RESERVE

S5_GROUP = 16
HG_HEAD = 128
GLA_HEADS = 4
GLA_GATE_TEMP = 16.0
N_MOD = 6
LN_EPS = 1e-5

S5_T = 16
GLA_CHUNK = 128
GLA_DK_PAD = 256
FFN_SUB_ROWS = 512


def _cparams(n_axes):
    return pltpu.CompilerParams(
        dimension_semantics=("arbitrary",) * n_axes, vmem_limit_bytes=VMEM_LIMIT)


def _sigmoid(x):
    return 1.0 / (1.0 + jnp.exp(-x))


def _dot(a, b):
    return jnp.dot(a, b, preferred_element_type=F32)


def _dot_nt(a, b, precision=None):
    return lax.dot_general(a, b, (((1,), (1,)), ((), ())),
                           preferred_element_type=F32, precision=precision)


def _pick_tile(n, cap):
    best = None
    for t in range(LANES, min(n, cap) + 1, LANES):
        if n % t == 0:
            best = t
    assert best is not None, (n, cap)
    return best


def _mm_ws_kernel(*refs, n_a, w_is_nk):
    a_refs, w_ref, o_ref = refs[:n_a], refs[n_a], refs[n_a + 1]
    if w_is_nk:
        acc = _dot_nt(a_refs[0][...], w_ref[...])
    else:
        acc = None
        r0 = 0
        for a_ref in a_refs:
            k = a_ref.shape[1]
            part = _dot(a_ref[...], w_ref[r0:r0 + k, :])
            acc = part if acc is None else acc + part
            r0 += k
    o_ref[...] = acc.astype(o_ref.dtype)


def matmul_ws(a_list, w_stack, layer, out_dtype, tm=512, tn=1024, w_is_nk=False):
    m = a_list[0].shape[0]
    if w_is_nk:
        assert len(a_list) == 1
        _, n, k = w_stack.shape
    else:
        _, k, n = w_stack.shape
    assert sum(a.shape[1] for a in a_list) == k
    tn = _pick_tile(n, tn)
    tm = min(tm, m)
    if w_is_nk:
        w_spec = pl.BlockSpec((None, tn, k), lambda j, i: (layer, j, 0))
    else:
        w_spec = pl.BlockSpec((None, k, tn), lambda j, i: (layer, 0, j))
    return pl.pallas_call(
        functools.partial(_mm_ws_kernel, n_a=len(a_list), w_is_nk=w_is_nk),
        grid=(n // tn, m // tm),
        in_specs=[pl.BlockSpec((tm, a.shape[1]), lambda j, i: (i, 0)) for a in a_list] + [w_spec],
        out_specs=pl.BlockSpec((tm, tn), lambda j, i: (i, j)),
        out_shape=jax.ShapeDtypeStruct((m, n), out_dtype),
        compiler_params=_cparams(2),
        name="matmul_ws",
    )(*a_list, w_stack)


def _ffn_up_kernel(a_ref, wg_ref, wu_ref, *refs, n_side):
    side_in, o_ref, side_out = refs[:n_side], refs[n_side], refs[n_side + 1:]
    wg = wg_ref[...]
    wu = wu_ref[...]
    tm = a_ref.shape[0]
    sub = min(tm, FFN_SUB_ROWS)
    for r0 in range(0, tm, sub):
        a = a_ref[r0:r0 + sub, :]
        g = _dot(a, wg)
        u = _dot(a, wu)
        o_ref[r0:r0 + sub, :] = (g * _sigmoid(g) * u).astype(o_ref.dtype)
    for s_in, s_out in zip(side_in, side_out):
        s_out[...] = s_in[...].astype(s_out.dtype)


def _slab_specs(shape, layer, gi, gj):
    r, c = shape
    cw = -(-(-(-c // gj)) // LANES) * LANES
    if r % gi == 0 and (r // gi) % 16 == 0 and -(-c // cw) == gj:
        blk = (r // gi, cw)
        return (pl.BlockSpec((None,) + blk, lambda i, j: (layer, i, j)), pl.BlockSpec(blk, lambda i, j: (i, j)))
    steps = gi * gj
    for rs in range(16, r + 1, 16):
        if r % rs == 0 and r // rs <= steps:
            last = r // rs - 1
            blk = (rs, c)
            return (pl.BlockSpec((None,) + blk, lambda i, j: (layer, jnp.minimum(i * gj + j, last), 0)),
                    pl.BlockSpec(blk, lambda i, j: (jnp.minimum(i * gj + j, last), 0)))
    return None


def ffn_up(h, wg, wu, cast_jobs, tm=2048, tn=256):
    m, k = h.shape
    n = wg.shape[1]
    tm = min(tm, m)
    gi, gj = m // tm, pl.cdiv(n, tn)
    specs = [_slab_specs(w.shape[1:], layer, gi, gj) for w, layer in cast_jobs]
    riding = [job for job, sp in zip(cast_jobs, specs) if sp is not None]
    rspecs = [sp for sp in specs if sp is not None]
    wspec = pl.BlockSpec((k, tn), lambda i, j: (0, j))
    res = pl.pallas_call(
        functools.partial(_ffn_up_kernel, n_side=len(riding)),
        grid=(gi, gj),
        in_specs=[pl.BlockSpec((tm, k), lambda i, j: (i, 0)), wspec, wspec] + [sp[0] for sp in rspecs],
        out_specs=[pl.BlockSpec((tm, tn), lambda i, j: (i, j))] + [sp[1] for sp in rspecs],
        out_shape=[jax.ShapeDtypeStruct((m, n), BF16)]
        + [jax.ShapeDtypeStruct(w.shape[1:], BF16) for w, _ in riding],
        compiler_params=_cparams(2),
        name="ffn_up",
    )(h, wg, wu, *[w for w, _ in riding])
    casts, it = [], iter(res[1:])
    for (w, layer), sp in zip(cast_jobs, specs):
        casts.append(next(it) if sp is not None else w[layer].astype(BF16))
    return res[0], casts


def _cond_kernel(c_ref, w_ref, b_ref, tab_ref, o_ref):
    c = c_ref[...]
    act = (c * _sigmoid(c)).astype(BF16)
    cond = _dot(act, w_ref[...].astype(BF16)) + b_ref[...]
    for l in range(tab_ref.shape[0]):
        o_ref[l] = cond + tab_ref[l:l + 1, :]


def cond_table(c_pad, w_ada, b_ada, ada_table2, tn=1024):
    rows, d = c_pad.shape
    n = w_ada.shape[1]
    depth = ada_table2.shape[0]
    tn = _pick_tile(n, tn)
    return pl.pallas_call(
        _cond_kernel,
        grid=(n // tn,),
        in_specs=[pl.BlockSpec((rows, d), lambda j: (0, 0)),
                  pl.BlockSpec((d, tn), lambda j: (0, j)),
                  pl.BlockSpec((1, tn), lambda j: (0, j)),
                  pl.BlockSpec((depth, tn), lambda j: (0, j))],
        out_specs=pl.BlockSpec((depth, rows, tn), lambda j: (0, 0, j)),
        out_shape=jax.ShapeDtypeStruct((depth, rows, n), F32),
        compiler_params=_cparams(1),
        name="cond_table",
    )(c_pad, w_ada, b_ada, ada_table2)


def _modulate_kernel(x_ref, mod_ref, h_ref):
    m = mod_ref[0]
    h_ref[...] = (x_ref[...] * (1.0 + m[1:2, :]) + m[0:1, :]).astype(h_ref.dtype)


def modulate(x2, mod_l, seq, tm=512):
    n, d = x2.shape
    tm = min(tm, seq)
    per_b = seq // tm
    return pl.pallas_call(
        _modulate_kernel,
        grid=(n // tm,),
        in_specs=[pl.BlockSpec((tm, d), lambda i: (i, 0)),
                  pl.BlockSpec((1, N_MOD, d), lambda i: (i // per_b, 0, 0))],
        out_specs=pl.BlockSpec((tm, d), lambda i: (i, 0)),
        out_shape=jax.ShapeDtypeStruct((n, d), BF16),
        compiler_params=_cparams(1),
        name="modulate",
    )(x2, mod_l)


def _ln_mod_kernel(x_ref, mm_ref, mod_ref, nmod_ref, gain_ref, bias_ref, xo_ref, *h_refs,
                   alpha, gate_row, next_row):
    m = mod_ref[0]
    z = alpha * x_ref[...] + (1.0 + m[gate_row:gate_row + 1, :]) * mm_ref[...].astype(F32)
    mu = jnp.mean(z, axis=-1, keepdims=True)
    zc = z - mu
    var = jnp.mean(zc * zc, axis=-1, keepdims=True)
    y = zc * lax.rsqrt(var + LN_EPS) * gain_ref[...] + bias_ref[...]
    xo_ref[...] = y
    if h_refs:
        nm = nmod_ref[0]
        h_refs[0][...] = (y * (1.0 + nm[next_row + 1:next_row + 2, :])
                          + nm[next_row:next_row + 1, :]).astype(BF16)


def ln_mod(x2, mm, mod_l, mod_next, gain, bias, seq, *, alpha, gate_row, next_row, with_h, tm=256):
    n, d = x2.shape
    tm = min(tm, seq)
    per_b = seq // tm
    row = pl.BlockSpec((tm, d), lambda i: (i, 0))
    modspec = pl.BlockSpec((1, N_MOD, d), lambda i: (i // per_b, 0, 0))
    vec = pl.BlockSpec((1, d), lambda i: (0, 0))
    out_shape = [jax.ShapeDtypeStruct((n, d), F32)]
    out_specs = [row]
    if with_h:
        out_shape.append(jax.ShapeDtypeStruct((n, d), BF16))
        out_specs.append(row)
    res = pl.pallas_call(
        functools.partial(_ln_mod_kernel, alpha=alpha, gate_row=gate_row, next_row=next_row),
        grid=(n // tm,),
        in_specs=[row, row, modspec, modspec, vec, vec],
        out_specs=out_specs,
        out_shape=out_shape,
        compiler_params=_cparams(1),
        name="ln_mod",
    )(x2, mm, mod_l, mod_next, gain, bias)
    return (res[0], res[1]) if with_h else (res[0], None)


def _gla_levels(chunk):
    lv = []
    h = chunk // 2
    while h >= 1:
        lv.append(h)
        h //= 2
    return lv


def _gla_sum_matrix(chunk):
    blocks = []
    idx = np.arange(chunk)
    for h in _gla_levels(chunk):
        m = np.zeros((chunk, chunk), np.float32)
        for i in range(chunk):
            r = (i // (2 * h)) * 2 * h + h - 1
            if i % (2 * h) >= h:
                m[i, r + 1:i + 1] = 1.0
            else:
                m[i, i + 1:r + 1] = 1.0
        blocks.append(m)
    blocks.append((idx[None, :] <= idx[:, None]).astype(np.float32))
    m = np.concatenate(blocks, axis=0)
    return np.concatenate([m, m], axis=1)


def _gla_pair_masks(chunk):
    ngrp = chunk // SUBLANES
    col = lax.broadcasted_iota(jnp.int32, (SUBLANES, chunk), 1)
    rows = [lax.broadcasted_iota(jnp.int32, (SUBLANES, chunk), 0) + r * SUBLANES for r in range(ngrp)]
    diag = [row == col for row in rows]
    pair = []
    for h in _gla_levels(chunk):
        blk = 2 * h
        pair.append([((row // blk) == (col // blk)) & ((row % blk) >= h) & ((col % blk) < h) for row in rows])
    rid = lax.broadcasted_iota(jnp.int32, (chunk, 1), 0)
    right = [(rid % (2 * h)) >= h for h in _gla_levels(chunk)]
    return diag, pair, right


def _row_groups(x, groups):
    parts, start, prev = [], None, None
    for r in groups:
        if start is None:
            start = r
        elif r != prev + 1:
            parts.append(x[start * SUBLANES:(prev + 1) * SUBLANES])
            start = r
        prev = r
    parts.append(x[start * SUBLANES:(prev + 1) * SUBLANES])
    return parts


def _gla_chunk_heads(q, k, v, g, s_ref, msum, masks, chunk, dk, dv, heads):
    levels = _gla_levels(chunk)
    nl = len(levels)
    ngrp = chunk // SUBLANES
    g2 = g * LOG2_E
    g_hi = g2.astype(BF16)
    g_lo = (g2 - g_hi.astype(F32)).astype(BF16)
    expo = _dot(msum, jnp.concatenate([g_hi, g_lo], axis=0))
    e_lv = jnp.exp2(expo[:nl * chunk, :])
    b_cum = expo[nl * chunk:, :]
    e_cum_all = jnp.exp2(b_cum)
    e_rev_all = jnp.exp2(b_cum[chunk - 1:chunk, :] - b_cum)

    diag, pair, right = masks
    outs = []
    for hd in range(heads):
        ks = slice(hd * dk, (hd + 1) * dk)
        qh, kh = q[:, ks], k[:, ks]
        vh = v[:, hd * dv:(hd + 1) * dv]
        dsum = jnp.sum(qh * kh, axis=-1, keepdims=True)
        a_rows = [jnp.where(diag[r], dsum[r * SUBLANES:(r + 1) * SUBLANES], 0.0) for r in range(ngrp)]
        for li, h in enumerate(levels):
            e = e_lv[li * chunk:(li + 1) * chunk, ks]
            if h >= SUBLANES:
                rgt = [r for r in range(ngrp) if (r * SUBLANES) % (2 * h) >= h]
                runs, cur = [], None
                for r in range(ngrp):
                    src = r in rgt
                    if cur is None or cur[0] != src:
                        cur = [src, r, r]
                        runs.append(cur)
                    else:
                        cur[2] = r
                sel = jnp.concatenate([(qh if src else kh)[a * SUBLANES:(b + 1) * SUBLANES]
                                       for src, a, b in runs], axis=0)
                prod = sel * e
                lhs = jnp.concatenate(_row_groups(prod, rgt), axis=0).astype(BF16)
                p = _dot_nt(lhs, prod.astype(BF16))
                for n_, r in enumerate(rgt):
                    a_rows[r] = jnp.where(pair[li][r], p[n_ * SUBLANES:(n_ + 1) * SUBLANES], a_rows[r])
            else:
                xb = (jnp.where(right[li], qh, kh) * e).astype(BF16)
                p = _dot_nt(xb, xb)
                a_rows = [jnp.where(pair[li][r], p[r * SUBLANES:(r + 1) * SUBLANES], a_rows[r])
                          for r in range(ngrp)]
        a = jnp.concatenate(a_rows, axis=0)
        e_cum = e_cum_all[:, ks]
        s_t = s_ref[hd]
        vt = vh.T.astype(BF16)
        o = _dot_nt(jnp.concatenate([a.astype(BF16), (qh * e_cum).astype(BF16)], axis=1),
                    jnp.concatenate([vt, s_t.astype(BF16)], axis=1))
        kb = (kh * e_rev_all[:, ks]).astype(BF16)
        s_ref[hd] = s_t * e_cum[chunk - 1:chunk, :] + _dot(vt, kb)
        outs.append(o)
    return outs


def _hgrn_kernel(q_ref, f_ref, i_ref, g_ref, lbraw_ref, gain_ref, bias_ref, msum_ref,
                 o_ref, s_ref, *, layer, heads, chunk):
    dk = dv = HG_HEAD
    tb = q_ref.shape[0]

    @pl.when(pl.program_id(2) == 0)
    def _():
        s_ref[...] = jnp.zeros_like(s_ref)

    raw = lbraw_ref[...]
    ex = jnp.exp(raw - jnp.max(raw, axis=0, keepdims=True))
    sm = ex / jnp.sum(ex, axis=0, keepdims=True)
    lb = jnp.zeros_like(sm[0:1, :])
    for m in range(1, layer + 1):
        lb = lb + sm[m:m + 1, :]
    gain = gain_ref[...]
    bias = bias_ref[...]
    msum = msum_ref[...]
    masks = _gla_pair_masks(chunk)

    def body(c, carry):
        rows = pl.ds(c * chunk, chunk)
        q_in = q_ref[rows, :]
        f = lb + (1.0 - lb) * _sigmoid(f_ref[rows, :])
        q = q_in * _sigmoid(q_in)
        outs = _gla_chunk_heads(q, 1.0 - f, i_ref[rows, :], jnp.log(f), s_ref, msum, masks,
                                chunk, dk, dv, heads)
        gate = _sigmoid(g_ref[rows, :])
        for hd, o in enumerate(outs):
            cs = slice(hd * dv, (hd + 1) * dv)
            y = gate[:, cs] * o
            mu = jnp.mean(y, axis=-1, keepdims=True)
            yc = y - mu
            var = jnp.mean(yc * yc, axis=-1, keepdims=True)
            yn = yc * lax.rsqrt(var + LN_EPS)
            o_ref[rows, cs] = (yn * gain[:, cs] + bias[:, cs]).astype(o_ref.dtype)
        return carry

    for c in range(tb // chunk):
        body(c, 0)


def hgrn2(proj, col0, lb_raw, gain, bias, *, layer, batch, seq, heads_per_block=4, tb=1024):
    n = proj.shape[0]
    w = lb_raw.shape[1]
    nheads = w // HG_HEAD
    hpb = heads_per_block
    while nheads % hpb:
        hpb -= 1
    bw = hpb * HG_HEAD
    ngrp = nheads // hpb
    assert col0 % bw == 0
    cb = col0 // bw
    tb = min(tb, seq)
    chunk = min(GLA_CHUNK, tb)
    nt = seq // tb
    msum = jnp.asarray(_gla_sum_matrix(chunk), BF16)

    def sec(s):
        return pl.BlockSpec((tb, bw), lambda b, hg, t: (b * nt + t, cb + s * ngrp + hg))

    vec = pl.BlockSpec((1, bw), lambda b, hg, t: (0, hg))
    return pl.pallas_call(
        functools.partial(_hgrn_kernel, layer=layer, heads=hpb, chunk=chunk),
        grid=(batch, ngrp, nt),
        in_specs=[sec(0), sec(1), sec(2), sec(3),
                  pl.BlockSpec((lb_raw.shape[0], bw), lambda b, hg, t: (0, hg)),
                  vec, vec,
                  pl.BlockSpec(msum.shape, lambda b, hg, t: (0, 0))],
        out_specs=pl.BlockSpec((tb, bw), lambda b, hg, t: (b * nt + t, hg)),
        out_shape=jax.ShapeDtypeStruct((n, w), BF16),
        scratch_shapes=[pltpu.VMEM((hpb, HG_HEAD, HG_HEAD), F32)],
        compiler_params=_cparams(3),
        name="hgrn2",
    )(proj, proj, proj, proj, lb_raw, gain, bias, msum)


def _gla_kernel(q_ref, k_ref, v_ref, g_ref, wg_ref, bg_ref, gain_ref, msum_ref,
                o_ref, s_ref, *, heads, chunk, dk, dv, q_scale):
    tb = q_ref.shape[0]

    @pl.when(pl.program_id(2) == 0)
    def _():
        s_ref[...] = jnp.zeros_like(s_ref)

    wg = wg_ref[...]
    bg = bg_ref[...]
    gain = gain_ref[...]
    msum = msum_ref[...]
    masks = _gla_pair_masks(chunk)

    def body(c, carry):
        rows = pl.ds(c * chunk, chunk)
        q_raw = q_ref[rows, :]
        q_bf = q_raw.astype(BF16)
        pre = jnp.concatenate([_dot(q_bf[:, hd * dk:(hd + 1) * dk], wg[:, hd * dk:(hd + 1) * dk])
                               for hd in range(heads)], axis=1) + bg
        log_a = (jnp.minimum(pre, 0.0) - jnp.log(1.0 + jnp.exp(-jnp.abs(pre)))) / GLA_GATE_TEMP
        outs = _gla_chunk_heads(q_raw * q_scale, k_ref[rows, :], v_ref[rows, :], log_a,
                                s_ref, msum, masks, chunk, dk, dv, heads)
        g_in = g_ref[rows, :]
        swish = g_in * _sigmoid(g_in)
        for hd, o in enumerate(outs):
            cs = slice(hd * dv, (hd + 1) * dv)
            y = o * lax.rsqrt(jnp.mean(o * o, axis=-1, keepdims=True) + LN_EPS)
            o_ref[rows, cs] = (y * gain[:, cs] * swish[:, cs]).astype(o_ref.dtype)
        return carry

    for c in range(tb // chunk):
        body(c, 0)


def gla(proj, q0, k0, v0, g0, w_gate, b_gate, gain, *, batch, seq, dk, dv, q_scale,
        heads_per_block=4, tb=512):
    n = proj.shape[0]
    nheads = w_gate.shape[1] // dk
    hpb = min(heads_per_block, nheads)
    ngrp = nheads // hpb
    tb = min(tb, seq)
    chunk = min(GLA_CHUNK, tb)
    nt = seq // tb
    msum = jnp.asarray(_gla_sum_matrix(chunk), BF16)
    kw, vw = hpb * dk, hpb * dv
    assert q0 % kw == 0 and k0 % kw == 0 and v0 % vw == 0 and g0 % vw == 0

    def rows(width, col0):
        cb = col0 // width
        return pl.BlockSpec((tb, width), lambda b, hg, t: (b * nt + t, cb + hg))

    return pl.pallas_call(
        functools.partial(_gla_kernel, heads=hpb, chunk=chunk, dk=dk, dv=dv, q_scale=q_scale),
        grid=(batch, ngrp, nt),
        in_specs=[rows(kw, q0), rows(kw, k0), rows(vw, v0), rows(vw, g0),
                  pl.BlockSpec((w_gate.shape[0], kw), lambda b, hg, t: (0, hg)),
                  pl.BlockSpec((1, kw), lambda b, hg, t: (0, hg)),
                  pl.BlockSpec((1, vw), lambda b, hg, t: (0, hg)),
                  pl.BlockSpec(msum.shape, lambda b, hg, t: (0, 0))],
        out_specs=pl.BlockSpec((tb, vw), lambda b, hg, t: (b * nt + t, hg)),
        out_shape=jax.ShapeDtypeStruct((n, nheads * dv), BF16),
        scratch_shapes=[pltpu.VMEM((hpb, dv, dk), F32)],
        compiler_params=_cparams(3),
        name="gla",
    )(proj, proj, proj, proj, w_gate, b_gate, gain, msum)


S5_GB = LANES // S5_GROUP


def _s5_scan_steps(nchunks):
    return max(1, int(math.ceil(math.log2(nchunks)))) if nchunks > 1 else 0


def _s5_prep_kernel(ar_ref, ai_ref, ldt_ref, b2_ref, c2_ref, d_ref,
                    ktoep_ref, win_ref, wo_ref, lscan_ref, *, nsteps):
    rows, p2 = ar_ref.shape
    half = p2 // 2
    t_sub = S5_T
    ar = ar_ref[...]
    ai = ai_ref[...]
    dt = jnp.exp(ldt_ref[...])
    lane = lax.broadcasted_iota(jnp.int32, (1, p2), 1)
    sgn_im = jnp.where(lane < half, -1.0, 1.0)
    sgn_re = -sgn_im

    def lam_pow(k):
        mag = jnp.exp(float(k) * (ar * dt))
        th = float(k) * (ai * dt)
        return mag * jnp.cos(th), mag * jnp.sin(th)

    pows = [lam_pow(k) for k in range(t_sub + 1)]

    def cmul(x, k):
        l_re, l_im = pows[k]
        return x * l_re + pltpu.roll(x, half, axis=1) * (l_im * sgn_im)

    lam_re, lam_im = pows[1]
    den = ar * ar + ai * ai
    nr = lam_re - 1.0
    ni = lam_im
    coef_re = (nr * ar + ni * ai) / den
    coef_im = (ni * ar - nr * ai) / den
    b2 = b2_ref[...]
    bbar = b2 * coef_re + pltpu.roll(b2, half, axis=1) * (coef_im * sgn_im)
    c2 = c2_ref[...]

    rgrp = lax.broadcasted_iota(jnp.int32, (rows, rows), 0) // S5_GROUP
    cgrp = lax.broadcasted_iota(jnp.int32, (rows, rows), 1) // S5_GROUP
    same_grp = rgrp == cgrp
    r_i = lax.broadcasted_iota(jnp.int32, (rows, rows), 0)
    c_i = lax.broadcasted_iota(jnp.int32, (rows, rows), 1)
    hp = lax.Precision.HIGHEST

    for j in range(t_sub):
        k = t_sub - 1 - j
        tap = jnp.where(same_grp, _dot_nt(cmul(bbar, k) * sgn_re, c2, hp), 0.0)
        if k == 0:
            tap = tap + jnp.where(r_i == c_i, d_ref[...], 0.0)
        ktoep_ref[j * rows:(j + 1) * rows, :] = tap.astype(ktoep_ref.dtype)

    grp_of_row = lax.broadcasted_iota(jnp.int32, (rows, p2), 0) // S5_GROUP

    def block_diag(tile):
        return jnp.concatenate([jnp.where(grp_of_row == gg, tile, 0.0) for gg in range(S5_GB)], axis=1)

    for s in range(t_sub):
        win_ref[s * rows:(s + 1) * rows, :] = block_diag(cmul(bbar, t_sub - 1 - s)).astype(win_ref.dtype)
        wo_ref[s * rows:(s + 1) * rows, :] = block_diag(cmul(c2, s + 1) * sgn_re).astype(wo_ref.dtype)

    def group_rows(tile):
        return jnp.concatenate([tile[gg * S5_GROUP:gg * S5_GROUP + 1, :] for gg in range(S5_GB)], axis=1)

    cur_re, cur_im = pows[t_sub]
    rows_re, rows_sw = [], []
    for _ in range(nsteps):
        rows_re.append(group_rows(cur_re))
        rows_sw.append(group_rows(cur_im * sgn_im))
        cur_re, cur_im = cur_re * cur_re - cur_im * cur_im, 2.0 * cur_re * cur_im
    pad = lscan_ref.shape[0] - 2 * nsteps
    parts = rows_re + rows_sw + ([jnp.zeros((pad, S5_GB * p2), F32)] if pad else [])
    lscan_ref[...] = jnp.concatenate(parts, axis=0)


def s5_prep(ar_rows, ai_rows, ldt_rows, b2_rows, c2_rows, d_row, nsteps):
    wd, p2 = ar_rows.shape
    nblk = wd // LANES
    lrows = ((2 * nsteps + 7) // 8) * 8
    tile = pl.BlockSpec((LANES, p2), lambda i: (i, 0))
    return pl.pallas_call(
        functools.partial(_s5_prep_kernel, nsteps=nsteps),
        grid=(nblk,),
        in_specs=[tile, tile, tile, tile, tile, pl.BlockSpec((1, LANES), lambda i: (0, i))],
        out_specs=[pl.BlockSpec((None, S5_T * LANES, LANES), lambda i: (i, 0, 0)),
                   pl.BlockSpec((None, S5_T * LANES, S5_GB * p2), lambda i: (i, 0, 0)),
                   pl.BlockSpec((None, S5_T * LANES, S5_GB * p2), lambda i: (i, 0, 0)),
                   pl.BlockSpec((None, lrows, S5_GB * p2), lambda i: (i, 0, 0))],
        out_shape=[jax.ShapeDtypeStruct((nblk, S5_T * LANES, LANES), BF16),
                   jax.ShapeDtypeStruct((nblk, S5_T * LANES, S5_GB * p2), BF16),
                   jax.ShapeDtypeStruct((nblk, S5_T * LANES, S5_GB * p2), BF16),
                   jax.ShapeDtypeStruct((nblk, lrows, S5_GB * p2), F32)],
        compiler_params=_cparams(1),
        name="s5_prep",
    )(ar_rows, ai_rows, ldt_rows, b2_rows, c2_rows, d_row)


def _s5_main_kernel(u_ref, ktoep_ref, win_ref, wo_ref, lscan_ref, y_ref, *, nsteps):
    t_sub = S5_T
    nch = u_ref.shape[0] // t_sub
    p2 = win_ref.shape[1] // S5_GB
    half = p2 // 2
    xcat = jnp.concatenate([u_ref[pl.ds(s, nch, stride=t_sub), :].astype(BF16) for s in range(t_sub)],
                           axis=1)
    z = _dot(xcat, win_ref[...])
    pos = lax.broadcasted_iota(jnp.int32, (nch, p2), 0)
    lscan = lscan_ref[...]
    xprev = []
    for gg in range(S5_GB):
        cols = slice(gg * p2, (gg + 1) * p2)
        x = z[:, cols]
        for j in range(nsteps):
            d = 1 << j
            sh = jnp.where(pos >= d, pltpu.roll(x, d, axis=0), 0.0)
            x = (x + sh * lscan[j:j + 1, cols]
                 + pltpu.roll(sh, half, axis=1) * lscan[nsteps + j:nsteps + j + 1, cols])
        xprev.append(jnp.where(pos >= 1, pltpu.roll(x, 1, axis=0), 0.0).astype(BF16))
    y_state = _dot_nt(jnp.concatenate(xprev, axis=1), wo_ref[...])
    zero_blk = jnp.zeros((LANES, LANES), ktoep_ref.dtype)
    for t in range(0, t_sub, 2):
        taps = jnp.concatenate(
            [jnp.concatenate([ktoep_ref[(t_sub - 1 - t) * LANES:, :], zero_blk], axis=0),
             ktoep_ref[(t_sub - 2 - t) * LANES:, :]], axis=1)
        y_pair = y_state[:, t * LANES:(t + 2) * LANES] + _dot(xcat[:, :(t + 2) * LANES], taps)
        y_ref[pl.ds(t, nch, stride=t_sub), :] = y_pair[:, :LANES]
        y_ref[pl.ds(t + 1, nch, stride=t_sub), :] = y_pair[:, LANES:]


def s5_main(proj, col0, wd, ktoep, win, wo, lscan, *, nsteps, batch, seq):
    n = proj.shape[0]
    nblk = wd // LANES
    assert col0 % LANES == 0
    cb = col0 // LANES

    def wspec(a):
        return pl.BlockSpec((None,) + a.shape[1:], lambda i, b: (i, 0, 0))

    return pl.pallas_call(
        functools.partial(_s5_main_kernel, nsteps=nsteps),
        grid=(nblk, batch),
        in_specs=[pl.BlockSpec((seq, LANES), lambda i, b: (b, cb + i)),
                  wspec(ktoep), wspec(win), wspec(wo), wspec(lscan)],
        out_specs=pl.BlockSpec((seq, LANES), lambda i, b: (b, i)),
        out_shape=jax.ShapeDtypeStruct((n, wd), F32),
        compiler_params=_cparams(2),
        name="s5_main",
    )(proj, ktoep, win, wo, lscan)


def _s5_glu_kernel(y_ref, w_ref, b_ref, o_ref, wbf_ref):
    @pl.when(pl.program_id(0) == 0)
    def _():
        wbf_ref[...] = w_ref[...].astype(BF16)

    y = y_ref[...]
    z = 0.5 * y * (1.0 + jnp.tanh(math.sqrt(2.0 / math.pi) * (y + 0.044715 * (y * y * y))))
    gate = _dot(z.astype(BF16), wbf_ref[...]) + b_ref[...]
    o_ref[...] = (z * _sigmoid(gate)).astype(o_ref.dtype)


def s5_glu(y, w_stack, layer, b, tm=512):
    n, wd = y.shape
    tm = min(tm, n)
    return pl.pallas_call(
        _s5_glu_kernel,
        grid=(n // tm,),
        in_specs=[pl.BlockSpec((tm, wd), lambda i: (i, 0)),
                  pl.BlockSpec((None, wd, wd), lambda i: (layer, 0, 0)),
                  pl.BlockSpec((1, wd), lambda i: (0, 0))],
        out_specs=pl.BlockSpec((tm, wd), lambda i: (i, 0)),
        out_shape=jax.ShapeDtypeStruct((n, wd), BF16),
        scratch_shapes=[pltpu.VMEM((wd, wd), BF16)],
        compiler_params=_cparams(1),
        name="s5_glu",
    )(y, w_stack, b)


def s5_mixer(proj, col0, a_re, a_im, log_dt, b_re, b_im, c_re, c_im, d_skip, glu_w_stack, layer, glu_b,
             *, batch, seq):
    wd = d_skip.shape[0]
    g, p = a_re.shape
    assert wd // g == S5_GROUP and seq % S5_T == 0 and wd % LANES == 0
    nsteps = _s5_scan_steps(seq // S5_T)
    per_row = lambda a: jnp.repeat(jnp.concatenate([a, a], axis=-1), S5_GROUP, axis=0)
    ldt_rows = jnp.broadcast_to(jnp.repeat(log_dt, S5_GROUP)[:, None], (wd, 2 * p))
    b2_rows = jnp.concatenate([b_re.transpose(0, 2, 1), b_im.transpose(0, 2, 1)], axis=-1).reshape(wd, 2 * p)
    c2_rows = jnp.concatenate([c_re, c_im], axis=-1).reshape(wd, 2 * p)
    ktoep, win, wo, lscan = s5_prep(per_row(a_re), per_row(a_im), ldt_rows, b2_rows, c2_rows,
                                    d_skip[None, :], nsteps)
    y = s5_main(proj, col0, wd, ktoep, win, wo, lscan, nsteps=nsteps, batch=batch, seq=seq)
    return s5_glu(y, glu_w_stack, layer, glu_b[None, :])


def _pad_heads(w, heads, width, new):
    r = w.shape[0]
    return jnp.pad(w.reshape(r, heads, width), ((0, 0), (0, 0), (0, new - width))).reshape(r, heads * new)


def kernel(x, c, w_ada, b_ada, ada_table, w_in, w_out, s5_a_re, s5_a_im, s5_log_dt, s5_b_re, s5_b_im, s5_c_re, s5_c_im, s5_d, s5_glu_w, s5_glu_b, hg_lb_raw, hg_norm_gain, hg_norm_bias, gla_w_gate, gla_b_gate, gla_norm_gain, w_ffn_gate, w_ffn_up, w_ffn_down, ln1_gain, ln1_bias, ln2_gain, ln2_bias):
    bsz, seq, d = x.shape
    depth = w_in.shape[0]
    n = bsz * seq
    s5_w = s5_d.shape[1]
    hg_w = hg_lb_raw.shape[1]
    gla_kw = gla_b_gate.shape[1]
    gla_vw = gla_norm_gain.shape[1]
    rank = gla_w_gate.shape[1]
    gla_dk = gla_kw // GLA_HEADS
    gla_dv = gla_vw // GLA_HEADS
    dk_pad = ((gla_dk + GLA_DK_PAD - 1) // GLA_DK_PAD) * GLA_DK_PAD
    assert dk_pad - gla_dk >= rank
    alpha = (2.0 * depth) ** 0.25

    rows = ((bsz + 7) // 8) * 8
    c_pad = jnp.pad(c, ((0, rows - bsz), (0, 0)))
    mod = cond_table(c_pad, w_ada, b_ada[None, :], ada_table.reshape(depth, N_MOD * d))
    mod = mod[:, :bsz].reshape(depth, bsz, N_MOD, d)

    o_u = 0
    o_hg = o_u + s5_w
    o_q = o_hg + 4 * hg_w
    o_k = o_q + gla_kw
    o_v = o_k + gla_kw
    o_lr = o_v + 2 * gla_vw

    hk = GLA_HEADS * dk_pad
    p_v = 0
    p_g = p_v + gla_vw
    p_hg = p_g + gla_vw
    p_u = p_hg + 4 * hg_w
    p_q = p_u + s5_w
    p_k = p_q + hk
    w_t = jnp.transpose(w_in, (0, 2, 1))
    lr_rows = jnp.broadcast_to(w_t[:, None, o_lr:o_lr + rank], (depth, GLA_HEADS, rank, d))
    q_rows = jnp.concatenate([w_t[:, o_q:o_k].reshape(depth, GLA_HEADS, gla_dk, d), lr_rows,
                              jnp.zeros((depth, GLA_HEADS, dk_pad - gla_dk - rank, d), F32)], axis=2)
    k_rows = jnp.pad(w_t[:, o_k:o_v].reshape(depth, GLA_HEADS, gla_dk, d),
                     ((0, 0), (0, 0), (0, dk_pad - gla_dk), (0, 0)))
    w_all = jnp.concatenate([w_t[:, o_v:o_lr], w_t[:, o_hg:o_q], w_t[:, o_u:o_hg],
                             q_rows.reshape(depth, hk, d), k_rows.reshape(depth, hk, d)], axis=1).astype(BF16)
    w_out_bf = w_out.astype(BF16)
    w_gate_bf = w_ffn_gate[0].astype(BF16)
    w_up_bf = w_ffn_up[0].astype(BF16)

    x2 = x.reshape(n, d)
    h = modulate(x2, mod[0], seq)
    for l in range(depth):
        proj = matmul_ws([h], w_all, l, F32, tm=1024, w_is_nk=True)
        y_a = s5_mixer(proj, p_u, s5_a_re[l], s5_a_im[l], s5_log_dt[l], s5_b_re[l],
                       s5_b_im[l], s5_c_re[l], s5_c_im[l], s5_d[l], s5_glu_w, l, s5_glu_b[l],
                       batch=bsz, seq=seq)
        y_b = hgrn2(proj, p_hg, hg_lb_raw, hg_norm_gain[l][None, :], hg_norm_bias[l][None, :],
                    layer=l, batch=bsz, seq=seq)
        w_gate = jnp.pad(
            jnp.pad(gla_w_gate[l].reshape(rank, GLA_HEADS, gla_dk), ((0, 0), (0, 0), (0, dk_pad - gla_dk))),
            ((gla_dk, dk_pad - gla_dk - rank), (0, 0), (0, 0))).reshape(dk_pad, hk).astype(BF16)
        b_gate = _pad_heads(gla_b_gate[l][None, :], GLA_HEADS, gla_dk, dk_pad)
        y_c = gla(proj, p_q, p_k, p_v, p_g, w_gate, b_gate, gla_norm_gain[l][None, :],
                  batch=bsz, seq=seq, dk=dk_pad, dv=gla_dv, q_scale=float(gla_dk) ** -0.5)
        mixed = matmul_ws([y_a, y_b, y_c], w_out_bf, l, BF16, tm=1024)
        x2, h = ln_mod(x2, mixed, mod[l], mod[l], ln1_gain[l][None, :], ln1_bias[l][None, :], seq,
                       alpha=alpha, gate_row=2, next_row=3, with_h=True)

        last = l == depth - 1
        jobs = [(w_ffn_down, l)] + ([] if last else [(w_ffn_gate, l + 1), (w_ffn_up, l + 1)])
        act, casts = ffn_up(h, w_gate_bf, w_up_bf, jobs)
        if not last:
            w_gate_bf, w_up_bf = casts[1], casts[2]
        ffn = matmul_ws([act], casts[0][None], 0, BF16, tn=512)
        x2, h = ln_mod(x2, ffn, mod[l], mod[l if last else l + 1],
                       ln2_gain[l][None, :], ln2_bias[l][None, :], seq,
                       alpha=alpha, gate_row=5, next_row=0, with_h=not last)
    return x2.reshape(bsz, seq, d)
```

```python
import functools
import math

import numpy as np
import jax
import jax.numpy as jnp
from jax import lax
from jax.experimental import pallas as pl
from jax.experimental.pallas import tpu as pltpu

F32 = jnp.float32
BF16 = jnp.bfloat16

LANES = 128
SUBLANES = 8
LOG2_E = 1.0 / math.log(2.0)
V7X_VMEM_BYTES = 64 * 1024 * 1024
VMEM_COMPILER_RESERVE = 8 * 1024 * 1024
VMEM_LIMIT = V7X_VMEM_BYTES - VMEM_COMPILER_RESERVE

S5_GROUP = 16
HG_HEAD = 128
GLA_HEADS = 4
GLA_GATE_TEMP = 16.0
N_MOD = 6
LN_EPS = 1e-5

S5_T = 16
GLA_CHUNK = 128
GLA_DK_PAD = 256
FFN_SUB_ROWS = 512


def _cparams(n_axes):
    return pltpu.CompilerParams(
        dimension_semantics=("arbitrary",) * n_axes, vmem_limit_bytes=VMEM_LIMIT)


def _sigmoid(x):
    return 1.0 / (1.0 + jnp.exp(-x))


def _dot(a, b):
    return jnp.dot(a, b, preferred_element_type=F32)


def _dot_nt(a, b, precision=None):
    return lax.dot_general(a, b, (((1,), (1,)), ((), ())),
                           preferred_element_type=F32, precision=precision)


def _pick_tile(n, cap):
    best = None
    for t in range(LANES, min(n, cap) + 1, LANES):
        if n % t == 0:
            best = t
    assert best is not None, (n, cap)
    return best


def _mm_ws_kernel(*refs, n_a, w_is_nk):
    a_refs, w_ref, o_ref = refs[:n_a], refs[n_a], refs[n_a + 1]
    if w_is_nk:
        acc = _dot_nt(a_refs[0][...], w_ref[...])
    else:
        acc = None
        r0 = 0
        for a_ref in a_refs:
            k = a_ref.shape[1]
            part = _dot(a_ref[...], w_ref[r0:r0 + k, :])
            acc = part if acc is None else acc + part
            r0 += k
    o_ref[...] = acc.astype(o_ref.dtype)


def matmul_ws(a_list, w_stack, layer, out_dtype, tm=512, tn=1024, w_is_nk=False):
    m = a_list[0].shape[0]
    if w_is_nk:
        assert len(a_list) == 1
        _, n, k = w_stack.shape
    else:
        _, k, n = w_stack.shape
    assert sum(a.shape[1] for a in a_list) == k
    tn = _pick_tile(n, tn)
    tm = min(tm, m)
    if w_is_nk:
        w_spec = pl.BlockSpec((None, tn, k), lambda j, i: (layer, j, 0))
    else:
        w_spec = pl.BlockSpec((None, k, tn), lambda j, i: (layer, 0, j))
    return pl.pallas_call(
        functools.partial(_mm_ws_kernel, n_a=len(a_list), w_is_nk=w_is_nk),
        grid=(n // tn, m // tm),
        in_specs=[pl.BlockSpec((tm, a.shape[1]), lambda j, i: (i, 0)) for a in a_list] + [w_spec],
        out_specs=pl.BlockSpec((tm, tn), lambda j, i: (i, j)),
        out_shape=jax.ShapeDtypeStruct((m, n), out_dtype),
        compiler_params=_cparams(2),
        name="matmul_ws",
    )(*a_list, w_stack)


def _ffn_up_kernel(a_ref, wg_ref, wu_ref, *refs, n_side):
    side_in, o_ref, side_out = refs[:n_side], refs[n_side], refs[n_side + 1:]
    wg = wg_ref[...]
    wu = wu_ref[...]
    tm = a_ref.shape[0]
    sub = min(tm, FFN_SUB_ROWS)
    for r0 in range(0, tm, sub):
        a = a_ref[r0:r0 + sub, :]
        g = _dot(a, wg)
        u = _dot(a, wu)
        o_ref[r0:r0 + sub, :] = (g * _sigmoid(g) * u).astype(o_ref.dtype)
    for s_in, s_out in zip(side_in, side_out):
        s_out[...] = s_in[...].astype(s_out.dtype)


def _slab_specs(shape, layer, gi, gj):
    r, c = shape
    cw = -(-(-(-c // gj)) // LANES) * LANES
    if r % gi == 0 and (r // gi) % 16 == 0 and -(-c // cw) == gj:
        blk = (r // gi, cw)
        return (pl.BlockSpec((None,) + blk, lambda i, j: (layer, i, j)), pl.BlockSpec(blk, lambda i, j: (i, j)))
    steps = gi * gj
    for rs in range(16, r + 1, 16):
        if r % rs == 0 and r // rs <= steps:
            last = r // rs - 1
            blk = (rs, c)
            return (pl.BlockSpec((None,) + blk, lambda i, j: (layer, jnp.minimum(i * gj + j, last), 0)),
                    pl.BlockSpec(blk, lambda i, j: (jnp.minimum(i * gj + j, last), 0)))
    return None


def ffn_up(h, wg, wu, cast_jobs, tm=2048, tn=256):
    m, k = h.shape
    n = wg.shape[1]
    tm = min(tm, m)
    gi, gj = m // tm, pl.cdiv(n, tn)
    specs = [_slab_specs(w.shape[1:], layer, gi, gj) for w, layer in cast_jobs]
    riding = [job for job, sp in zip(cast_jobs, specs) if sp is not None]
    rspecs = [sp for sp in specs if sp is not None]
    wspec = pl.BlockSpec((k, tn), lambda i, j: (0, j))
    res = pl.pallas_call(
        functools.partial(_ffn_up_kernel, n_side=len(riding)),
        grid=(gi, gj),
        in_specs=[pl.BlockSpec((tm, k), lambda i, j: (i, 0)), wspec, wspec] + [sp[0] for sp in rspecs],
        out_specs=[pl.BlockSpec((tm, tn), lambda i, j: (i, j))] + [sp[1] for sp in rspecs],
        out_shape=[jax.ShapeDtypeStruct((m, n), BF16)]
        + [jax.ShapeDtypeStruct(w.shape[1:], BF16) for w, _ in riding],
        compiler_params=_cparams(2),
        name="ffn_up",
    )(h, wg, wu, *[w for w, _ in riding])
    casts, it = [], iter(res[1:])
    for (w, layer), sp in zip(cast_jobs, specs):
        casts.append(next(it) if sp is not None else w[layer].astype(BF16))
    return res[0], casts


def _cond_kernel(c_ref, w_ref, b_ref, tab_ref, o_ref):
    c = c_ref[...]
    act = (c * _sigmoid(c)).astype(BF16)
    cond = _dot(act, w_ref[...].astype(BF16)) + b_ref[...]
    for l in range(tab_ref.shape[0]):
        o_ref[l] = cond + tab_ref[l:l + 1, :]


def cond_table(c_pad, w_ada, b_ada, ada_table2, tn=1024):
    rows, d = c_pad.shape
    n = w_ada.shape[1]
    depth = ada_table2.shape[0]
    tn = _pick_tile(n, tn)
    return pl.pallas_call(
        _cond_kernel,
        grid=(n // tn,),
        in_specs=[pl.BlockSpec((rows, d), lambda j: (0, 0)),
                  pl.BlockSpec((d, tn), lambda j: (0, j)),
                  pl.BlockSpec((1, tn), lambda j: (0, j)),
                  pl.BlockSpec((depth, tn), lambda j: (0, j))],
        out_specs=pl.BlockSpec((depth, rows, tn), lambda j: (0, 0, j)),
        out_shape=jax.ShapeDtypeStruct((depth, rows, n), F32),
        compiler_params=_cparams(1),
        name="cond_table",
    )(c_pad, w_ada, b_ada, ada_table2)


def _modulate_kernel(x_ref, mod_ref, h_ref):
    m = mod_ref[0]
    h_ref[...] = (x_ref[...] * (1.0 + m[1:2, :]) + m[0:1, :]).astype(h_ref.dtype)


def modulate(x2, mod_l, seq, tm=512):
    n, d = x2.shape
    tm = min(tm, seq)
    per_b = seq // tm
    return pl.pallas_call(
        _modulate_kernel,
        grid=(n // tm,),
        in_specs=[pl.BlockSpec((tm, d), lambda i: (i, 0)),
                  pl.BlockSpec((1, N_MOD, d), lambda i: (i // per_b, 0, 0))],
        out_specs=pl.BlockSpec((tm, d), lambda i: (i, 0)),
        out_shape=jax.ShapeDtypeStruct((n, d), BF16),
        compiler_params=_cparams(1),
        name="modulate",
    )(x2, mod_l)


def _ln_mod_kernel(x_ref, mm_ref, mod_ref, nmod_ref, gain_ref, bias_ref, xo_ref, *h_refs,
                   alpha, gate_row, next_row):
    m = mod_ref[0]
    z = alpha * x_ref[...] + (1.0 + m[gate_row:gate_row + 1, :]) * mm_ref[...].astype(F32)
    mu = jnp.mean(z, axis=-1, keepdims=True)
    zc = z - mu
    var = jnp.mean(zc * zc, axis=-1, keepdims=True)
    y = zc * lax.rsqrt(var + LN_EPS) * gain_ref[...] + bias_ref[...]
    xo_ref[...] = y
    if h_refs:
        nm = nmod_ref[0]
        h_refs[0][...] = (y * (1.0 + nm[next_row + 1:next_row + 2, :])
                          + nm[next_row:next_row + 1, :]).astype(BF16)


def ln_mod(x2, mm, mod_l, mod_next, gain, bias, seq, *, alpha, gate_row, next_row, with_h, tm=256):
    n, d = x2.shape
    tm = min(tm, seq)
    per_b = seq // tm
    row = pl.BlockSpec((tm, d), lambda i: (i, 0))
    modspec = pl.BlockSpec((1, N_MOD, d), lambda i: (i // per_b, 0, 0))
    vec = pl.BlockSpec((1, d), lambda i: (0, 0))
    out_shape = [jax.ShapeDtypeStruct((n, d), F32)]
    out_specs = [row]
    if with_h:
        out_shape.append(jax.ShapeDtypeStruct((n, d), BF16))
        out_specs.append(row)
    res = pl.pallas_call(
        functools.partial(_ln_mod_kernel, alpha=alpha, gate_row=gate_row, next_row=next_row),
        grid=(n // tm,),
        in_specs=[row, row, modspec, modspec, vec, vec],
        out_specs=out_specs,
        out_shape=out_shape,
        compiler_params=_cparams(1),
        name="ln_mod",
    )(x2, mm, mod_l, mod_next, gain, bias)
    return (res[0], res[1]) if with_h else (res[0], None)


def _gla_levels(chunk):
    lv = []
    h = chunk // 2
    while h >= 1:
        lv.append(h)
        h //= 2
    return lv


def _gla_sum_matrix(chunk):
    blocks = []
    idx = np.arange(chunk)
    for h in _gla_levels(chunk):
        m = np.zeros((chunk, chunk), np.float32)
        for i in range(chunk):
            r = (i // (2 * h)) * 2 * h + h - 1
            if i % (2 * h) >= h:
                m[i, r + 1:i + 1] = 1.0
            else:
                m[i, i + 1:r + 1] = 1.0
        blocks.append(m)
    blocks.append((idx[None, :] <= idx[:, None]).astype(np.float32))
    m = np.concatenate(blocks, axis=0)
    return np.concatenate([m, m], axis=1)


def _gla_pair_masks(chunk):
    ngrp = chunk // SUBLANES
    col = lax.broadcasted_iota(jnp.int32, (SUBLANES, chunk), 1)
    rows = [lax.broadcasted_iota(jnp.int32, (SUBLANES, chunk), 0) + r * SUBLANES for r in range(ngrp)]
    diag = [row == col for row in rows]
    pair = []
    for h in _gla_levels(chunk):
        blk = 2 * h
        pair.append([((row // blk) == (col // blk)) & ((row % blk) >= h) & ((col % blk) < h) for row in rows])
    rid = lax.broadcasted_iota(jnp.int32, (chunk, 1), 0)
    right = [(rid % (2 * h)) >= h for h in _gla_levels(chunk)]
    return diag, pair, right


def _row_groups(x, groups):
    parts, start, prev = [], None, None
    for r in groups:
        if start is None:
            start = r
        elif r != prev + 1:
            parts.append(x[start * SUBLANES:(prev + 1) * SUBLANES])
            start = r
        prev = r
    parts.append(x[start * SUBLANES:(prev + 1) * SUBLANES])
    return parts


def _gla_chunk_heads(q, k, v, g, s_ref, msum, masks, chunk, dk, dv, heads):
    levels = _gla_levels(chunk)
    nl = len(levels)
    ngrp = chunk // SUBLANES
    g2 = g * LOG2_E
    g_hi = g2.astype(BF16)
    g_lo = (g2 - g_hi.astype(F32)).astype(BF16)
    expo = _dot(msum, jnp.concatenate([g_hi, g_lo], axis=0))
    e_lv = jnp.exp2(expo[:nl * chunk, :])
    b_cum = expo[nl * chunk:, :]
    e_cum_all = jnp.exp2(b_cum)
    e_rev_all = jnp.exp2(b_cum[chunk - 1:chunk, :] - b_cum)

    diag, pair, right = masks
    outs = []
    for hd in range(heads):
        ks = slice(hd * dk, (hd + 1) * dk)
        qh, kh = q[:, ks], k[:, ks]
        vh = v[:, hd * dv:(hd + 1) * dv]
        dsum = jnp.sum(qh * kh, axis=-1, keepdims=True)
        a_rows = [jnp.where(diag[r], dsum[r * SUBLANES:(r + 1) * SUBLANES], 0.0) for r in range(ngrp)]
        for li, h in enumerate(levels):
            e = e_lv[li * chunk:(li + 1) * chunk, ks]
            if h >= SUBLANES:
                rgt = [r for r in range(ngrp) if (r * SUBLANES) % (2 * h) >= h]
                runs, cur = [], None
                for r in range(ngrp):
                    src = r in rgt
                    if cur is None or cur[0] != src:
                        cur = [src, r, r]
                        runs.append(cur)
                    else:
                        cur[2] = r
                sel = jnp.concatenate([(qh if src else kh)[a * SUBLANES:(b + 1) * SUBLANES]
                                       for src, a, b in runs], axis=0)
                prod = sel * e
                lhs = jnp.concatenate(_row_groups(prod, rgt), axis=0).astype(BF16)
                p = _dot_nt(lhs, prod.astype(BF16))
                for n_, r in enumerate(rgt):
                    a_rows[r] = jnp.where(pair[li][r], p[n_ * SUBLANES:(n_ + 1) * SUBLANES], a_rows[r])
            else:
                xb = (jnp.where(right[li], qh, kh) * e).astype(BF16)
                p = _dot_nt(xb, xb)
                a_rows = [jnp.where(pair[li][r], p[r * SUBLANES:(r + 1) * SUBLANES], a_rows[r])
                          for r in range(ngrp)]
        a = jnp.concatenate(a_rows, axis=0)
        e_cum = e_cum_all[:, ks]
        s_t = s_ref[hd]
        vt = vh.T.astype(BF16)
        o = _dot_nt(jnp.concatenate([a.astype(BF16), (qh * e_cum).astype(BF16)], axis=1),
                    jnp.concatenate([vt, s_t.astype(BF16)], axis=1))
        kb = (kh * e_rev_all[:, ks]).astype(BF16)
        s_ref[hd] = s_t * e_cum[chunk - 1:chunk, :] + _dot(vt, kb)
        outs.append(o)
    return outs


def _hgrn_kernel(q_ref, f_ref, i_ref, g_ref, lbraw_ref, gain_ref, bias_ref, msum_ref,
                 o_ref, s_ref, *, layer, heads, chunk):
    dk = dv = HG_HEAD
    tb = q_ref.shape[0]

    @pl.when(pl.program_id(2) == 0)
    def _():
        s_ref[...] = jnp.zeros_like(s_ref)

    raw = lbraw_ref[...]
    ex = jnp.exp(raw - jnp.max(raw, axis=0, keepdims=True))
    sm = ex / jnp.sum(ex, axis=0, keepdims=True)
    lb = jnp.zeros_like(sm[0:1, :])
    for m in range(1, layer + 1):
        lb = lb + sm[m:m + 1, :]
    gain = gain_ref[...]
    bias = bias_ref[...]
    msum = msum_ref[...]
    masks = _gla_pair_masks(chunk)

    def body(c, carry):
        rows = pl.ds(c * chunk, chunk)
        q_in = q_ref[rows, :]
        f = lb + (1.0 - lb) * _sigmoid(f_ref[rows, :])
        q = q_in * _sigmoid(q_in)
        outs = _gla_chunk_heads(q, 1.0 - f, i_ref[rows, :], jnp.log(f), s_ref, msum, masks,
                                chunk, dk, dv, heads)
        gate = _sigmoid(g_ref[rows, :])
        for hd, o in enumerate(outs):
            cs = slice(hd * dv, (hd + 1) * dv)
            y = gate[:, cs] * o
            mu = jnp.mean(y, axis=-1, keepdims=True)
            yc = y - mu
            var = jnp.mean(yc * yc, axis=-1, keepdims=True)
            yn = yc * lax.rsqrt(var + LN_EPS)
            o_ref[rows, cs] = (yn * gain[:, cs] + bias[:, cs]).astype(o_ref.dtype)
        return carry

    for c in range(tb // chunk):
        body(c, 0)


def hgrn2(proj, col0, lb_raw, gain, bias, *, layer, batch, seq, heads_per_block=4, tb=1024):
    n = proj.shape[0]
    w = lb_raw.shape[1]
    nheads = w // HG_HEAD
    hpb = heads_per_block
    while nheads % hpb:
        hpb -= 1
    bw = hpb * HG_HEAD
    ngrp = nheads // hpb
    assert col0 % bw == 0
    cb = col0 // bw
    tb = min(tb, seq)
    chunk = min(GLA_CHUNK, tb)
    nt = seq // tb
    msum = jnp.asarray(_gla_sum_matrix(chunk), BF16)

    def sec(s):
        return pl.BlockSpec((tb, bw), lambda b, hg, t: (b * nt + t, cb + s * ngrp + hg))

    vec = pl.BlockSpec((1, bw), lambda b, hg, t: (0, hg))
    return pl.pallas_call(
        functools.partial(_hgrn_kernel, layer=layer, heads=hpb, chunk=chunk),
        grid=(batch, ngrp, nt),
        in_specs=[sec(0), sec(1), sec(2), sec(3),
                  pl.BlockSpec((lb_raw.shape[0], bw), lambda b, hg, t: (0, hg)),
                  vec, vec,
                  pl.BlockSpec(msum.shape, lambda b, hg, t: (0, 0))],
        out_specs=pl.BlockSpec((tb, bw), lambda b, hg, t: (b * nt + t, hg)),
        out_shape=jax.ShapeDtypeStruct((n, w), BF16),
        scratch_shapes=[pltpu.VMEM((hpb, HG_HEAD, HG_HEAD), F32)],
        compiler_params=_cparams(3),
        name="hgrn2",
    )(proj, proj, proj, proj, lb_raw, gain, bias, msum)


def _gla_kernel(q_ref, k_ref, v_ref, g_ref, wg_ref, bg_ref, gain_ref, msum_ref,
                o_ref, s_ref, *, heads, chunk, dk, dv, q_scale):
    tb = q_ref.shape[0]

    @pl.when(pl.program_id(2) == 0)
    def _():
        s_ref[...] = jnp.zeros_like(s_ref)

    wg = wg_ref[...]
    bg = bg_ref[...]
    gain = gain_ref[...]
    msum = msum_ref[...]
    masks = _gla_pair_masks(chunk)

    def body(c, carry):
        rows = pl.ds(c * chunk, chunk)
        q_raw = q_ref[rows, :]
        q_bf = q_raw.astype(BF16)
        pre = jnp.concatenate([_dot(q_bf[:, hd * dk:(hd + 1) * dk], wg[:, hd * dk:(hd + 1) * dk])
                               for hd in range(heads)], axis=1) + bg
        log_a = (jnp.minimum(pre, 0.0) - jnp.log(1.0 + jnp.exp(-jnp.abs(pre)))) / GLA_GATE_TEMP
        outs = _gla_chunk_heads(q_raw * q_scale, k_ref[rows, :], v_ref[rows, :], log_a,
                                s_ref, msum, masks, chunk, dk, dv, heads)
        g_in = g_ref[rows, :]
        swish = g_in * _sigmoid(g_in)
        for hd, o in enumerate(outs):
            cs = slice(hd * dv, (hd + 1) * dv)
            y = o * lax.rsqrt(jnp.mean(o * o, axis=-1, keepdims=True) + LN_EPS)
            o_ref[rows, cs] = (y * gain[:, cs] * swish[:, cs]).astype(o_ref.dtype)
        return carry

    for c in range(tb // chunk):
        body(c, 0)


def gla(proj, q0, k0, v0, g0, w_gate, b_gate, gain, *, batch, seq, dk, dv, q_scale,
        heads_per_block=4, tb=512):
    n = proj.shape[0]
    nheads = w_gate.shape[1] // dk
    hpb = min(heads_per_block, nheads)
    ngrp = nheads // hpb
    tb = min(tb, seq)
    chunk = min(GLA_CHUNK, tb)
    nt = seq // tb
    msum = jnp.asarray(_gla_sum_matrix(chunk), BF16)
    kw, vw = hpb * dk, hpb * dv
    assert q0 % kw == 0 and k0 % kw == 0 and v0 % vw == 0 and g0 % vw == 0

    def rows(width, col0):
        cb = col0 // width
        return pl.BlockSpec((tb, width), lambda b, hg, t: (b * nt + t, cb + hg))

    return pl.pallas_call(
        functools.partial(_gla_kernel, heads=hpb, chunk=chunk, dk=dk, dv=dv, q_scale=q_scale),
        grid=(batch, ngrp, nt),
        in_specs=[rows(kw, q0), rows(kw, k0), rows(vw, v0), rows(vw, g0),
                  pl.BlockSpec((w_gate.shape[0], kw), lambda b, hg, t: (0, hg)),
                  pl.BlockSpec((1, kw), lambda b, hg, t: (0, hg)),
                  pl.BlockSpec((1, vw), lambda b, hg, t: (0, hg)),
                  pl.BlockSpec(msum.shape, lambda b, hg, t: (0, 0))],
        out_specs=pl.BlockSpec((tb, vw), lambda b, hg, t: (b * nt + t, hg)),
        out_shape=jax.ShapeDtypeStruct((n, nheads * dv), BF16),
        scratch_shapes=[pltpu.VMEM((hpb, dv, dk), F32)],
        compiler_params=_cparams(3),
        name="gla",
    )(proj, proj, proj, proj, w_gate, b_gate, gain, msum)


S5_GB = LANES // S5_GROUP


def _s5_scan_steps(nchunks):
    return max(1, int(math.ceil(math.log2(nchunks)))) if nchunks > 1 else 0


def _s5_prep_kernel(ar_ref, ai_ref, ldt_ref, b2_ref, c2_ref, d_ref,
                    ktoep_ref, win_ref, wo_ref, lscan_ref, *, nsteps):
    rows, p2 = ar_ref.shape
    half = p2 // 2
    t_sub = S5_T
    ar = ar_ref[...]
    ai = ai_ref[...]
    dt = jnp.exp(ldt_ref[...])
    lane = lax.broadcasted_iota(jnp.int32, (1, p2), 1)
    sgn_im = jnp.where(lane < half, -1.0, 1.0)
    sgn_re = -sgn_im

    def lam_pow(k):
        mag = jnp.exp(float(k) * (ar * dt))
        th = float(k) * (ai * dt)
        return mag * jnp.cos(th), mag * jnp.sin(th)

    pows = [lam_pow(k) for k in range(t_sub + 1)]

    def cmul(x, k):
        l_re, l_im = pows[k]
        return x * l_re + pltpu.roll(x, half, axis=1) * (l_im * sgn_im)

    lam_re, lam_im = pows[1]
    den = ar * ar + ai * ai
    nr = lam_re - 1.0
    ni = lam_im
    coef_re = (nr * ar + ni * ai) / den
    coef_im = (ni * ar - nr * ai) / den
    b2 = b2_ref[...]
    bbar = b2 * coef_re + pltpu.roll(b2, half, axis=1) * (coef_im * sgn_im)
    c2 = c2_ref[...]

    rgrp = lax.broadcasted_iota(jnp.int32, (rows, rows), 0) // S5_GROUP
    cgrp = lax.broadcasted_iota(jnp.int32, (rows, rows), 1) // S5_GROUP
    same_grp = rgrp == cgrp
    r_i = lax.broadcasted_iota(jnp.int32, (rows, rows), 0)
    c_i = lax.broadcasted_iota(jnp.int32, (rows, rows), 1)
    hp = lax.Precision.HIGHEST

    for j in range(t_sub):
        k = t_sub - 1 - j
        tap = jnp.where(same_grp, _dot_nt(cmul(bbar, k) * sgn_re, c2, hp), 0.0)
        if k == 0:
            tap = tap + jnp.where(r_i == c_i, d_ref[...], 0.0)
        ktoep_ref[j * rows:(j + 1) * rows, :] = tap.astype(ktoep_ref.dtype)

    grp_of_row = lax.broadcasted_iota(jnp.int32, (rows, p2), 0) // S5_GROUP

    def block_diag(tile):
        return jnp.concatenate([jnp.where(grp_of_row == gg, tile, 0.0) for gg in range(S5_GB)], axis=1)

    for s in range(t_sub):
        win_ref[s * rows:(s + 1) * rows, :] = block_diag(cmul(bbar, t_sub - 1 - s)).astype(win_ref.dtype)
        wo_ref[s * rows:(s + 1) * rows, :] = block_diag(cmul(c2, s + 1) * sgn_re).astype(wo_ref.dtype)

    def group_rows(tile):
        return jnp.concatenate([tile[gg * S5_GROUP:gg * S5_GROUP + 1, :] for gg in range(S5_GB)], axis=1)

    cur_re, cur_im = pows[t_sub]
    rows_re, rows_sw = [], []
    for _ in range(nsteps):
        rows_re.append(group_rows(cur_re))
        rows_sw.append(group_rows(cur_im * sgn_im))
        cur_re, cur_im = cur_re * cur_re - cur_im * cur_im, 2.0 * cur_re * cur_im
    pad = lscan_ref.shape[0] - 2 * nsteps
    parts = rows_re + rows_sw + ([jnp.zeros((pad, S5_GB * p2), F32)] if pad else [])
    lscan_ref[...] = jnp.concatenate(parts, axis=0)


def s5_prep(ar_rows, ai_rows, ldt_rows, b2_rows, c2_rows, d_row, nsteps):
    wd, p2 = ar_rows.shape
    nblk = wd // LANES
    lrows = ((2 * nsteps + 7) // 8) * 8
    tile = pl.BlockSpec((LANES, p2), lambda i: (i, 0))
    return pl.pallas_call(
        functools.partial(_s5_prep_kernel, nsteps=nsteps),
        grid=(nblk,),
        in_specs=[tile, tile, tile, tile, tile, pl.BlockSpec((1, LANES), lambda i: (0, i))],
        out_specs=[pl.BlockSpec((None, S5_T * LANES, LANES), lambda i: (i, 0, 0)),
                   pl.BlockSpec((None, S5_T * LANES, S5_GB * p2), lambda i: (i, 0, 0)),
                   pl.BlockSpec((None, S5_T * LANES, S5_GB * p2), lambda i: (i, 0, 0)),
                   pl.BlockSpec((None, lrows, S5_GB * p2), lambda i: (i, 0, 0))],
        out_shape=[jax.ShapeDtypeStruct((nblk, S5_T * LANES, LANES), BF16),
                   jax.ShapeDtypeStruct((nblk, S5_T * LANES, S5_GB * p2), BF16),
                   jax.ShapeDtypeStruct((nblk, S5_T * LANES, S5_GB * p2), BF16),
                   jax.ShapeDtypeStruct((nblk, lrows, S5_GB * p2), F32)],
        compiler_params=_cparams(1),
        name="s5_prep",
    )(ar_rows, ai_rows, ldt_rows, b2_rows, c2_rows, d_row)


def _s5_main_kernel(u_ref, ktoep_ref, win_ref, wo_ref, lscan_ref, y_ref, *, nsteps):
    t_sub = S5_T
    nch = u_ref.shape[0] // t_sub
    p2 = win_ref.shape[1] // S5_GB
    half = p2 // 2
    xcat = jnp.concatenate([u_ref[pl.ds(s, nch, stride=t_sub), :].astype(BF16) for s in range(t_sub)],
                           axis=1)
    z = _dot(xcat, win_ref[...])
    pos = lax.broadcasted_iota(jnp.int32, (nch, p2), 0)
    lscan = lscan_ref[...]
    xprev = []
    for gg in range(S5_GB):
        cols = slice(gg * p2, (gg + 1) * p2)
        x = z[:, cols]
        for j in range(nsteps):
            d = 1 << j
            sh = jnp.where(pos >= d, pltpu.roll(x, d, axis=0), 0.0)
            x = (x + sh * lscan[j:j + 1, cols]
                 + pltpu.roll(sh, half, axis=1) * lscan[nsteps + j:nsteps + j + 1, cols])
        xprev.append(jnp.where(pos >= 1, pltpu.roll(x, 1, axis=0), 0.0).astype(BF16))
    y_state = _dot_nt(jnp.concatenate(xprev, axis=1), wo_ref[...])
    zero_blk = jnp.zeros((LANES, LANES), ktoep_ref.dtype)
    for t in range(0, t_sub, 2):
        taps = jnp.concatenate(
            [jnp.concatenate([ktoep_ref[(t_sub - 1 - t) * LANES:, :], zero_blk], axis=0),
             ktoep_ref[(t_sub - 2 - t) * LANES:, :]], axis=1)
        y_pair = y_state[:, t * LANES:(t + 2) * LANES] + _dot(xcat[:, :(t + 2) * LANES], taps)
        y_ref[pl.ds(t, nch, stride=t_sub), :] = y_pair[:, :LANES]
        y_ref[pl.ds(t + 1, nch, stride=t_sub), :] = y_pair[:, LANES:]


def s5_main(proj, col0, wd, ktoep, win, wo, lscan, *, nsteps, batch, seq):
    n = proj.shape[0]
    nblk = wd // LANES
    assert col0 % LANES == 0
    cb = col0 // LANES

    def wspec(a):
        return pl.BlockSpec((None,) + a.shape[1:], lambda i, b: (i, 0, 0))

    return pl.pallas_call(
        functools.partial(_s5_main_kernel, nsteps=nsteps),
        grid=(nblk, batch),
        in_specs=[pl.BlockSpec((seq, LANES), lambda i, b: (b, cb + i)),
                  wspec(ktoep), wspec(win), wspec(wo), wspec(lscan)],
        out_specs=pl.BlockSpec((seq, LANES), lambda i, b: (b, i)),
        out_shape=jax.ShapeDtypeStruct((n, wd), F32),
        compiler_params=_cparams(2),
        name="s5_main",
    )(proj, ktoep, win, wo, lscan)


def _s5_glu_kernel(y_ref, w_ref, b_ref, o_ref, wbf_ref):
    @pl.when(pl.program_id(0) == 0)
    def _():
        wbf_ref[...] = w_ref[...].astype(BF16)

    y = y_ref[...]
    z = 0.5 * y * (1.0 + jnp.tanh(math.sqrt(2.0 / math.pi) * (y + 0.044715 * (y * y * y))))
    gate = _dot(z.astype(BF16), wbf_ref[...]) + b_ref[...]
    o_ref[...] = (z * _sigmoid(gate)).astype(o_ref.dtype)


def s5_glu(y, w_stack, layer, b, tm=2048):
    n, wd = y.shape
    tm = min(tm, n)
    return pl.pallas_call(
        _s5_glu_kernel,
        grid=(n // tm,),
        in_specs=[pl.BlockSpec((tm, wd), lambda i: (i, 0)),
                  pl.BlockSpec((None, wd, wd), lambda i: (layer, 0, 0)),
                  pl.BlockSpec((1, wd), lambda i: (0, 0))],
        out_specs=pl.BlockSpec((tm, wd), lambda i: (i, 0)),
        out_shape=jax.ShapeDtypeStruct((n, wd), BF16),
        scratch_shapes=[pltpu.VMEM((wd, wd), BF16)],
        compiler_params=_cparams(1),
        name="s5_glu",
    )(y, w_stack, b)


def s5_mixer(proj, col0, a_re, a_im, log_dt, b_re, b_im, c_re, c_im, d_skip, glu_w_stack, layer, glu_b,
             *, batch, seq):
    wd = d_skip.shape[0]
    g, p = a_re.shape
    assert wd // g == S5_GROUP and seq % S5_T == 0 and wd % LANES == 0
    nsteps = _s5_scan_steps(seq // S5_T)
    per_row = lambda a: jnp.repeat(jnp.concatenate([a, a], axis=-1), S5_GROUP, axis=0)
    ldt_rows = jnp.broadcast_to(jnp.repeat(log_dt, S5_GROUP)[:, None], (wd, 2 * p))
    b2_rows = jnp.concatenate([b_re.transpose(0, 2, 1), b_im.transpose(0, 2, 1)], axis=-1).reshape(wd, 2 * p)
    c2_rows = jnp.concatenate([c_re, c_im], axis=-1).reshape(wd, 2 * p)
    ktoep, win, wo, lscan = s5_prep(per_row(a_re), per_row(a_im), ldt_rows, b2_rows, c2_rows,
                                    d_skip[None, :], nsteps)
    y = s5_main(proj, col0, wd, ktoep, win, wo, lscan, nsteps=nsteps, batch=batch, seq=seq)
    return s5_glu(y, glu_w_stack, layer, glu_b[None, :])


def _pad_heads(w, heads, width, new):
    r = w.shape[0]
    return jnp.pad(w.reshape(r, heads, width), ((0, 0), (0, 0), (0, new - width))).reshape(r, heads * new)


def kernel(x, c, w_ada, b_ada, ada_table, w_in, w_out, s5_a_re, s5_a_im, s5_log_dt, s5_b_re, s5_b_im, s5_c_re, s5_c_im, s5_d, s5_glu_w, s5_glu_b, hg_lb_raw, hg_norm_gain, hg_norm_bias, gla_w_gate, gla_b_gate, gla_norm_gain, w_ffn_gate, w_ffn_up, w_ffn_down, ln1_gain, ln1_bias, ln2_gain, ln2_bias):
    bsz, seq, d = x.shape
    depth = w_in.shape[0]
    n = bsz * seq
    s5_w = s5_d.shape[1]
    hg_w = hg_lb_raw.shape[1]
    gla_kw = gla_b_gate.shape[1]
    gla_vw = gla_norm_gain.shape[1]
    rank = gla_w_gate.shape[1]
    gla_dk = gla_kw // GLA_HEADS
    gla_dv = gla_vw // GLA_HEADS
    dk_pad = ((gla_dk + GLA_DK_PAD - 1) // GLA_DK_PAD) * GLA_DK_PAD
    assert dk_pad - gla_dk >= rank
    alpha = (2.0 * depth) ** 0.25

    rows = ((bsz + 7) // 8) * 8
    c_pad = jnp.pad(c, ((0, rows - bsz), (0, 0)))
    mod = cond_table(c_pad, w_ada, b_ada[None, :], ada_table.reshape(depth, N_MOD * d))
    mod = mod[:, :bsz].reshape(depth, bsz, N_MOD, d)

    o_u = 0
    o_hg = o_u + s5_w
    o_q = o_hg + 4 * hg_w
    o_k = o_q + gla_kw
    o_v = o_k + gla_kw
    o_lr = o_v + 2 * gla_vw

    hk = GLA_HEADS * dk_pad
    p_v = 0
    p_g = p_v + gla_vw
    p_hg = p_g + gla_vw
    p_u = p_hg + 4 * hg_w
    p_q = p_u + s5_w
    p_k = p_q + hk
    w_t = jnp.transpose(w_in, (0, 2, 1))
    lr_rows = jnp.broadcast_to(w_t[:, None, o_lr:o_lr + rank], (depth, GLA_HEADS, rank, d))
    q_rows = jnp.concatenate([w_t[:, o_q:o_k].reshape(depth, GLA_HEADS, gla_dk, d), lr_rows,
                              jnp.zeros((depth, GLA_HEADS, dk_pad - gla_dk - rank, d), F32)], axis=2)
    k_rows = jnp.pad(w_t[:, o_k:o_v].reshape(depth, GLA_HEADS, gla_dk, d),
                     ((0, 0), (0, 0), (0, dk_pad - gla_dk), (0, 0)))
    w_all = jnp.concatenate([w_t[:, o_v:o_lr], w_t[:, o_hg:o_q], w_t[:, o_u:o_hg],
                             q_rows.reshape(depth, hk, d), k_rows.reshape(depth, hk, d)], axis=1).astype(BF16)
    w_out_bf = w_out[0].astype(BF16)
    w_gate_bf = w_ffn_gate[0].astype(BF16)
    w_up_bf = w_ffn_up[0].astype(BF16)

    x2 = x.reshape(n, d)
    h = modulate(x2, mod[0], seq)
    for l in range(depth):
        proj = matmul_ws([h], w_all, l, F32, tm=1024, w_is_nk=True)
        y_a = s5_mixer(proj, p_u, s5_a_re[l], s5_a_im[l], s5_log_dt[l], s5_b_re[l],
                       s5_b_im[l], s5_c_re[l], s5_c_im[l], s5_d[l], s5_glu_w, l, s5_glu_b[l],
                       batch=bsz, seq=seq)
        y_b = hgrn2(proj, p_hg, hg_lb_raw, hg_norm_gain[l][None, :], hg_norm_bias[l][None, :],
                    layer=l, batch=bsz, seq=seq)
        w_gate = jnp.pad(
            jnp.pad(gla_w_gate[l].reshape(rank, GLA_HEADS, gla_dk), ((0, 0), (0, 0), (0, dk_pad - gla_dk))),
            ((gla_dk, dk_pad - gla_dk - rank), (0, 0), (0, 0))).reshape(dk_pad, hk).astype(BF16)
        b_gate = _pad_heads(gla_b_gate[l][None, :], GLA_HEADS, gla_dk, dk_pad)
        y_c = gla(proj, p_q, p_k, p_v, p_g, w_gate, b_gate, gla_norm_gain[l][None, :],
                  batch=bsz, seq=seq, dk=dk_pad, dv=gla_dv, q_scale=float(gla_dk) ** -0.5)
        mixed = matmul_ws([y_a, y_b, y_c], w_out_bf[None], 0, BF16, tm=1024)
        x2, h = ln_mod(x2, mixed, mod[l], mod[l], ln1_gain[l][None, :], ln1_bias[l][None, :], seq,
                       alpha=alpha, gate_row=2, next_row=3, with_h=True)

        last = l == depth - 1
        jobs = [(w_ffn_down, l)] + ([] if last else [(w_ffn_gate, l + 1), (w_ffn_up, l + 1), (w_out, l + 1)])
        act, casts = ffn_up(h, w_gate_bf, w_up_bf, jobs)
        if not last:
            w_gate_bf, w_up_bf, w_out_bf = casts[1], casts[2], casts[3]
        ffn = matmul_ws([act], casts[0][None], 0, BF16, tn=512)
        x2, h = ln_mod(x2, ffn, mod[l], mod[l if last else l + 1],
                       ln2_gain[l][None, :], ln2_bias[l][None, :], seq,
                       alpha=alpha, gate_row=5, next_row=0, with_h=not last)
    return x2.reshape(bsz, seq, d)
```

```python
import functools
import math

import numpy as np
import jax
import jax.numpy as jnp
from jax import lax
from jax.experimental import pallas as pl
from jax.experimental.pallas import tpu as pltpu

F32 = jnp.float32
BF16 = jnp.bfloat16

LANES = 128
SUBLANES = 8
BF16_ROWS = 16
LOG2_E = 1.0 / math.log(2.0)
V7X_VMEM_BYTES = 64 * 1024 * 1024
VMEM_COMPILER_RESERVE = 8 * 1024 * 1024
VMEM_LIMIT = V7X_VMEM_BYTES - VMEM_COMPILER_RESERVE

S5_GROUP = 16
HG_HEAD = 128
GLA_HEADS = 4
GLA_GATE_TEMP = 16.0
N_MOD = 6
LN_EPS = 1e-5

S5_T = 16
GLA_CHUNK = 128
GLA_DK_PAD = 256
FFN_SUB_ROWS = 512


def _cparams(n_axes):
    return pltpu.CompilerParams(
        dimension_semantics=("arbitrary",) * n_axes, vmem_limit_bytes=VMEM_LIMIT)


def _sigmoid(x):
    return 1.0 / (1.0 + jnp.exp(-x))


def _dot(a, b):
    return jnp.dot(a, b, preferred_element_type=F32)


def _dot_nt(a, b, precision=None):
    return lax.dot_general(a, b, (((1,), (1,)), ((), ())),
                           preferred_element_type=F32, precision=precision)


def _pick_tile(n, cap):
    best = None
    for t in range(LANES, min(n, cap) + 1, LANES):
        if n % t == 0:
            best = t
    assert best is not None, (n, cap)
    return best


def _mm_ws_kernel(*refs, n_a, w_is_nk):
    a_refs, w_ref, o_ref = refs[:n_a], refs[n_a], refs[n_a + 1]
    if w_is_nk:
        acc = _dot_nt(a_refs[0][...], w_ref[...])
    else:
        acc = None
        r0 = 0
        for a_ref in a_refs:
            k = a_ref.shape[1]
            part = _dot(a_ref[...], w_ref[r0:r0 + k, :])
            acc = part if acc is None else acc + part
            r0 += k
    o_ref[...] = acc.astype(o_ref.dtype)


def matmul_ws(a_list, w_stack, layer, out_dtype, tm=512, tn=1024, w_is_nk=False):
    m = a_list[0].shape[0]
    if w_is_nk:
        assert len(a_list) == 1
        _, n, k = w_stack.shape
    else:
        _, k, n = w_stack.shape
    assert sum(a.shape[1] for a in a_list) == k
    tn = _pick_tile(n, tn)
    tm = min(tm, m)
    if w_is_nk:
        w_spec = pl.BlockSpec((None, tn, k), lambda j, i: (layer, j, 0))
    else:
        w_spec = pl.BlockSpec((None, k, tn), lambda j, i: (layer, 0, j))
    return pl.pallas_call(
        functools.partial(_mm_ws_kernel, n_a=len(a_list), w_is_nk=w_is_nk),
        grid=(n // tn, m // tm),
        in_specs=[pl.BlockSpec((tm, a.shape[1]), lambda j, i: (i, 0)) for a in a_list] + [w_spec],
        out_specs=pl.BlockSpec((tm, tn), lambda j, i: (i, j)),
        out_shape=jax.ShapeDtypeStruct((m, n), out_dtype),
        compiler_params=_cparams(2),
        name="matmul_ws",
    )(*a_list, w_stack)


def _ffn_up_kernel(a_ref, wg_ref, wu_ref, *refs, n_side):
    side_in, o_ref, side_out = refs[:n_side], refs[n_side], refs[n_side + 1:]
    wg = wg_ref[...]
    wu = wu_ref[...]
    tm = a_ref.shape[0]
    sub = min(tm, FFN_SUB_ROWS)
    for r0 in range(0, tm, sub):
        a = a_ref[r0:r0 + sub, :]
        g = _dot(a, wg)
        u = _dot(a, wu)
        o_ref[r0:r0 + sub, :] = (g * _sigmoid(g) * u).astype(o_ref.dtype)
    for s_in, s_out in zip(side_in, side_out):
        s_out[...] = s_in[...].astype(s_out.dtype)


def _slab_specs(shape, layer, gi, gj):
    r, c = shape
    cw = -(-(-(-c // gj)) // LANES) * LANES
    if r % gi == 0 and (r // gi) % BF16_ROWS == 0 and -(-c // cw) == gj:
        blk = (r // gi, cw)
        return (pl.BlockSpec((None,) + blk, lambda i, j: (layer, i, j)), pl.BlockSpec(blk, lambda i, j: (i, j)))
    steps = gi * gj
    for rs in range(BF16_ROWS, r + 1, BF16_ROWS):
        if r % rs == 0 and r // rs <= steps:
            last = r // rs - 1
            blk = (rs, c)
            return (pl.BlockSpec((None,) + blk, lambda i, j: (layer, jnp.minimum(i * gj + j, last), 0)),
                    pl.BlockSpec(blk, lambda i, j: (jnp.minimum(i * gj + j, last), 0)))
    return None


def ffn_up(h, wg, wu, cast_jobs, tm=2048, tn=256):
    m, k = h.shape
    n = wg.shape[1]
    tm = min(tm, m)
    gi, gj = m // tm, pl.cdiv(n, tn)
    specs = [_slab_specs(w.shape[1:], layer, gi, gj) for w, layer in cast_jobs]
    riding = [job for job, sp in zip(cast_jobs, specs) if sp is not None]
    rspecs = [sp for sp in specs if sp is not None]
    wspec = pl.BlockSpec((k, tn), lambda i, j: (0, j))
    res = pl.pallas_call(
        functools.partial(_ffn_up_kernel, n_side=len(riding)),
        grid=(gi, gj),
        in_specs=[pl.BlockSpec((tm, k), lambda i, j: (i, 0)), wspec, wspec] + [sp[0] for sp in rspecs],
        out_specs=[pl.BlockSpec((tm, tn), lambda i, j: (i, j))] + [sp[1] for sp in rspecs],
        out_shape=[jax.ShapeDtypeStruct((m, n), BF16)]
        + [jax.ShapeDtypeStruct(w.shape[1:], BF16) for w, _ in riding],
        compiler_params=_cparams(2),
        name="ffn_up",
    )(h, wg, wu, *[w for w, _ in riding])
    casts, it = [], iter(res[1:])
    for (w, layer), sp in zip(cast_jobs, specs):
        casts.append(next(it) if sp is not None else w[layer].astype(BF16))
    return res[0], casts


def _cond_kernel(c_ref, w_ref, b_ref, tab_ref, o_ref):
    c = c_ref[...]
    act = (c * _sigmoid(c)).astype(BF16)
    cond = _dot(act, w_ref[...].astype(BF16)) + b_ref[...]
    for l in range(tab_ref.shape[0]):
        o_ref[l] = cond + tab_ref[l:l + 1, :]


def cond_table(c_pad, w_ada, b_ada, ada_table2, tn=1024):
    rows, d = c_pad.shape
    n = w_ada.shape[1]
    depth = ada_table2.shape[0]
    tn = _pick_tile(n, tn)
    return pl.pallas_call(
        _cond_kernel,
        grid=(n // tn,),
        in_specs=[pl.BlockSpec((rows, d), lambda j: (0, 0)),
                  pl.BlockSpec((d, tn), lambda j: (0, j)),
                  pl.BlockSpec((1, tn), lambda j: (0, j)),
                  pl.BlockSpec((depth, tn), lambda j: (0, j))],
        out_specs=pl.BlockSpec((depth, rows, tn), lambda j: (0, 0, j)),
        out_shape=jax.ShapeDtypeStruct((depth, rows, n), F32),
        compiler_params=_cparams(1),
        name="cond_table",
    )(c_pad, w_ada, b_ada, ada_table2)


def _modulate_kernel(x_ref, mod_ref, h_ref):
    m = mod_ref[0]
    h_ref[...] = (x_ref[...] * (1.0 + m[1:2, :]) + m[0:1, :]).astype(h_ref.dtype)


def modulate(x2, mod_l, seq, tm=512):
    n, d = x2.shape
    tm = min(tm, seq)
    per_b = seq // tm
    return pl.pallas_call(
        _modulate_kernel,
        grid=(n // tm,),
        in_specs=[pl.BlockSpec((tm, d), lambda i: (i, 0)),
                  pl.BlockSpec((1, N_MOD, d), lambda i: (i // per_b, 0, 0))],
        out_specs=pl.BlockSpec((tm, d), lambda i: (i, 0)),
        out_shape=jax.ShapeDtypeStruct((n, d), BF16),
        compiler_params=_cparams(1),
        name="modulate",
    )(x2, mod_l)


def _ln_mod_kernel(x_ref, mm_ref, mod_ref, nmod_ref, gain_ref, bias_ref, xo_ref, *h_refs,
                   alpha, gate_row, next_row):
    m = mod_ref[0]
    z = alpha * x_ref[...] + (1.0 + m[gate_row:gate_row + 1, :]) * mm_ref[...].astype(F32)
    mu = jnp.mean(z, axis=-1, keepdims=True)
    zc = z - mu
    var = jnp.mean(zc * zc, axis=-1, keepdims=True)
    y = zc * lax.rsqrt(var + LN_EPS) * gain_ref[...] + bias_ref[...]
    xo_ref[...] = y
    if h_refs:
        nm = nmod_ref[0]
        h_refs[0][...] = (y * (1.0 + nm[next_row + 1:next_row + 2, :])
                          + nm[next_row:next_row + 1, :]).astype(BF16)


def ln_mod(x2, mm, mod_l, mod_next, gain, bias, seq, *, alpha, gate_row, next_row, with_h, tm=256):
    n, d = x2.shape
    tm = min(tm, seq)
    per_b = seq // tm
    row = pl.BlockSpec((tm, d), lambda i: (i, 0))
    modspec = pl.BlockSpec((1, N_MOD, d), lambda i: (i // per_b, 0, 0))
    vec = pl.BlockSpec((1, d), lambda i: (0, 0))
    out_shape = [jax.ShapeDtypeStruct((n, d), F32)]
    out_specs = [row]
    if with_h:
        out_shape.append(jax.ShapeDtypeStruct((n, d), BF16))
        out_specs.append(row)
    res = pl.pallas_call(
        functools.partial(_ln_mod_kernel, alpha=alpha, gate_row=gate_row, next_row=next_row),
        grid=(n // tm,),
        in_specs=[row, row, modspec, modspec, vec, vec],
        out_specs=out_specs,
        out_shape=out_shape,
        compiler_params=_cparams(1),
        name="ln_mod",
    )(x2, mm, mod_l, mod_next, gain, bias)
    return (res[0], res[1]) if with_h else (res[0], None)


def _gla_levels(chunk):
    lv = []
    h = chunk // 2
    while h >= 1:
        lv.append(h)
        h //= 2
    return lv


def _gla_sum_matrix(chunk):
    blocks = []
    idx = np.arange(chunk)
    for h in _gla_levels(chunk):
        m = np.zeros((chunk, chunk), np.float32)
        for i in range(chunk):
            r = (i // (2 * h)) * 2 * h + h - 1
            if i % (2 * h) >= h:
                m[i, r + 1:i + 1] = 1.0
            else:
                m[i, i + 1:r + 1] = 1.0
        blocks.append(m)
    blocks.append((idx[None, :] <= idx[:, None]).astype(np.float32))
    m = np.concatenate(blocks, axis=0)
    return np.concatenate([m, m], axis=1)


def _gla_pair_masks(chunk):
    ngrp = chunk // SUBLANES
    col = lax.broadcasted_iota(jnp.int32, (SUBLANES, chunk), 1)
    rows = [lax.broadcasted_iota(jnp.int32, (SUBLANES, chunk), 0) + r * SUBLANES for r in range(ngrp)]
    diag = [row == col for row in rows]
    pair = []
    for h in _gla_levels(chunk):
        blk = 2 * h
        pair.append([((row // blk) == (col // blk)) & ((row % blk) >= h) & ((col % blk) < h) for row in rows])
    rid = lax.broadcasted_iota(jnp.int32, (chunk, 1), 0)
    right = [(rid % (2 * h)) >= h for h in _gla_levels(chunk)]
    return diag, pair, right


def _row_groups(x, groups):
    parts, start, prev = [], None, None
    for r in groups:
        if start is None:
            start = r
        elif r != prev + 1:
            parts.append(x[start * SUBLANES:(prev + 1) * SUBLANES])
            start = r
        prev = r
    parts.append(x[start * SUBLANES:(prev + 1) * SUBLANES])
    return parts


def _gla_chunk_heads(q, k, v, g, s_ref, msum, masks, chunk, dk, dv, heads):
    levels = _gla_levels(chunk)
    nl = len(levels)
    ngrp = chunk // SUBLANES
    g2 = g * LOG2_E
    g_hi = g2.astype(BF16)
    g_lo = (g2 - g_hi.astype(F32)).astype(BF16)
    expo = _dot(msum, jnp.concatenate([g_hi, g_lo], axis=0))
    e_lv = jnp.exp2(expo[:nl * chunk, :])
    b_cum = expo[nl * chunk:, :]
    e_cum_all = jnp.exp2(b_cum)
    e_rev_all = jnp.exp2(b_cum[chunk - 1:chunk, :] - b_cum)

    diag, pair, right = masks
    outs = []
    for hd in range(heads):
        ks = slice(hd * dk, (hd + 1) * dk)
        qh, kh = q[:, ks], k[:, ks]
        vh = v[:, hd * dv:(hd + 1) * dv]
        dsum = jnp.sum(qh * kh, axis=-1, keepdims=True)
        a_rows = [jnp.where(diag[r], dsum[r * SUBLANES:(r + 1) * SUBLANES], 0.0) for r in range(ngrp)]
        for li, h in enumerate(levels):
            e = e_lv[li * chunk:(li + 1) * chunk, ks]
            if h >= SUBLANES:
                rgt = [r for r in range(ngrp) if (r * SUBLANES) % (2 * h) >= h]
                runs, cur = [], None
                for r in range(ngrp):
                    src = r in rgt
                    if cur is None or cur[0] != src:
                        cur = [src, r, r]
                        runs.append(cur)
                    else:
                        cur[2] = r
                sel = jnp.concatenate([(qh if src else kh)[a * SUBLANES:(b + 1) * SUBLANES]
                                       for src, a, b in runs], axis=0)
                prod = sel * e
                lhs = jnp.concatenate(_row_groups(prod, rgt), axis=0).astype(BF16)
                p = _dot_nt(lhs, prod.astype(BF16))
                for n_, r in enumerate(rgt):
                    a_rows[r] = jnp.where(pair[li][r], p[n_ * SUBLANES:(n_ + 1) * SUBLANES], a_rows[r])
            else:
                xb = (jnp.where(right[li], qh, kh) * e).astype(BF16)
                p = _dot_nt(xb, xb)
                a_rows = [jnp.where(pair[li][r], p[r * SUBLANES:(r + 1) * SUBLANES], a_rows[r])
                          for r in range(ngrp)]
        a = jnp.concatenate(a_rows, axis=0)
        e_cum = e_cum_all[:, ks]
        s_t = s_ref[hd]
        vt = vh.T.astype(BF16)
        o = _dot_nt(jnp.concatenate([a.astype(BF16), (qh * e_cum).astype(BF16)], axis=1),
                    jnp.concatenate([vt, s_t.astype(BF16)], axis=1))
        kb = (kh * e_rev_all[:, ks]).astype(BF16)
        s_ref[hd] = s_t * e_cum[chunk - 1:chunk, :] + _dot(vt, kb)
        outs.append(o)
    return outs


def _hgrn_kernel(q_ref, f_ref, i_ref, g_ref, lbraw_ref, gain_ref, bias_ref, msum_ref,
                 o_ref, s_ref, *, layer, heads, chunk):
    dk = dv = HG_HEAD
    tb = q_ref.shape[0]

    @pl.when(pl.program_id(2) == 0)
    def _():
        s_ref[...] = jnp.zeros_like(s_ref)

    raw = lbraw_ref[...]
    ex = jnp.exp(raw - jnp.max(raw, axis=0, keepdims=True))
    sm = ex / jnp.sum(ex, axis=0, keepdims=True)
    lb = jnp.zeros_like(sm[0:1, :])
    for m in range(1, layer + 1):
        lb = lb + sm[m:m + 1, :]
    gain = gain_ref[...]
    bias = bias_ref[...]
    msum = msum_ref[...]
    masks = _gla_pair_masks(chunk)

    def body(c, carry):
        rows = pl.ds(c * chunk, chunk)
        q_in = q_ref[rows, :]
        f = lb + (1.0 - lb) * _sigmoid(f_ref[rows, :])
        q = q_in * _sigmoid(q_in)
        outs = _gla_chunk_heads(q, 1.0 - f, i_ref[rows, :], jnp.log(f), s_ref, msum, masks,
                                chunk, dk, dv, heads)
        gate = _sigmoid(g_ref[rows, :])
        for hd, o in enumerate(outs):
            cs = slice(hd * dv, (hd + 1) * dv)
            y = gate[:, cs] * o
            mu = jnp.mean(y, axis=-1, keepdims=True)
            yc = y - mu
            var = jnp.mean(yc * yc, axis=-1, keepdims=True)
            yn = yc * lax.rsqrt(var + LN_EPS)
            o_ref[rows, cs] = (yn * gain[:, cs] + bias[:, cs]).astype(o_ref.dtype)
        return carry

    for c in range(tb // chunk):
        body(c, 0)


def hgrn2(proj, col0, lb_raw, gain, bias, *, layer, batch, seq, heads_per_block=6, tb=1024):
    n = proj.shape[0]
    w = lb_raw.shape[1]
    nheads = w // HG_HEAD
    hpb = heads_per_block
    while nheads % hpb:
        hpb -= 1
    bw = hpb * HG_HEAD
    ngrp = nheads // hpb
    assert col0 % bw == 0
    cb = col0 // bw
    tb = min(tb, seq)
    chunk = min(GLA_CHUNK, tb)
    nt = seq // tb
    msum = jnp.asarray(_gla_sum_matrix(chunk), BF16)

    def sec(s):
        return pl.BlockSpec((tb, bw), lambda b, hg, t: (b * nt + t, cb + s * ngrp + hg))

    vec = pl.BlockSpec((1, bw), lambda b, hg, t: (0, hg))
    return pl.pallas_call(
        functools.partial(_hgrn_kernel, layer=layer, heads=hpb, chunk=chunk),
        grid=(batch, ngrp, nt),
        in_specs=[sec(0), sec(1), sec(2), sec(3),
                  pl.BlockSpec((lb_raw.shape[0], bw), lambda b, hg, t: (0, hg)),
                  vec, vec,
                  pl.BlockSpec(msum.shape, lambda b, hg, t: (0, 0))],
        out_specs=pl.BlockSpec((tb, bw), lambda b, hg, t: (b * nt + t, hg)),
        out_shape=jax.ShapeDtypeStruct((n, w), BF16),
        scratch_shapes=[pltpu.VMEM((hpb, HG_HEAD, HG_HEAD), F32)],
        compiler_params=_cparams(3),
        name="hgrn2",
    )(proj, proj, proj, proj, lb_raw, gain, bias, msum)


def _gla_kernel(q_ref, k_ref, v_ref, g_ref, wg_ref, bg_ref, gain_ref, msum_ref,
                o_ref, s_ref, *, heads, chunk, dk, dv, q_scale):
    tb = q_ref.shape[0]

    @pl.when(pl.program_id(2) == 0)
    def _():
        s_ref[...] = jnp.zeros_like(s_ref)

    wg = wg_ref[...]
    bg = bg_ref[...]
    gain = gain_ref[...]
    msum = msum_ref[...]
    masks = _gla_pair_masks(chunk)

    def body(c, carry):
        rows = pl.ds(c * chunk, chunk)
        q_raw = q_ref[rows, :]
        q_bf = q_raw.astype(BF16)
        pre = jnp.concatenate([_dot(q_bf[:, hd * dk:(hd + 1) * dk], wg[:, hd * dk:(hd + 1) * dk])
                               for hd in range(heads)], axis=1) + bg
        log_a = (jnp.minimum(pre, 0.0) - jnp.log(1.0 + jnp.exp(-jnp.abs(pre)))) / GLA_GATE_TEMP
        outs = _gla_chunk_heads(q_raw * q_scale, k_ref[rows, :], v_ref[rows, :], log_a,
                                s_ref, msum, masks, chunk, dk, dv, heads)
        g_in = g_ref[rows, :]
        swish = g_in * _sigmoid(g_in)
        for hd, o in enumerate(outs):
            cs = slice(hd * dv, (hd + 1) * dv)
            y = o * lax.rsqrt(jnp.mean(o * o, axis=-1, keepdims=True) + LN_EPS)
            o_ref[rows, cs] = (y * gain[:, cs] * swish[:, cs]).astype(o_ref.dtype)
        return carry

    for c in range(tb // chunk):
        body(c, 0)


def gla(proj, q0, k0, v0, g0, w_gate, b_gate, gain, *, batch, seq, dk, dv, q_scale,
        heads_per_block=4, tb=512):
    n = proj.shape[0]
    nheads = w_gate.shape[1] // dk
    hpb = min(heads_per_block, nheads)
    ngrp = nheads // hpb
    tb = min(tb, seq)
    chunk = min(GLA_CHUNK, tb)
    nt = seq // tb
    msum = jnp.asarray(_gla_sum_matrix(chunk), BF16)
    kw, vw = hpb * dk, hpb * dv
    assert q0 % kw == 0 and k0 % kw == 0 and v0 % vw == 0 and g0 % vw == 0

    def rows(width, col0):
        cb = col0 // width
        return pl.BlockSpec((tb, width), lambda b, hg, t: (b * nt + t, cb + hg))

    return pl.pallas_call(
        functools.partial(_gla_kernel, heads=hpb, chunk=chunk, dk=dk, dv=dv, q_scale=q_scale),
        grid=(batch, ngrp, nt),
        in_specs=[rows(kw, q0), rows(kw, k0), rows(vw, v0), rows(vw, g0),
                  pl.BlockSpec((w_gate.shape[0], kw), lambda b, hg, t: (0, hg)),
                  pl.BlockSpec((1, kw), lambda b, hg, t: (0, hg)),
                  pl.BlockSpec((1, vw), lambda b, hg, t: (0, hg)),
                  pl.BlockSpec(msum.shape, lambda b, hg, t: (0, 0))],
        out_specs=pl.BlockSpec((tb, vw), lambda b, hg, t: (b * nt + t, hg)),
        out_shape=jax.ShapeDtypeStruct((n, nheads * dv), BF16),
        scratch_shapes=[pltpu.VMEM((hpb, dv, dk), F32)],
        compiler_params=_cparams(3),
        name="gla",
    )(proj, proj, proj, proj, w_gate, b_gate, gain, msum)


S5_GB = LANES // S5_GROUP


def _s5_scan_steps(nchunks):
    return max(1, int(math.ceil(math.log2(nchunks)))) if nchunks > 1 else 0


def _s5_prep_kernel(ar_ref, ai_ref, ldt_ref, b2_ref, c2_ref, d_ref,
                    ktoep_ref, win_ref, wo_ref, lscan_ref, *, nsteps):
    rows, p2 = ar_ref.shape
    half = p2 // 2
    t_sub = S5_T
    ar = ar_ref[...]
    ai = ai_ref[...]
    dt = jnp.exp(ldt_ref[...])
    lane = lax.broadcasted_iota(jnp.int32, (1, p2), 1)
    sgn_im = jnp.where(lane < half, -1.0, 1.0)
    sgn_re = -sgn_im

    def lam_pow(k):
        mag = jnp.exp(float(k) * (ar * dt))
        th = float(k) * (ai * dt)
        return mag * jnp.cos(th), mag * jnp.sin(th)

    pows = [lam_pow(k) for k in range(t_sub + 1)]

    def cmul(x, k):
        l_re, l_im = pows[k]
        return x * l_re + pltpu.roll(x, half, axis=1) * (l_im * sgn_im)

    lam_re, lam_im = pows[1]
    den = ar * ar + ai * ai
    nr = lam_re - 1.0
    ni = lam_im
    coef_re = (nr * ar + ni * ai) / den
    coef_im = (ni * ar - nr * ai) / den
    b2 = b2_ref[...]
    bbar = b2 * coef_re + pltpu.roll(b2, half, axis=1) * (coef_im * sgn_im)
    c2 = c2_ref[...]

    rgrp = lax.broadcasted_iota(jnp.int32, (rows, rows), 0) // S5_GROUP
    cgrp = lax.broadcasted_iota(jnp.int32, (rows, rows), 1) // S5_GROUP
    same_grp = rgrp == cgrp
    r_i = lax.broadcasted_iota(jnp.int32, (rows, rows), 0)
    c_i = lax.broadcasted_iota(jnp.int32, (rows, rows), 1)
    hp = lax.Precision.HIGHEST

    for j in range(t_sub):
        k = t_sub - 1 - j
        tap = jnp.where(same_grp, _dot_nt(cmul(bbar, k) * sgn_re, c2, hp), 0.0)
        if k == 0:
            tap = tap + jnp.where(r_i == c_i, d_ref[...], 0.0)
        ktoep_ref[j * rows:(j + 1) * rows, :] = tap.astype(ktoep_ref.dtype)

    grp_of_row = lax.broadcasted_iota(jnp.int32, (rows, p2), 0) // S5_GROUP

    def block_diag(tile):
        return jnp.concatenate([jnp.where(grp_of_row == gg, tile, 0.0) for gg in range(S5_GB)], axis=1)

    for s in range(t_sub):
        win_ref[s * rows:(s + 1) * rows, :] = block_diag(cmul(bbar, t_sub - 1 - s)).astype(win_ref.dtype)
        wo_ref[s * rows:(s + 1) * rows, :] = block_diag(cmul(c2, s + 1) * sgn_re).astype(wo_ref.dtype)

    def group_rows(tile):
        return jnp.concatenate([tile[gg * S5_GROUP:gg * S5_GROUP + 1, :] for gg in range(S5_GB)], axis=1)

    cur_re, cur_im = pows[t_sub]
    rows_re, rows_sw = [], []
    for _ in range(nsteps):
        rows_re.append(group_rows(cur_re))
        rows_sw.append(group_rows(cur_im * sgn_im))
        cur_re, cur_im = cur_re * cur_re - cur_im * cur_im, 2.0 * cur_re * cur_im
    pad = lscan_ref.shape[0] - 2 * nsteps
    parts = rows_re + rows_sw + ([jnp.zeros((pad, S5_GB * p2), F32)] if pad else [])
    lscan_ref[...] = jnp.concatenate(parts, axis=0)


def s5_prep(ar_rows, ai_rows, ldt_rows, b2_rows, c2_rows, d_row, nsteps):
    wd, p2 = ar_rows.shape
    nblk = wd // LANES
    lrows = ((2 * nsteps + 7) // 8) * 8
    tile = pl.BlockSpec((LANES, p2), lambda i: (i, 0))
    return pl.pallas_call(
        functools.partial(_s5_prep_kernel, nsteps=nsteps),
        grid=(nblk,),
        in_specs=[tile, tile, tile, tile, tile, pl.BlockSpec((1, LANES), lambda i: (0, i))],
        out_specs=[pl.BlockSpec((None, S5_T * LANES, LANES), lambda i: (i, 0, 0)),
                   pl.BlockSpec((None, S5_T * LANES, S5_GB * p2), lambda i: (i, 0, 0)),
                   pl.BlockSpec((None, S5_T * LANES, S5_GB * p2), lambda i: (i, 0, 0)),
                   pl.BlockSpec((None, lrows, S5_GB * p2), lambda i: (i, 0, 0))],
        out_shape=[jax.ShapeDtypeStruct((nblk, S5_T * LANES, LANES), BF16),
                   jax.ShapeDtypeStruct((nblk, S5_T * LANES, S5_GB * p2), BF16),
                   jax.ShapeDtypeStruct((nblk, S5_T * LANES, S5_GB * p2), BF16),
                   jax.ShapeDtypeStruct((nblk, lrows, S5_GB * p2), F32)],
        compiler_params=_cparams(1),
        name="s5_prep",
    )(ar_rows, ai_rows, ldt_rows, b2_rows, c2_rows, d_row)


def _s5_main_kernel(u_ref, ktoep_ref, win_ref, wo_ref, lscan_ref, y_ref, *, nsteps):
    t_sub = S5_T
    nch = u_ref.shape[0] // t_sub
    p2 = win_ref.shape[1] // S5_GB
    half = p2 // 2
    xcat = jnp.concatenate([u_ref[pl.ds(s, nch, stride=t_sub), :].astype(BF16) for s in range(t_sub)],
                           axis=1)
    z = _dot(xcat, win_ref[...])
    pos = lax.broadcasted_iota(jnp.int32, (nch, p2), 0)
    lscan = lscan_ref[...]
    xprev = []
    for gg in range(S5_GB):
        cols = slice(gg * p2, (gg + 1) * p2)
        x = z[:, cols]
        for j in range(nsteps):
            d = 1 << j
            sh = jnp.where(pos >= d, pltpu.roll(x, d, axis=0), 0.0)
            x = (x + sh * lscan[j:j + 1, cols]
                 + pltpu.roll(sh, half, axis=1) * lscan[nsteps + j:nsteps + j + 1, cols])
        xprev.append(jnp.where(pos >= 1, pltpu.roll(x, 1, axis=0), 0.0).astype(BF16))
    y_state = _dot_nt(jnp.concatenate(xprev, axis=1), wo_ref[...])
    zero_blk = jnp.zeros((LANES, LANES), ktoep_ref.dtype)
    for t in range(0, t_sub, 2):
        taps = jnp.concatenate(
            [jnp.concatenate([ktoep_ref[(t_sub - 1 - t) * LANES:, :], zero_blk], axis=0),
             ktoep_ref[(t_sub - 2 - t) * LANES:, :]], axis=1)
        y_pair = y_state[:, t * LANES:(t + 2) * LANES] + _dot(xcat[:, :(t + 2) * LANES], taps)
        y_ref[pl.ds(t, nch, stride=t_sub), :] = y_pair[:, :LANES]
        y_ref[pl.ds(t + 1, nch, stride=t_sub), :] = y_pair[:, LANES:]


def s5_main(proj, col0, wd, ktoep, win, wo, lscan, *, nsteps, batch, seq):
    n = proj.shape[0]
    nblk = wd // LANES
    assert col0 % LANES == 0
    cb = col0 // LANES

    def wspec(a):
        return pl.BlockSpec((None,) + a.shape[1:], lambda i, b: (i, 0, 0))

    return pl.pallas_call(
        functools.partial(_s5_main_kernel, nsteps=nsteps),
        grid=(nblk, batch),
        in_specs=[pl.BlockSpec((seq, LANES), lambda i, b: (b, cb + i)),
                  wspec(ktoep), wspec(win), wspec(wo), wspec(lscan)],
        out_specs=pl.BlockSpec((seq, LANES), lambda i, b: (b, i)),
        out_shape=jax.ShapeDtypeStruct((n, wd), F32),
        compiler_params=_cparams(2),
        name="s5_main",
    )(proj, ktoep, win, wo, lscan)


def _s5_glu_kernel(y_ref, w_ref, b_ref, o_ref, wbf_ref):
    @pl.when(pl.program_id(0) == 0)
    def _():
        wbf_ref[...] = w_ref[...].astype(BF16)

    y = y_ref[...]
    z = 0.5 * y * (1.0 + jnp.tanh(math.sqrt(2.0 / math.pi) * (y + 0.044715 * (y * y * y))))
    gate = _dot(z.astype(BF16), wbf_ref[...]) + b_ref[...]
    o_ref[...] = (z * _sigmoid(gate)).astype(o_ref.dtype)


def s5_glu(y, w_stack, layer, b, tm=2048):
    n, wd = y.shape
    tm = min(tm, n)
    return pl.pallas_call(
        _s5_glu_kernel,
        grid=(n // tm,),
        in_specs=[pl.BlockSpec((tm, wd), lambda i: (i, 0)),
                  pl.BlockSpec((None, wd, wd), lambda i: (layer, 0, 0)),
                  pl.BlockSpec((1, wd), lambda i: (0, 0))],
        out_specs=pl.BlockSpec((tm, wd), lambda i: (i, 0)),
        out_shape=jax.ShapeDtypeStruct((n, wd), BF16),
        scratch_shapes=[pltpu.VMEM((wd, wd), BF16)],
        compiler_params=_cparams(1),
        name="s5_glu",
    )(y, w_stack, b)


def s5_mixer(proj, col0, a_re, a_im, log_dt, b_re, b_im, c_re, c_im, d_skip, glu_w_stack, layer, glu_b,
             *, batch, seq):
    wd = d_skip.shape[0]
    g, p = a_re.shape
    assert wd // g == S5_GROUP and seq % S5_T == 0 and wd % LANES == 0
    nsteps = _s5_scan_steps(seq // S5_T)
    per_row = lambda a: jnp.repeat(jnp.concatenate([a, a], axis=-1), S5_GROUP, axis=0)
    ldt_rows = jnp.broadcast_to(jnp.repeat(log_dt, S5_GROUP)[:, None], (wd, 2 * p))
    b2_rows = jnp.concatenate([b_re.transpose(0, 2, 1), b_im.transpose(0, 2, 1)], axis=-1).reshape(wd, 2 * p)
    c2_rows = jnp.concatenate([c_re, c_im], axis=-1).reshape(wd, 2 * p)
    ktoep, win, wo, lscan = s5_prep(per_row(a_re), per_row(a_im), ldt_rows, b2_rows, c2_rows,
                                    d_skip[None, :], nsteps)
    y = s5_main(proj, col0, wd, ktoep, win, wo, lscan, nsteps=nsteps, batch=batch, seq=seq)
    return s5_glu(y, glu_w_stack, layer, glu_b[None, :])


def _pad_heads(w, heads, width, new):
    r = w.shape[0]
    return jnp.pad(w.reshape(r, heads, width), ((0, 0), (0, 0), (0, new - width))).reshape(r, heads * new)


def kernel(x, c, w_ada, b_ada, ada_table, w_in, w_out, s5_a_re, s5_a_im, s5_log_dt, s5_b_re, s5_b_im, s5_c_re, s5_c_im, s5_d, s5_glu_w, s5_glu_b, hg_lb_raw, hg_norm_gain, hg_norm_bias, gla_w_gate, gla_b_gate, gla_norm_gain, w_ffn_gate, w_ffn_up, w_ffn_down, ln1_gain, ln1_bias, ln2_gain, ln2_bias):
    bsz, seq, d = x.shape
    depth = w_in.shape[0]
    n = bsz * seq
    s5_w = s5_d.shape[1]
    hg_w = hg_lb_raw.shape[1]
    gla_kw = gla_b_gate.shape[1]
    gla_vw = gla_norm_gain.shape[1]
    rank = gla_w_gate.shape[1]
    gla_dk = gla_kw // GLA_HEADS
    gla_dv = gla_vw // GLA_HEADS
    dk_pad = ((gla_dk + GLA_DK_PAD - 1) // GLA_DK_PAD) * GLA_DK_PAD
    assert dk_pad - gla_dk >= rank
    alpha = (2.0 * depth) ** 0.25

    rows = ((bsz + 7) // 8) * 8
    c_pad = jnp.pad(c, ((0, rows - bsz), (0, 0)))
    mod = cond_table(c_pad, w_ada, b_ada[None, :], ada_table.reshape(depth, N_MOD * d))
    mod = mod[:, :bsz].reshape(depth, bsz, N_MOD, d)

    o_u = 0
    o_hg = o_u + s5_w
    o_q = o_hg + 4 * hg_w
    o_k = o_q + gla_kw
    o_v = o_k + gla_kw
    o_lr = o_v + 2 * gla_vw

    hk = GLA_HEADS * dk_pad
    p_v = 0
    p_g = p_v + gla_vw
    p_hg = p_g + gla_vw
    p_u = p_hg + 4 * hg_w
    p_q = p_u + s5_w
    p_k = p_q + hk
    w_t = jnp.transpose(w_in, (0, 2, 1))
    lr_rows = jnp.broadcast_to(w_t[:, None, o_lr:o_lr + rank], (depth, GLA_HEADS, rank, d))
    q_rows = jnp.concatenate([w_t[:, o_q:o_k].reshape(depth, GLA_HEADS, gla_dk, d), lr_rows,
                              jnp.zeros((depth, GLA_HEADS, dk_pad - gla_dk - rank, d), F32)], axis=2)
    k_rows = jnp.pad(w_t[:, o_k:o_v].reshape(depth, GLA_HEADS, gla_dk, d),
                     ((0, 0), (0, 0), (0, dk_pad - gla_dk), (0, 0)))
    w_all = jnp.concatenate([w_t[:, o_v:o_lr], w_t[:, o_hg:o_q], w_t[:, o_u:o_hg],
                             q_rows.reshape(depth, hk, d), k_rows.reshape(depth, hk, d)], axis=1).astype(BF16)
    w_out_bf = w_out[0].astype(BF16)
    w_gate_bf = w_ffn_gate[0].astype(BF16)
    w_up_bf = w_ffn_up[0].astype(BF16)

    x2 = x.reshape(n, d)
    h = modulate(x2, mod[0], seq)
    for l in range(depth):
        proj = matmul_ws([h], w_all, l, F32, tm=1024, w_is_nk=True)
        y_a = s5_mixer(proj, p_u, s5_a_re[l], s5_a_im[l], s5_log_dt[l], s5_b_re[l],
                       s5_b_im[l], s5_c_re[l], s5_c_im[l], s5_d[l], s5_glu_w, l, s5_glu_b[l],
                       batch=bsz, seq=seq)
        y_b = hgrn2(proj, p_hg, hg_lb_raw, hg_norm_gain[l][None, :], hg_norm_bias[l][None, :],
                    layer=l, batch=bsz, seq=seq)
        w_gate = jnp.pad(
            jnp.pad(gla_w_gate[l].reshape(rank, GLA_HEADS, gla_dk), ((0, 0), (0, 0), (0, dk_pad - gla_dk))),
            ((gla_dk, dk_pad - gla_dk - rank), (0, 0), (0, 0))).reshape(dk_pad, hk).astype(BF16)
        b_gate = _pad_heads(gla_b_gate[l][None, :], GLA_HEADS, gla_dk, dk_pad)
        y_c = gla(proj, p_q, p_k, p_v, p_g, w_gate, b_gate, gla_norm_gain[l][None, :],
                  batch=bsz, seq=seq, dk=dk_pad, dv=gla_dv, q_scale=float(gla_dk) ** -0.5)
        mixed = matmul_ws([y_a, y_b, y_c], w_out_bf[None], 0, BF16, tm=1024)
        x2, h = ln_mod(x2, mixed, mod[l], mod[l], ln1_gain[l][None, :], ln1_bias[l][None, :], seq,
                       alpha=alpha, gate_row=2, next_row=3, with_h=True)

        last = l == depth - 1
        jobs = [(w_ffn_down, l)] + ([] if last else [(w_ffn_gate, l + 1), (w_ffn_up, l + 1), (w_out, l + 1)])
        act, casts = ffn_up(h, w_gate_bf, w_up_bf, jobs)
        if not last:
            w_gate_bf, w_up_bf, w_out_bf = casts[1], casts[2], casts[3]
        ffn = matmul_ws([act], casts[0][None], 0, BF16, tn=512)
        x2, h = ln_mod(x2, ffn, mod[l], mod[l if last else l + 1],
                       ln2_gain[l][None, :], ln2_bias[l][None, :], seq,
                       alpha=alpha, gate_row=5, next_row=0, with_h=not last)
    return x2.reshape(bsz, seq, d)
```

```python
import functools
import math

import numpy as np
import jax
import jax.numpy as jnp
from jax import lax
from jax.experimental import pallas as pl
from jax.experimental.pallas import tpu as pltpu

F32 = jnp.float32
BF16 = jnp.bfloat16

LANES = 128
SUBLANES = 8
BF16_ROWS = 16
LOG2_E = 1.0 / math.log(2.0)
V7X_VMEM_BYTES = 64 * 1024 * 1024
VMEM_COMPILER_RESERVE = 8 * 1024 * 1024
VMEM_LIMIT = V7X_VMEM_BYTES - VMEM_COMPILER_RESERVE

S5_GROUP = 16
HG_HEAD = 128
GLA_HEADS = 4
GLA_GATE_TEMP = 16.0
N_MOD = 6
LN_EPS = 1e-5

S5_T = 16
GLA_CHUNK = 128
GLA_DK_PAD = 256
FFN_SUB_ROWS = 512


def _cparams(n_axes):
    return pltpu.CompilerParams(
        dimension_semantics=("arbitrary",) * n_axes, vmem_limit_bytes=VMEM_LIMIT)


def _sigmoid(x):
    return 1.0 / (1.0 + jnp.exp(-x))


def _dot(a, b):
    return jnp.dot(a, b, preferred_element_type=F32)


def _dot_nt(a, b, precision=None):
    return lax.dot_general(a, b, (((1,), (1,)), ((), ())),
                           preferred_element_type=F32, precision=precision)


def _pick_tile(n, cap):
    best = None
    for t in range(LANES, min(n, cap) + 1, LANES):
        if n % t == 0:
            best = t
    assert best is not None, (n, cap)
    return best


def _mm_ws_kernel(*refs, n_a, w_is_nk):
    a_refs, w_ref, o_ref = refs[:n_a], refs[n_a], refs[n_a + 1]
    if w_is_nk:
        acc = _dot_nt(a_refs[0][...], w_ref[...])
    else:
        acc = None
        r0 = 0
        for a_ref in a_refs:
            k = a_ref.shape[1]
            part = _dot(a_ref[...], w_ref[r0:r0 + k, :])
            acc = part if acc is None else acc + part
            r0 += k
    o_ref[...] = acc.astype(o_ref.dtype)


def matmul_ws(a_list, w_stack, layer, out_dtype, tm=512, tn=1024, w_is_nk=False, w_buffers=2):
    m = a_list[0].shape[0]
    if w_is_nk:
        assert len(a_list) == 1
        _, n, k = w_stack.shape
    else:
        _, k, n = w_stack.shape
    assert sum(a.shape[1] for a in a_list) == k
    tn = _pick_tile(n, tn)
    tm = min(tm, m)
    mode = pl.Buffered(w_buffers)
    if w_is_nk:
        w_spec = pl.BlockSpec((None, tn, k), lambda j, i: (layer, j, 0), pipeline_mode=mode)
    else:
        w_spec = pl.BlockSpec((None, k, tn), lambda j, i: (layer, 0, j), pipeline_mode=mode)
    return pl.pallas_call(
        functools.partial(_mm_ws_kernel, n_a=len(a_list), w_is_nk=w_is_nk),
        grid=(n // tn, m // tm),
        in_specs=[pl.BlockSpec((tm, a.shape[1]), lambda j, i: (i, 0)) for a in a_list] + [w_spec],
        out_specs=pl.BlockSpec((tm, tn), lambda j, i: (i, j)),
        out_shape=jax.ShapeDtypeStruct((m, n), out_dtype),
        compiler_params=_cparams(2),
        name="matmul_ws",
    )(*a_list, w_stack)


def _ffn_up_kernel(a_ref, wg_ref, wu_ref, *refs, n_side):
    side_in, o_ref, side_out = refs[:n_side], refs[n_side], refs[n_side + 1:]
    wg = wg_ref[...]
    wu = wu_ref[...]
    tm = a_ref.shape[0]
    sub = min(tm, FFN_SUB_ROWS)
    for r0 in range(0, tm, sub):
        a = a_ref[r0:r0 + sub, :]
        g = _dot(a, wg)
        u = _dot(a, wu)
        o_ref[r0:r0 + sub, :] = (g * _sigmoid(g) * u).astype(o_ref.dtype)
    for s_in, s_out in zip(side_in, side_out):
        s_out[...] = s_in[...].astype(s_out.dtype)


def _slab_specs(shape, layer, gi, gj):
    r, c = shape
    cw = -(-(-(-c // gj)) // LANES) * LANES
    if r % gi == 0 and (r // gi) % BF16_ROWS == 0 and -(-c // cw) == gj:
        blk = (r // gi, cw)
        return (pl.BlockSpec((None,) + blk, lambda i, j: (layer, i, j)), pl.BlockSpec(blk, lambda i, j: (i, j)))
    steps = gi * gj
    for rs in range(BF16_ROWS, r + 1, BF16_ROWS):
        if r % rs == 0 and r // rs <= steps:
            last = r // rs - 1
            blk = (rs, c)
            return (pl.BlockSpec((None,) + blk, lambda i, j: (layer, jnp.minimum(i * gj + j, last), 0)),
                    pl.BlockSpec(blk, lambda i, j: (jnp.minimum(i * gj + j, last), 0)))
    return None


def ffn_up(h, wg, wu, cast_jobs, tm=2048, tn=256):
    m, k = h.shape
    n = wg.shape[1]
    tm = min(tm, m)
    gi, gj = m // tm, pl.cdiv(n, tn)
    specs = [_slab_specs(w.shape[1:], layer, gi, gj) for w, layer in cast_jobs]
    riding = [job for job, sp in zip(cast_jobs, specs) if sp is not None]
    rspecs = [sp for sp in specs if sp is not None]
    wspec = pl.BlockSpec((k, tn), lambda i, j: (0, j))
    res = pl.pallas_call(
        functools.partial(_ffn_up_kernel, n_side=len(riding)),
        grid=(gi, gj),
        in_specs=[pl.BlockSpec((tm, k), lambda i, j: (i, 0)), wspec, wspec] + [sp[0] for sp in rspecs],
        out_specs=[pl.BlockSpec((tm, tn), lambda i, j: (i, j))] + [sp[1] for sp in rspecs],
        out_shape=[jax.ShapeDtypeStruct((m, n), BF16)]
        + [jax.ShapeDtypeStruct(w.shape[1:], BF16) for w, _ in riding],
        compiler_params=_cparams(2),
        name="ffn_up",
    )(h, wg, wu, *[w for w, _ in riding])
    casts, it = [], iter(res[1:])
    for (w, layer), sp in zip(cast_jobs, specs):
        casts.append(next(it) if sp is not None else w[layer].astype(BF16))
    return res[0], casts


def _cond_kernel(c_ref, w_ref, b_ref, tab_ref, o_ref):
    c = c_ref[...]
    act = (c * _sigmoid(c)).astype(BF16)
    cond = _dot(act, w_ref[...].astype(BF16)) + b_ref[...]
    for l in range(tab_ref.shape[0]):
        o_ref[l] = cond + tab_ref[l:l + 1, :]


def cond_table(c_pad, w_ada, b_ada, ada_table2, tn=1024):
    rows, d = c_pad.shape
    n = w_ada.shape[1]
    depth = ada_table2.shape[0]
    tn = _pick_tile(n, tn)
    return pl.pallas_call(
        _cond_kernel,
        grid=(n // tn,),
        in_specs=[pl.BlockSpec((rows, d), lambda j: (0, 0)),
                  pl.BlockSpec((d, tn), lambda j: (0, j)),
                  pl.BlockSpec((1, tn), lambda j: (0, j)),
                  pl.BlockSpec((depth, tn), lambda j: (0, j))],
        out_specs=pl.BlockSpec((depth, rows, tn), lambda j: (0, 0, j)),
        out_shape=jax.ShapeDtypeStruct((depth, rows, n), F32),
        compiler_params=_cparams(1),
        name="cond_table",
    )(c_pad, w_ada, b_ada, ada_table2)


def _modulate_kernel(x_ref, mod_ref, h_ref):
    m = mod_ref[0]
    h_ref[...] = (x_ref[...] * (1.0 + m[1:2, :]) + m[0:1, :]).astype(h_ref.dtype)


def modulate(x2, mod_l, seq, tm=512):
    n, d = x2.shape
    tm = min(tm, seq)
    per_b = seq // tm
    return pl.pallas_call(
        _modulate_kernel,
        grid=(n // tm,),
        in_specs=[pl.BlockSpec((tm, d), lambda i: (i, 0)),
                  pl.BlockSpec((1, N_MOD, d), lambda i: (i // per_b, 0, 0))],
        out_specs=pl.BlockSpec((tm, d), lambda i: (i, 0)),
        out_shape=jax.ShapeDtypeStruct((n, d), BF16),
        compiler_params=_cparams(1),
        name="modulate",
    )(x2, mod_l)


def _ln_mod_kernel(x_ref, mm_ref, mod_ref, nmod_ref, gain_ref, bias_ref, xo_ref, *h_refs,
                   alpha, gate_row, next_row):
    m = mod_ref[0]
    z = alpha * x_ref[...] + (1.0 + m[gate_row:gate_row + 1, :]) * mm_ref[...].astype(F32)
    mu = jnp.mean(z, axis=-1, keepdims=True)
    zc = z - mu
    var = jnp.mean(zc * zc, axis=-1, keepdims=True)
    y = zc * lax.rsqrt(var + LN_EPS) * gain_ref[...] + bias_ref[...]
    xo_ref[...] = y
    if h_refs:
        nm = nmod_ref[0]
        h_refs[0][...] = (y * (1.0 + nm[next_row + 1:next_row + 2, :])
                          + nm[next_row:next_row + 1, :]).astype(BF16)


def ln_mod(x2, mm, mod_l, mod_next, gain, bias, seq, *, alpha, gate_row, next_row, with_h, tm=256):
    n, d = x2.shape
    tm = min(tm, seq)
    per_b = seq // tm
    row = pl.BlockSpec((tm, d), lambda i: (i, 0))
    modspec = pl.BlockSpec((1, N_MOD, d), lambda i: (i // per_b, 0, 0))
    vec = pl.BlockSpec((1, d), lambda i: (0, 0))
    out_shape = [jax.ShapeDtypeStruct((n, d), F32)]
    out_specs = [row]
    if with_h:
        out_shape.append(jax.ShapeDtypeStruct((n, d), BF16))
        out_specs.append(row)
    res = pl.pallas_call(
        functools.partial(_ln_mod_kernel, alpha=alpha, gate_row=gate_row, next_row=next_row),
        grid=(n // tm,),
        in_specs=[row, row, modspec, modspec, vec, vec],
        out_specs=out_specs,
        out_shape=out_shape,
        compiler_params=_cparams(1),
        name="ln_mod",
    )(x2, mm, mod_l, mod_next, gain, bias)
    return (res[0], res[1]) if with_h else (res[0], None)


def _gla_levels(chunk):
    lv = []
    h = chunk // 2
    while h >= 1:
        lv.append(h)
        h //= 2
    return lv


def _gla_sum_matrix(chunk):
    blocks = []
    idx = np.arange(chunk)
    for h in _gla_levels(chunk):
        m = np.zeros((chunk, chunk), np.float32)
        for i in range(chunk):
            r = (i // (2 * h)) * 2 * h + h - 1
            if i % (2 * h) >= h:
                m[i, r + 1:i + 1] = 1.0
            else:
                m[i, i + 1:r + 1] = 1.0
        blocks.append(m)
    blocks.append((idx[None, :] <= idx[:, None]).astype(np.float32))
    m = np.concatenate(blocks, axis=0)
    return np.concatenate([m, m], axis=1)


def _gla_pair_masks(chunk):
    ngrp = chunk // SUBLANES
    col = lax.broadcasted_iota(jnp.int32, (SUBLANES, chunk), 1)
    rows = [lax.broadcasted_iota(jnp.int32, (SUBLANES, chunk), 0) + r * SUBLANES for r in range(ngrp)]
    diag = [row == col for row in rows]
    pair = []
    for h in _gla_levels(chunk):
        blk = 2 * h
        pair.append([((row // blk) == (col // blk)) & ((row % blk) >= h) & ((col % blk) < h) for row in rows])
    rid = lax.broadcasted_iota(jnp.int32, (chunk, 1), 0)
    right = [(rid % (2 * h)) >= h for h in _gla_levels(chunk)]
    return diag, pair, right


def _row_groups(x, groups):
    parts, start, prev = [], None, None
    for r in groups:
        if start is None:
            start = r
        elif r != prev + 1:
            parts.append(x[start * SUBLANES:(prev + 1) * SUBLANES])
            start = r
        prev = r
    parts.append(x[start * SUBLANES:(prev + 1) * SUBLANES])
    return parts


def _gla_chunk_heads(q, k, v, g, s_ref, msum, masks, chunk, dk, dv, heads):
    levels = _gla_levels(chunk)
    nl = len(levels)
    ngrp = chunk // SUBLANES
    g2 = g * LOG2_E
    g_hi = g2.astype(BF16)
    g_lo = (g2 - g_hi.astype(F32)).astype(BF16)
    expo = _dot(msum, jnp.concatenate([g_hi, g_lo], axis=0))
    e_lv = jnp.exp2(expo[:nl * chunk, :])
    b_cum = expo[nl * chunk:, :]
    e_cum_all = jnp.exp2(b_cum)
    e_rev_all = jnp.exp2(b_cum[chunk - 1:chunk, :] - b_cum)

    diag, pair, right = masks
    outs = []
    for hd in range(heads):
        ks = slice(hd * dk, (hd + 1) * dk)
        qh, kh = q[:, ks], k[:, ks]
        vh = v[:, hd * dv:(hd + 1) * dv]
        dsum = jnp.sum(qh * kh, axis=-1, keepdims=True)
        a_rows = [jnp.where(diag[r], dsum[r * SUBLANES:(r + 1) * SUBLANES], 0.0) for r in range(ngrp)]
        for li, h in enumerate(levels):
            e = e_lv[li * chunk:(li + 1) * chunk, ks]
            if h >= SUBLANES:
                rgt = [r for r in range(ngrp) if (r * SUBLANES) % (2 * h) >= h]
                runs, cur = [], None
                for r in range(ngrp):
                    src = r in rgt
                    if cur is None or cur[0] != src:
                        cur = [src, r, r]
                        runs.append(cur)
                    else:
                        cur[2] = r
                sel = jnp.concatenate([(qh if src else kh)[a * SUBLANES:(b + 1) * SUBLANES]
                                       for src, a, b in runs], axis=0)
                prod = sel * e
                lhs = jnp.concatenate(_row_groups(prod, rgt), axis=0).astype(BF16)
                p = _dot_nt(lhs, prod.astype(BF16))
                for n_, r in enumerate(rgt):
                    a_rows[r] = jnp.where(pair[li][r], p[n_ * SUBLANES:(n_ + 1) * SUBLANES], a_rows[r])
            else:
                xb = (jnp.where(right[li], qh, kh) * e).astype(BF16)
                p = _dot_nt(xb, xb)
                a_rows = [jnp.where(pair[li][r], p[r * SUBLANES:(r + 1) * SUBLANES], a_rows[r])
                          for r in range(ngrp)]
        a = jnp.concatenate(a_rows, axis=0)
        e_cum = e_cum_all[:, ks]
        s_t = s_ref[hd]
        vt = vh.T.astype(BF16)
        o = _dot_nt(jnp.concatenate([a.astype(BF16), (qh * e_cum).astype(BF16)], axis=1),
                    jnp.concatenate([vt, s_t.astype(BF16)], axis=1))
        kb = (kh * e_rev_all[:, ks]).astype(BF16)
        s_ref[hd] = s_t * e_cum[chunk - 1:chunk, :] + _dot(vt, kb)
        outs.append(o)
    return outs


def _hgrn_kernel(q_ref, f_ref, i_ref, g_ref, lbraw_ref, gain_ref, bias_ref, msum_ref,
                 o_ref, s_ref, *, layer, heads, chunk):
    dk = dv = HG_HEAD
    tb = q_ref.shape[0]

    @pl.when(pl.program_id(2) == 0)
    def _():
        s_ref[...] = jnp.zeros_like(s_ref)

    raw = lbraw_ref[...]
    ex = jnp.exp(raw - jnp.max(raw, axis=0, keepdims=True))
    sm = ex / jnp.sum(ex, axis=0, keepdims=True)
    lb = jnp.zeros_like(sm[0:1, :])
    for m in range(1, layer + 1):
        lb = lb + sm[m:m + 1, :]
    gain = gain_ref[...]
    bias = bias_ref[...]
    msum = msum_ref[...]
    masks = _gla_pair_masks(chunk)

    def body(c, carry):
        rows = pl.ds(c * chunk, chunk)
        q_in = q_ref[rows, :]
        f = lb + (1.0 - lb) * _sigmoid(f_ref[rows, :])
        q = q_in * _sigmoid(q_in)
        outs = _gla_chunk_heads(q, 1.0 - f, i_ref[rows, :], jnp.log(f), s_ref, msum, masks,
                                chunk, dk, dv, heads)
        gate = _sigmoid(g_ref[rows, :])
        for hd, o in enumerate(outs):
            cs = slice(hd * dv, (hd + 1) * dv)
            y = gate[:, cs] * o
            mu = jnp.mean(y, axis=-1, keepdims=True)
            yc = y - mu
            var = jnp.mean(yc * yc, axis=-1, keepdims=True)
            yn = yc * lax.rsqrt(var + LN_EPS)
            o_ref[rows, cs] = (yn * gain[:, cs] + bias[:, cs]).astype(o_ref.dtype)
        return carry

    for c in range(tb // chunk):
        body(c, 0)


def hgrn2(proj, col0, lb_raw, gain, bias, *, layer, batch, seq, heads_per_block=6, tb=1024):
    n = proj.shape[0]
    w = lb_raw.shape[1]
    nheads = w // HG_HEAD
    hpb = heads_per_block
    while nheads % hpb:
        hpb -= 1
    bw = hpb * HG_HEAD
    ngrp = nheads // hpb
    assert col0 % bw == 0
    cb = col0 // bw
    tb = min(tb, seq)
    chunk = min(GLA_CHUNK, tb)
    nt = seq // tb
    msum = jnp.asarray(_gla_sum_matrix(chunk), BF16)

    def sec(s):
        return pl.BlockSpec((tb, bw), lambda b, hg, t: (b * nt + t, cb + s * ngrp + hg))

    vec = pl.BlockSpec((1, bw), lambda b, hg, t: (0, hg))
    return pl.pallas_call(
        functools.partial(_hgrn_kernel, layer=layer, heads=hpb, chunk=chunk),
        grid=(batch, ngrp, nt),
        in_specs=[sec(0), sec(1), sec(2), sec(3),
                  pl.BlockSpec((lb_raw.shape[0], bw), lambda b, hg, t: (0, hg)),
                  vec, vec,
                  pl.BlockSpec(msum.shape, lambda b, hg, t: (0, 0))],
        out_specs=pl.BlockSpec((tb, bw), lambda b, hg, t: (b * nt + t, hg)),
        out_shape=jax.ShapeDtypeStruct((n, w), BF16),
        scratch_shapes=[pltpu.VMEM((hpb, HG_HEAD, HG_HEAD), F32)],
        compiler_params=_cparams(3),
        name="hgrn2",
    )(proj, proj, proj, proj, lb_raw, gain, bias, msum)


def _gla_kernel(q_ref, k_ref, v_ref, g_ref, wg_ref, bg_ref, gain_ref, msum_ref,
                o_ref, s_ref, *, heads, chunk, dk, dv, q_scale):
    tb = q_ref.shape[0]

    @pl.when(pl.program_id(2) == 0)
    def _():
        s_ref[...] = jnp.zeros_like(s_ref)

    wg = wg_ref[...]
    bg = bg_ref[...]
    gain = gain_ref[...]
    msum = msum_ref[...]
    masks = _gla_pair_masks(chunk)

    def body(c, carry):
        rows = pl.ds(c * chunk, chunk)
        q_raw = q_ref[rows, :]
        q_bf = q_raw.astype(BF16)
        pre = jnp.concatenate([_dot(q_bf[:, hd * dk:(hd + 1) * dk], wg[:, hd * dk:(hd + 1) * dk])
                               for hd in range(heads)], axis=1) + bg
        log_a = (jnp.minimum(pre, 0.0) - jnp.log(1.0 + jnp.exp(-jnp.abs(pre)))) / GLA_GATE_TEMP
        outs = _gla_chunk_heads(q_raw * q_scale, k_ref[rows, :], v_ref[rows, :], log_a,
                                s_ref, msum, masks, chunk, dk, dv, heads)
        g_in = g_ref[rows, :]
        swish = g_in * _sigmoid(g_in)
        for hd, o in enumerate(outs):
            cs = slice(hd * dv, (hd + 1) * dv)
            y = o * lax.rsqrt(jnp.mean(o * o, axis=-1, keepdims=True) + LN_EPS)
            o_ref[rows, cs] = (y * gain[:, cs] * swish[:, cs]).astype(o_ref.dtype)
        return carry

    for c in range(tb // chunk):
        body(c, 0)


def gla(proj, q0, k0, v0, g0, w_gate, b_gate, gain, *, batch, seq, dk, dv, q_scale,
        heads_per_block=4, tb=512):
    n = proj.shape[0]
    nheads = w_gate.shape[1] // dk
    hpb = min(heads_per_block, nheads)
    ngrp = nheads // hpb
    tb = min(tb, seq)
    chunk = min(GLA_CHUNK, tb)
    nt = seq // tb
    msum = jnp.asarray(_gla_sum_matrix(chunk), BF16)
    kw, vw = hpb * dk, hpb * dv
    assert q0 % kw == 0 and k0 % kw == 0 and v0 % vw == 0 and g0 % vw == 0

    def rows(width, col0):
        cb = col0 // width
        return pl.BlockSpec((tb, width), lambda b, hg, t: (b * nt + t, cb + hg))

    return pl.pallas_call(
        functools.partial(_gla_kernel, heads=hpb, chunk=chunk, dk=dk, dv=dv, q_scale=q_scale),
        grid=(batch, ngrp, nt),
        in_specs=[rows(kw, q0), rows(kw, k0), rows(vw, v0), rows(vw, g0),
                  pl.BlockSpec((w_gate.shape[0], kw), lambda b, hg, t: (0, hg)),
                  pl.BlockSpec((1, kw), lambda b, hg, t: (0, hg)),
                  pl.BlockSpec((1, vw), lambda b, hg, t: (0, hg)),
                  pl.BlockSpec(msum.shape, lambda b, hg, t: (0, 0))],
        out_specs=pl.BlockSpec((tb, vw), lambda b, hg, t: (b * nt + t, hg)),
        out_shape=jax.ShapeDtypeStruct((n, nheads * dv), BF16),
        scratch_shapes=[pltpu.VMEM((hpb, dv, dk), F32)],
        compiler_params=_cparams(3),
        name="gla",
    )(proj, proj, proj, proj, w_gate, b_gate, gain, msum)


S5_GB = LANES // S5_GROUP


def _s5_scan_steps(nchunks):
    return max(1, int(math.ceil(math.log2(nchunks)))) if nchunks > 1 else 0


def _s5_prep_kernel(ar_ref, ai_ref, ldt_ref, b2_ref, c2_ref, d_ref,
                    ktoep_ref, win_ref, wo_ref, lscan_ref, *, nsteps):
    rows, p2 = ar_ref.shape
    half = p2 // 2
    t_sub = S5_T
    ar = ar_ref[...]
    ai = ai_ref[...]
    dt = jnp.exp(ldt_ref[...])
    lane = lax.broadcasted_iota(jnp.int32, (1, p2), 1)
    sgn_im = jnp.where(lane < half, -1.0, 1.0)
    sgn_re = -sgn_im

    def lam_pow(k):
        mag = jnp.exp(float(k) * (ar * dt))
        th = float(k) * (ai * dt)
        return mag * jnp.cos(th), mag * jnp.sin(th)

    pows = [lam_pow(k) for k in range(t_sub + 1)]

    def cmul(x, k):
        l_re, l_im = pows[k]
        return x * l_re + pltpu.roll(x, half, axis=1) * (l_im * sgn_im)

    lam_re, lam_im = pows[1]
    den = ar * ar + ai * ai
    nr = lam_re - 1.0
    ni = lam_im
    coef_re = (nr * ar + ni * ai) / den
    coef_im = (ni * ar - nr * ai) / den
    b2 = b2_ref[...]
    bbar = b2 * coef_re + pltpu.roll(b2, half, axis=1) * (coef_im * sgn_im)
    c2 = c2_ref[...]

    rgrp = lax.broadcasted_iota(jnp.int32, (rows, rows), 0) // S5_GROUP
    cgrp = lax.broadcasted_iota(jnp.int32, (rows, rows), 1) // S5_GROUP
    same_grp = rgrp == cgrp
    r_i = lax.broadcasted_iota(jnp.int32, (rows, rows), 0)
    c_i = lax.broadcasted_iota(jnp.int32, (rows, rows), 1)
    hp = lax.Precision.HIGHEST

    for j in range(t_sub):
        k = t_sub - 1 - j
        tap = jnp.where(same_grp, _dot_nt(cmul(bbar, k) * sgn_re, c2, hp), 0.0)
        if k == 0:
            tap = tap + jnp.where(r_i == c_i, d_ref[...], 0.0)
        ktoep_ref[j * rows:(j + 1) * rows, :] = tap.astype(ktoep_ref.dtype)

    grp_of_row = lax.broadcasted_iota(jnp.int32, (rows, p2), 0) // S5_GROUP

    def block_diag(tile):
        return jnp.concatenate([jnp.where(grp_of_row == gg, tile, 0.0) for gg in range(S5_GB)], axis=1)

    for s in range(t_sub):
        win_ref[s * rows:(s + 1) * rows, :] = block_diag(cmul(bbar, t_sub - 1 - s)).astype(win_ref.dtype)
        wo_ref[s * rows:(s + 1) * rows, :] = block_diag(cmul(c2, s + 1) * sgn_re).astype(wo_ref.dtype)

    def group_rows(tile):
        return jnp.concatenate([tile[gg * S5_GROUP:gg * S5_GROUP + 1, :] for gg in range(S5_GB)], axis=1)

    cur_re, cur_im = pows[t_sub]
    rows_re, rows_sw = [], []
    for _ in range(nsteps):
        rows_re.append(group_rows(cur_re))
        rows_sw.append(group_rows(cur_im * sgn_im))
        cur_re, cur_im = cur_re * cur_re - cur_im * cur_im, 2.0 * cur_re * cur_im
    pad = lscan_ref.shape[0] - 2 * nsteps
    parts = rows_re + rows_sw + ([jnp.zeros((pad, S5_GB * p2), F32)] if pad else [])
    lscan_ref[...] = jnp.concatenate(parts, axis=0)


def s5_prep(ar_rows, ai_rows, ldt_rows, b2_rows, c2_rows, d_row, nsteps):
    wd, p2 = ar_rows.shape
    nblk = wd // LANES
    lrows = ((2 * nsteps + 7) // 8) * 8
    tile = pl.BlockSpec((LANES, p2), lambda i: (i, 0))
    return pl.pallas_call(
        functools.partial(_s5_prep_kernel, nsteps=nsteps),
        grid=(nblk,),
        in_specs=[tile, tile, tile, tile, tile, pl.BlockSpec((1, LANES), lambda i: (0, i))],
        out_specs=[pl.BlockSpec((None, S5_T * LANES, LANES), lambda i: (i, 0, 0)),
                   pl.BlockSpec((None, S5_T * LANES, S5_GB * p2), lambda i: (i, 0, 0)),
                   pl.BlockSpec((None, S5_T * LANES, S5_GB * p2), lambda i: (i, 0, 0)),
                   pl.BlockSpec((None, lrows, S5_GB * p2), lambda i: (i, 0, 0))],
        out_shape=[jax.ShapeDtypeStruct((nblk, S5_T * LANES, LANES), BF16),
                   jax.ShapeDtypeStruct((nblk, S5_T * LANES, S5_GB * p2), BF16),
                   jax.ShapeDtypeStruct((nblk, S5_T * LANES, S5_GB * p2), BF16),
                   jax.ShapeDtypeStruct((nblk, lrows, S5_GB * p2), F32)],
        compiler_params=_cparams(1),
        name="s5_prep",
    )(ar_rows, ai_rows, ldt_rows, b2_rows, c2_rows, d_row)


def _s5_main_kernel(u_ref, ktoep_ref, win_ref, wo_ref, lscan_ref, y_ref, *, nsteps):
    t_sub = S5_T
    nch = u_ref.shape[0] // t_sub
    p2 = win_ref.shape[1] // S5_GB
    half = p2 // 2
    xcat = jnp.concatenate([u_ref[pl.ds(s, nch, stride=t_sub), :].astype(BF16) for s in range(t_sub)],
                           axis=1)
    z = _dot(xcat, win_ref[...])
    pos = lax.broadcasted_iota(jnp.int32, (nch, p2), 0)
    lscan = lscan_ref[...]
    xprev = []
    for gg in range(S5_GB):
        cols = slice(gg * p2, (gg + 1) * p2)
        x = z[:, cols]
        for j in range(nsteps):
            d = 1 << j
            sh = jnp.where(pos >= d, pltpu.roll(x, d, axis=0), 0.0)
            x = (x + sh * lscan[j:j + 1, cols]
                 + pltpu.roll(sh, half, axis=1) * lscan[nsteps + j:nsteps + j + 1, cols])
        xprev.append(jnp.where(pos >= 1, pltpu.roll(x, 1, axis=0), 0.0).astype(BF16))
    y_state = _dot_nt(jnp.concatenate(xprev, axis=1), wo_ref[...])
    zero_blk = jnp.zeros((LANES, LANES), ktoep_ref.dtype)
    for t in range(0, t_sub, 2):
        taps = jnp.concatenate(
            [jnp.concatenate([ktoep_ref[(t_sub - 1 - t) * LANES:, :], zero_blk], axis=0),
             ktoep_ref[(t_sub - 2 - t) * LANES:, :]], axis=1)
        y_pair = y_state[:, t * LANES:(t + 2) * LANES] + _dot(xcat[:, :(t + 2) * LANES], taps)
        y_ref[pl.ds(t, nch, stride=t_sub), :] = y_pair[:, :LANES]
        y_ref[pl.ds(t + 1, nch, stride=t_sub), :] = y_pair[:, LANES:]


def s5_main(proj, col0, wd, ktoep, win, wo, lscan, *, nsteps, batch, seq):
    n = proj.shape[0]
    nblk = wd // LANES
    assert col0 % LANES == 0
    cb = col0 // LANES

    def wspec(a):
        return pl.BlockSpec((None,) + a.shape[1:], lambda i, b: (i, 0, 0))

    return pl.pallas_call(
        functools.partial(_s5_main_kernel, nsteps=nsteps),
        grid=(nblk, batch),
        in_specs=[pl.BlockSpec((seq, LANES), lambda i, b: (b, cb + i)),
                  wspec(ktoep), wspec(win), wspec(wo), wspec(lscan)],
        out_specs=pl.BlockSpec((seq, LANES), lambda i, b: (b, i)),
        out_shape=jax.ShapeDtypeStruct((n, wd), F32),
        compiler_params=_cparams(2),
        name="s5_main",
    )(proj, ktoep, win, wo, lscan)


def _s5_glu_kernel(y_ref, w_ref, b_ref, o_ref, wbf_ref):
    @pl.when(pl.program_id(0) == 0)
    def _():
        wbf_ref[...] = w_ref[...].astype(BF16)

    y = y_ref[...]
    z = 0.5 * y * (1.0 + jnp.tanh(math.sqrt(2.0 / math.pi) * (y + 0.044715 * (y * y * y))))
    gate = _dot(z.astype(BF16), wbf_ref[...]) + b_ref[...]
    o_ref[...] = (z * _sigmoid(gate)).astype(o_ref.dtype)


def s5_glu(y, w_stack, layer, b, tm=2048):
    n, wd = y.shape
    tm = min(tm, n)
    return pl.pallas_call(
        _s5_glu_kernel,
        grid=(n // tm,),
        in_specs=[pl.BlockSpec((tm, wd), lambda i: (i, 0)),
                  pl.BlockSpec((None, wd, wd), lambda i: (layer, 0, 0)),
                  pl.BlockSpec((1, wd), lambda i: (0, 0))],
        out_specs=pl.BlockSpec((tm, wd), lambda i: (i, 0)),
        out_shape=jax.ShapeDtypeStruct((n, wd), BF16),
        scratch_shapes=[pltpu.VMEM((wd, wd), BF16)],
        compiler_params=_cparams(1),
        name="s5_glu",
    )(y, w_stack, b)


def s5_mixer(proj, col0, a_re, a_im, log_dt, b_re, b_im, c_re, c_im, d_skip, glu_w_stack, layer, glu_b,
             *, batch, seq):
    wd = d_skip.shape[0]
    g, p = a_re.shape
    assert wd // g == S5_GROUP and seq % S5_T == 0 and wd % LANES == 0
    nsteps = _s5_scan_steps(seq // S5_T)
    per_row = lambda a: jnp.repeat(jnp.concatenate([a, a], axis=-1), S5_GROUP, axis=0)
    ldt_rows = jnp.broadcast_to(jnp.repeat(log_dt, S5_GROUP)[:, None], (wd, 2 * p))
    b2_rows = jnp.concatenate([b_re.transpose(0, 2, 1), b_im.transpose(0, 2, 1)], axis=-1).reshape(wd, 2 * p)
    c2_rows = jnp.concatenate([c_re, c_im], axis=-1).reshape(wd, 2 * p)
    ktoep, win, wo, lscan = s5_prep(per_row(a_re), per_row(a_im), ldt_rows, b2_rows, c2_rows,
                                    d_skip[None, :], nsteps)
    y = s5_main(proj, col0, wd, ktoep, win, wo, lscan, nsteps=nsteps, batch=batch, seq=seq)
    return s5_glu(y, glu_w_stack, layer, glu_b[None, :])


def _pad_heads(w, heads, width, new):
    r = w.shape[0]
    return jnp.pad(w.reshape(r, heads, width), ((0, 0), (0, 0), (0, new - width))).reshape(r, heads * new)


def kernel(x, c, w_ada, b_ada, ada_table, w_in, w_out, s5_a_re, s5_a_im, s5_log_dt, s5_b_re, s5_b_im, s5_c_re, s5_c_im, s5_d, s5_glu_w, s5_glu_b, hg_lb_raw, hg_norm_gain, hg_norm_bias, gla_w_gate, gla_b_gate, gla_norm_gain, w_ffn_gate, w_ffn_up, w_ffn_down, ln1_gain, ln1_bias, ln2_gain, ln2_bias):
    bsz, seq, d = x.shape
    depth = w_in.shape[0]
    n = bsz * seq
    s5_w = s5_d.shape[1]
    hg_w = hg_lb_raw.shape[1]
    gla_kw = gla_b_gate.shape[1]
    gla_vw = gla_norm_gain.shape[1]
    rank = gla_w_gate.shape[1]
    gla_dk = gla_kw // GLA_HEADS
    gla_dv = gla_vw // GLA_HEADS
    dk_pad = ((gla_dk + GLA_DK_PAD - 1) // GLA_DK_PAD) * GLA_DK_PAD
    assert dk_pad - gla_dk >= rank
    alpha = (2.0 * depth) ** 0.25

    rows = ((bsz + 7) // 8) * 8
    c_pad = jnp.pad(c, ((0, rows - bsz), (0, 0)))
    mod = cond_table(c_pad, w_ada, b_ada[None, :], ada_table.reshape(depth, N_MOD * d))
    mod = mod[:, :bsz].reshape(depth, bsz, N_MOD, d)

    o_u = 0
    o_hg = o_u + s5_w
    o_q = o_hg + 4 * hg_w
    o_k = o_q + gla_kw
    o_v = o_k + gla_kw
    o_lr = o_v + 2 * gla_vw

    hk = GLA_HEADS * dk_pad
    p_v = 0
    p_g = p_v + gla_vw
    p_hg = p_g + gla_vw
    p_u = p_hg + 4 * hg_w
    p_q = p_u + s5_w
    p_k = p_q + hk
    w_t = jnp.transpose(w_in, (0, 2, 1))
    lr_rows = jnp.broadcast_to(w_t[:, None, o_lr:o_lr + rank], (depth, GLA_HEADS, rank, d))
    q_rows = jnp.concatenate([w_t[:, o_q:o_k].reshape(depth, GLA_HEADS, gla_dk, d), lr_rows,
                              jnp.zeros((depth, GLA_HEADS, dk_pad - gla_dk - rank, d), F32)], axis=2)
    k_rows = jnp.pad(w_t[:, o_k:o_v].reshape(depth, GLA_HEADS, gla_dk, d),
                     ((0, 0), (0, 0), (0, dk_pad - gla_dk), (0, 0)))
    w_all = jnp.concatenate([w_t[:, o_v:o_lr], w_t[:, o_hg:o_q], w_t[:, o_u:o_hg],
                             q_rows.reshape(depth, hk, d), k_rows.reshape(depth, hk, d)], axis=1).astype(BF16)
    w_out_bf = w_out[0].astype(BF16)
    w_gate_bf = w_ffn_gate[0].astype(BF16)
    w_up_bf = w_ffn_up[0].astype(BF16)

    x2 = x.reshape(n, d)
    h = modulate(x2, mod[0], seq)
    for l in range(depth):
        proj = matmul_ws([h], w_all, l, F32, tm=1024, w_is_nk=True)
        y_a = s5_mixer(proj, p_u, s5_a_re[l], s5_a_im[l], s5_log_dt[l], s5_b_re[l],
                       s5_b_im[l], s5_c_re[l], s5_c_im[l], s5_d[l], s5_glu_w, l, s5_glu_b[l],
                       batch=bsz, seq=seq)
        y_b = hgrn2(proj, p_hg, hg_lb_raw, hg_norm_gain[l][None, :], hg_norm_bias[l][None, :],
                    layer=l, batch=bsz, seq=seq)
        w_gate = jnp.pad(
            jnp.pad(gla_w_gate[l].reshape(rank, GLA_HEADS, gla_dk), ((0, 0), (0, 0), (0, dk_pad - gla_dk))),
            ((gla_dk, dk_pad - gla_dk - rank), (0, 0), (0, 0))).reshape(dk_pad, hk).astype(BF16)
        b_gate = _pad_heads(gla_b_gate[l][None, :], GLA_HEADS, gla_dk, dk_pad)
        y_c = gla(proj, p_q, p_k, p_v, p_g, w_gate, b_gate, gla_norm_gain[l][None, :],
                  batch=bsz, seq=seq, dk=dk_pad, dv=gla_dv, q_scale=float(gla_dk) ** -0.5)
        mixed = matmul_ws([y_a, y_b, y_c], w_out_bf[None], 0, BF16, tm=1024)
        x2, h = ln_mod(x2, mixed, mod[l], mod[l], ln1_gain[l][None, :], ln1_bias[l][None, :], seq,
                       alpha=alpha, gate_row=2, next_row=3, with_h=True)

        last = l == depth - 1
        jobs = [(w_ffn_down, l)] + ([] if last else [(w_ffn_gate, l + 1), (w_ffn_up, l + 1), (w_out, l + 1)])
        act, casts = ffn_up(h, w_gate_bf, w_up_bf, jobs)
        if not last:
            w_gate_bf, w_up_bf, w_out_bf = casts[1], casts[2], casts[3]
        ffn = matmul_ws([act], casts[0][None], 0, BF16, tn=1024, w_buffers=1)
        x2, h = ln_mod(x2, ffn, mod[l], mod[l if last else l + 1],
                       ln2_gain[l][None, :], ln2_bias[l][None, :], seq,
                       alpha=alpha, gate_row=5, next_row=0, with_h=not last)
    return x2.reshape(bsz, seq, d)
```
